```python
import math
import jax, jax.numpy as jnp
from jax import lax
import numpy as np

D_MODEL = 1024
BATCH = 8
SEQ = 4096
DEPTH = 2

GRID_W = 64
CTX_LEN = 256
MIX_W = D_MODEL
GROUP_W = MIX_W // 4
HEAD_DIM = 64
A_HEADS = GROUP_W // HEAD_DIM
A_KV_HEADS = 2
CONV_W = 31
FNET_GROUPS = 4
D_HEADS = 4
D_VDIM = GROUP_W // D_HEADS
D_QKDIM = D_VDIM // 2
N_EXPERTS = 32
TOP_K = 4
D_FF = D_MODEL
SWIGLU_LIMIT = 7.0
SWIGLU_ALPHA = 1.702
MOE_BLOCK = 256
Q_BLOCK = 128
ROPE_THETA = 10000.0
EPS = 1e-6

A_Q_W = A_HEADS * HEAD_DIM
A_KV_W = A_KV_HEADS * HEAD_DIM
D_QK_W = D_HEADS * 2 * D_QKDIM
D_V_W = D_HEADS * D_VDIM
IN_WIDTHS = (A_Q_W, A_KV_W, A_KV_W, GROUP_W, GROUP_W, GROUP_W, D_QK_W, D_QK_W, D_V_W)
D_IN = A_Q_W + 2 * A_KV_W + 3 * GROUP_W + 2 * D_QK_W + D_V_W

kernel_name = 'hybrid_diffusion_parallel_groups_moe'


def rms_norm(x, g):
    xf = x.astype(jnp.float32)
    y = xf * lax.rsqrt(jnp.mean(xf * xf, axis=-1, keepdims=True) + EPS)
    return (y * g.astype(jnp.float32)).astype(x.dtype)


def layer_norm(x, g, b):
    xf = x.astype(jnp.float32)
    mu = jnp.mean(xf, axis=-1, keepdims=True)
    xc = xf - mu
    y = xc * lax.rsqrt(jnp.mean(xc * xc, axis=-1, keepdims=True) + EPS)
    return (y * g.astype(jnp.float32) + b.astype(jnp.float32)).astype(x.dtype)


def modulate(h, shift, scale):
    return h * (1 + scale) + shift


def split_cols(p):
    out, start = [], 0
    for w in IN_WIDTHS:
        out.append(p[..., start:start + w])
        start += w
    return out


def rope_rotate(x, pos):
    n = x.shape[-1]
    inv = jnp.power(ROPE_THETA, -jnp.arange(0, n, 2, dtype=jnp.float32) / n)
    ang = pos.astype(jnp.float32)[:, None] * inv[None, :]
    shape = (ang.shape[0],) + (1,) * (x.ndim - 3) + (n // 2,)
    cos = jnp.cos(ang).reshape(shape)
    sin = jnp.sin(ang).reshape(shape)
    xf = x.astype(jnp.float32)
    x1, x2 = xf[..., : n // 2], xf[..., n // 2:]
    return jnp.concatenate([x1 * cos - x2 * sin, x1 * sin + x2 * cos], axis=-1).astype(x.dtype)


def axial_rope(x, rows, cols):
    half = x.shape[-1] // 2
    return jnp.concatenate([rope_rotate(x[..., :half], rows), rope_rotate(x[..., half:], cols)], axis=-1)


def attn_heads(aq, ak, av, dq, dk, dv, a_qg, a_kg, d_qg, d_kg, rows, cols):
    B_, T_ = aq.shape[:2]
    aq = rms_norm(aq.reshape(B_, T_, A_KV_HEADS, A_HEADS // A_KV_HEADS, HEAD_DIM), a_qg)
    ak = rms_norm(ak.reshape(B_, T_, A_KV_HEADS, HEAD_DIM), a_kg)
    av = av.reshape(B_, T_, A_KV_HEADS, HEAD_DIM)
    dq = rms_norm(dq.reshape(B_, T_, D_HEADS, 2, D_QKDIM), d_qg)
    dk = rms_norm(dk.reshape(B_, T_, D_HEADS, 2, D_QKDIM), d_kg)
    dv = dv.reshape(B_, T_, D_HEADS, D_VDIM)
    if rows is not None:
        aq, ak = axial_rope(aq, rows, cols), axial_rope(ak, rows, cols)
        dq, dk = axial_rope(dq, rows, cols), axial_rope(dk, rows, cols)
    return aq, ak, av, dq, dk, dv


def gqa_block(q, k, v):
    s = jnp.einsum('bqhgd,bkhd->bhgqk', q, k).astype(jnp.float32) * (HEAD_DIM ** -0.5)
    p = jax.nn.softmax(s, axis=-1).astype(v.dtype)
    return jnp.einsum('bhgqk,bkhd->bqhgd', p, v)


def diff_block(q, k, v, lam):
    s = jnp.einsum('bqhcd,bkhcd->bchqk', q, k).astype(jnp.float32) * (D_QKDIM ** -0.5)
    p = jax.nn.softmax(s, axis=-1)
    a = (p[:, 0] - lam * p[:, 1]).astype(v.dtype)
    return jnp.einsum('bhqk,bkhd->bqhd', a, v)


def diff_out(o, g, lam_init):
    B_, T_ = o.shape[:2]
    return (rms_norm(o, g) * (1.0 - lam_init)).reshape(B_, T_, GROUP_W)


def sweep_queries(fn, q):
    B_, L_ = q.shape[:2]
    nb = L_ // Q_BLOCK
    qb = jnp.moveaxis(q.reshape((B_, nb, Q_BLOCK) + q.shape[2:]), 1, 0)
    ob = jnp.moveaxis(lax.map(fn, qb), 0, 1)
    return ob.reshape((B_, L_) + ob.shape[3:])


def conformer_conv(u, gate, w_dw, b_dw, ln_g, ln_b):
    z = u * jax.nn.sigmoid(gate)
    z = lax.conv_general_dilated(
        z, w_dw[:, None, :], window_strides=(1,), padding=[(CONV_W // 2, CONV_W // 2)],
        dimension_numbers=('NWC', 'WIO', 'NWC'), feature_group_count=z.shape[-1]) + b_dw
    return jax.nn.silu(layer_norm(z, ln_g, ln_b))


def fourier_mix(u):
    B_, L_, C_ = u.shape
    uf = u.astype(jnp.float32).reshape(B_, L_, FNET_GROUPS, C_ // FNET_GROUPS)
    y = jnp.fft.fft2(uf, axes=(1, 3), norm='ortho').real
    return y.reshape(B_, L_, C_).astype(u.dtype)


def moe(h, router_w, router_b, w_gu, b_gu, w_down, b_down):
    T = h.shape[0]
    logits = (h @ router_w + router_b).astype(jnp.float32)
    top_val, top_idx = lax.top_k(logits, TOP_K)
    gates = jax.nn.softmax(top_val, axis=-1)
    n_assign = T * TOP_K
    flat_e = top_idx.reshape(-1).astype(jnp.int32)
    flat_tok = jnp.arange(n_assign, dtype=jnp.int32) // TOP_K
    flat_w = gates.reshape(-1)
    order = jnp.argsort(flat_e)
    sorted_e, sorted_tok, sorted_w = flat_e[order], flat_tok[order], flat_w[order]
    counts = jnp.zeros((N_EXPERTS,), jnp.int32).at[flat_e].add(1)
    starts = jnp.cumsum(counts) - counts
    padded = (counts + MOE_BLOCK - 1) // MOE_BLOCK * MOE_BLOCK
    pad_ends = jnp.cumsum(padded)
    pad_starts = pad_ends - padded
    dest = pad_starts[sorted_e] + jnp.arange(n_assign, dtype=jnp.int32) - starts[sorted_e]
    n_blocks = -(-(n_assign + N_EXPERTS * (MOE_BLOCK - 1)) // MOE_BLOCK)
    P = n_blocks * MOE_BLOCK
    slot_tok = jnp.full((P,), T, jnp.int32).at[dest].set(sorted_tok)
    slot_w = jnp.zeros((P,), jnp.float32).at[dest].set(sorted_w)
    block_start = jnp.arange(n_blocks, dtype=jnp.int32) * MOE_BLOCK
    block_e = jnp.minimum(jnp.searchsorted(pad_ends, block_start, side='right'), N_EXPERTS - 1).astype(jnp.int32)
    h_pad = jnp.concatenate([h, jnp.zeros((1, h.shape[1]), h.dtype)], axis=0)

    def expert_block(args):
        tok, e, w = args
        xb = h_pad[tok]
        gu = xb @ w_gu[e] + b_gu[e]
        g, u = gu[:, :D_FF], gu[:, D_FF:]
        g = jnp.minimum(g, SWIGLU_LIMIT)
        u = jnp.clip(u, -SWIGLU_LIMIT, SWIGLU_LIMIT)
        act = g * jax.nn.sigmoid(SWIGLU_ALPHA * g) * (u + 1)
        return (act @ w_down[e] + b_down[e]) * w[:, None].astype(xb.dtype)

    yb = lax.map(expert_block, (slot_tok.reshape(n_blocks, MOE_BLOCK), block_e,
                                slot_w.reshape(n_blocks, MOE_BLOCK)))
    y = jax.ops.segment_sum(yb.reshape(P, -1), slot_tok, num_segments=T + 1)
    return y[:T]


def setup_inputs(seed: int = 0) -> dict:
    key = jax.random.key(seed)
    ks = jax.random.split(key, 26)
    f32 = jnp.float32
    D = D_MODEL

    def nrm(k, shape, scale):
        return jax.random.normal(k, shape, f32) * scale

    return {
        'x': nrm(ks[0], (BATCH, SEQ, D), 1.0),
        'c': nrm(ks[1], (BATCH, D), 1.0),
        'ctx': nrm(ks[2], (BATCH, CTX_LEN, D), 1.0),
        'c_ctx': nrm(ks[3], (D,), 1.0),
        'norm1_g': 1.0 + nrm(ks[4], (DEPTH, D), 0.02),
        'norm2_g': 1.0 + nrm(ks[5], (DEPTH, D), 0.02),
        'ada_w': nrm(ks[6], (DEPTH, D, 6 * D), 0.5 * D ** -0.5),
        'ada_b': nrm(ks[7], (DEPTH, 6 * D), 0.01),
        'w_in': nrm(ks[8], (DEPTH, D, D_IN), D ** -0.5),
        'a_qnorm_g': 1.0 + nrm(ks[9], (DEPTH, HEAD_DIM), 0.02),
        'a_knorm_g': 1.0 + nrm(ks[10], (DEPTH, HEAD_DIM), 0.02),
        'conv_dw_w': nrm(ks[11], (DEPTH, CONV_W, GROUP_W), CONV_W ** -0.5),
        'conv_dw_b': nrm(ks[12], (DEPTH, GROUP_W), 0.01),
        'conv_ln_g': 1.0 + nrm(ks[13], (DEPTH, GROUP_W), 0.02),
        'conv_ln_b': nrm(ks[14], (DEPTH, GROUP_W), 0.01),
        'd_qnorm_g': 1.0 + nrm(ks[15], (DEPTH, D_QKDIM), 0.02),
        'd_knorm_g': 1.0 + nrm(ks[16], (DEPTH, D_QKDIM), 0.02),
        'd_lambda': nrm(ks[17], (DEPTH, 4, D_QKDIM), 0.1),
        'd_subln_g': 1.0 + nrm(ks[18], (DEPTH, D_VDIM), 0.02),
        'w_out': nrm(ks[19], (DEPTH, MIX_W, D), MIX_W ** -0.5),
        'router_w': nrm(ks[20], (DEPTH, D, N_EXPERTS), D ** -0.5),
        'router_b': nrm(ks[21], (DEPTH, N_EXPERTS), 0.01),
        'exp_w_gu': nrm(ks[22], (DEPTH, N_EXPERTS, D, 2 * D_FF), D ** -0.5),
        'exp_b_gu': nrm(ks[23], (DEPTH, N_EXPERTS, 2 * D_FF), 0.01),
        'exp_w_down': nrm(ks[24], (DEPTH, N_EXPERTS, D_FF, D), D_FF ** -0.5),
        'exp_b_down': nrm(ks[25], (DEPTH, N_EXPERTS, D), 0.01),
    }


def reference(x, c, ctx, c_ctx, norm1_g, norm2_g, ada_w, ada_b, w_in, a_qnorm_g, a_knorm_g,
              conv_dw_w, conv_dw_b, conv_ln_g, conv_ln_b, d_qnorm_g, d_knorm_g, d_lambda,
              d_subln_g, w_out, router_w, router_b, exp_w_gu, exp_b_gu, exp_w_down, exp_b_down):
    B_, L, D = x.shape
    C_ = ctx.shape[1]
    ROWS = L // GRID_W
    rows = jnp.repeat(jnp.arange(ROWS, dtype=jnp.int32), GRID_W)
    cols = jnp.tile(jnp.arange(GRID_W, dtype=jnp.int32), ROWS)
    xc = ctx
    for li in range(DEPTH):
        last = li == DEPTH - 1
        mod = jax.nn.silu(c) @ ada_w[li] + ada_b[li]
        mod_c = jax.nn.silu(c_ctx) @ ada_w[li] + ada_b[li]
        sh1, sc1, g1, sh2, sc2, g2 = jnp.split(mod[:, None, :], 6, axis=-1)
        csh1, csc1, cg1, csh2, csc2, cg2 = jnp.split(mod_c[None, None, :], 6, axis=-1)

        h = modulate(rms_norm(x, norm1_g[li]), sh1, sc1)
        hc = modulate(rms_norm(xc, norm1_g[li]), csh1, csc1)
        aq, ak, av, bu, bg, cu, dq, dk, dv = split_cols(h @ w_in[li])
        aqc, akc, avc, buc, bgc, cuc, dqc, dkc, dvc = split_cols(hc @ w_in[li])
        aq, ak, av, dq, dk, dv = attn_heads(aq, ak, av, dq, dk, dv, a_qnorm_g[li], a_knorm_g[li],
                                            d_qnorm_g[li], d_knorm_g[li], rows, cols)
        aqc, akc, avc, dqc, dkc, dvc = attn_heads(aqc, akc, avc, dqc, dkc, dvc, a_qnorm_g[li], a_knorm_g[li],
                                                  d_qnorm_g[li], d_knorm_g[li], None, None)
        lam_init = 0.8 - 0.6 * math.exp(-0.3 * li)
        lq1, lk1, lq2, lk2 = d_lambda[li].astype(jnp.float32)
        lam = jnp.exp(jnp.sum(lq1 * lk1)) - jnp.exp(jnp.sum(lq2 * lk2)) + lam_init

        ka = jnp.concatenate([ak, akc], axis=1)
        va = jnp.concatenate([av, avc], axis=1)
        kd = jnp.concatenate([dk, dkc], axis=1)
        vd = jnp.concatenate([dv, dvc], axis=1)
        o_a = sweep_queries(lambda qb: gqa_block(qb, ka, va), aq).reshape(B_, L, GROUP_W)
        o_b = conformer_conv(bu, bg, conv_dw_w[li], conv_dw_b[li], conv_ln_g[li], conv_ln_b[li])
        o_c = fourier_mix(cu)
        o_d = diff_out(sweep_queries(lambda qb: diff_block(qb, kd, vd, lam), dq), d_subln_g[li], lam_init)
        x = x + g1 * (jnp.concatenate([o_a, o_b, o_c, o_d], axis=-1) @ w_out[li])

        if not last:
            oc_a = gqa_block(aqc, akc, avc).reshape(B_, C_, GROUP_W)
            oc_b = conformer_conv(buc, bgc, conv_dw_w[li], conv_dw_b[li], conv_ln_g[li], conv_ln_b[li])
            oc_c = fourier_mix(cuc)
            oc_d = diff_out(diff_block(dqc, dkc, dvc, lam), d_subln_g[li], lam_init)
            xc = xc + cg1 * (jnp.concatenate([oc_a, oc_b, oc_c, oc_d], axis=-1) @ w_out[li])

        h2 = modulate(rms_norm(x, norm2_g[li]), sh2, sc2).reshape(B_ * L, D)
        if last:
            y = moe(h2, router_w[li], router_b[li], exp_w_gu[li], exp_b_gu[li], exp_w_down[li], exp_b_down[li])
            x = x + g2 * y.reshape(B_, L, D)
        else:
            h2c = modulate(rms_norm(xc, norm2_g[li]), csh2, csc2).reshape(B_ * C_, D)
            y = moe(jnp.concatenate([h2, h2c], axis=0), router_w[li], router_b[li], exp_w_gu[li],
                    exp_b_gu[li], exp_w_down[li], exp_b_down[li])
            x = x + g2 * y[: B_ * L].reshape(B_, L, D)
            xc = xc + cg2 * y[B_ * L:].reshape(B_, C_, D)
    return x
```

```python
import functools
import math

import jax
import jax.numpy as jnp
from jax import lax
from jax.experimental import pallas as pl
from jax.experimental.pallas import tpu as pltpu

F32 = jnp.float32
BF16 = jnp.bfloat16
I32 = jnp.int32

D_MODEL = 1024
TILE = 256
LANES = 128
GRID_W = 64
HEAD_DIM = 64
A_HEADS = 4
A_KV_HEADS = 2
CONV_W = 31
CONV_HALO = 16
CONV_ROWS = 128
FNET_GROUPS = 4
D_HEADS = 4
D_QKDIM = 32
D_VDIM = 64
GROUP_W = 256
N_EXPERTS = 32
TOP_K = 4
D_FF = D_MODEL
SWIGLU_LIMIT = 7.0
SWIGLU_ALPHA = 1.702
MOE_BLOCK = 256
KEY_CHUNK = 256
ROPE_THETA = 10000.0
EPS = 1e-6
NEG_BIG = -1e30
SUBLANES = 8
ROW_SUB = D_MODEL // LANES
assert ROW_SUB == SUBLANES
ROW_TILE = TILE
ROW_PITCH = ROW_TILE + SUBLANES

PROJ_W = (A_HEADS * LANES, A_KV_HEADS * LANES, A_KV_HEADS * LANES, GROUP_W, GROUP_W, GROUP_W,
          D_HEADS * LANES, D_HEADS * LANES, D_HEADS * LANES)
PROJ_OFF = tuple(sum(PROJ_W[:i]) for i in range(len(PROJ_W) + 1))
OUT_K = A_HEADS * LANES + 2 * GROUP_W + D_HEADS * LANES

VMEM_LIMIT = 52 * 1024 * 1024


def _cparams(sem):
    return pltpu.CompilerParams(dimension_semantics=sem, vmem_limit_bytes=VMEM_LIMIT)


def _lane_iota(shape):
    return lax.broadcasted_iota(I32, shape, len(shape) - 1)


def _rows_to_tiles(h, out_ref, slab_ref):
    for j in range(ROW_SUB):
        slab_ref[j * ROW_PITCH:j * ROW_PITCH + ROW_TILE, :] = h[:, j * LANES:(j + 1) * LANES]
    for t in range(ROW_TILE):
        out_ref[ROW_SUB * t:ROW_SUB * (t + 1), :] = slab_ref[pl.ds(t, ROW_SUB, stride=ROW_PITCH), :]


def _tiles_to_rows(in_ref, slab_ref):
    for t in range(ROW_TILE):
        slab_ref[pl.ds(t, ROW_SUB, stride=ROW_PITCH), :] = in_ref[ROW_SUB * t:ROW_SUB * (t + 1), :]
    return jnp.concatenate([slab_ref[j * ROW_PITCH:j * ROW_PITCH + ROW_TILE, :] for j in range(ROW_SUB)], axis=1)


def _slab_scratch():
    return pltpu.VMEM((ROW_SUB * ROW_PITCH, LANES), F32)


def _ada_kernel(a_ref, w_ref, b_ref, o_ref):
    o_ref[...] = jnp.dot(a_ref[...], w_ref[...], preferred_element_type=F32,
                         precision=lax.Precision.HIGHEST) + b_ref[...]


def _ada_call(a, ada_w, ada_b):
    depth, d, n = ada_w.shape
    rows = a.shape[0]
    nb = n // d
    return pl.pallas_call(
        _ada_kernel,
        grid=(depth, nb),
        in_specs=[pl.BlockSpec((rows, d), lambda l, j: (0, 0)),
                  pl.BlockSpec((None, d, d), lambda l, j: (l, 0, j)),
                  pl.BlockSpec((None, 1, d), lambda l, j: (l, 0, j))],
        out_specs=pl.BlockSpec((None, rows, d), lambda l, j: (l, 0, j)),
        out_shape=jax.ShapeDtypeStruct((depth, rows, n), F32),
        compiler_params=_cparams(("arbitrary", "arbitrary")),
        name="ada_mod",
    )(a, ada_w, ada_b.reshape(depth, 1, n))


def _swap_halves(x, half):
    lane = _lane_iota(x.shape)
    first = (lane & (2 * half - 1)) < half
    return jnp.where(first, pltpu.roll(x, LANES - half, 1), pltpu.roll(x, half, 1))


def _proj_kernel(x_ref, a_ref, sh_ref, w_ref, gains_ref, ca_ref, sa_ref, cd_ref, sd_ref,
                 qa_ref, ka_ref, va_ref, z_ref, cu_ref, qd_ref, kd_ref, vd_ref):
    x = x_ref[...]
    ms = jnp.mean(x * x, axis=-1, keepdims=True)
    h = (x * lax.rsqrt(ms + EPS) * a_ref[...] + sh_ref[...]).astype(BF16)
    p = jnp.dot(h, w_ref[...], preferred_element_type=F32)
    lane = _lane_iota((TILE, LANES))
    ca, sa, cd, sd = ca_ref[...], sa_ref[...], cd_ref[...], sd_ref[...]

    def slab(sec, i):
        o = PROJ_OFF[sec] + i * LANES
        return p[:, o:o + LANES]

    def norm_rope_a(s, gain):
        msq = jnp.sum(s * s, axis=-1, keepdims=True) * (1.0 / HEAD_DIM)
        sn = s * lax.rsqrt(msq + EPS) * gain
        return (sn * ca + _swap_halves(sn, HEAD_DIM // 4) * sa).astype(BF16)

    def norm_rope_d(s, gain):
        s2 = s * s
        m1 = jnp.sum(jnp.where(lane < D_QKDIM, s2, 0.0), axis=-1, keepdims=True)
        m2 = jnp.sum(jnp.where(lane >= D_QKDIM, s2, 0.0), axis=-1, keepdims=True)
        msq = jnp.where(lane < D_QKDIM, m1, m2) * (1.0 / D_QKDIM)
        sn = s * lax.rsqrt(msq + EPS) * gain
        return (sn * cd + _swap_halves(sn, D_QKDIM // 4) * sd).astype(BF16)

    def with_ones(s, width):
        return jnp.where(lane == width, 1.0, s).astype(BF16)

    for hh in range(A_HEADS):
        qa_ref[hh] = norm_rope_a(slab(0, hh), gains_ref[0:1, :])
    for hh in range(A_KV_HEADS):
        ka_ref[hh] = norm_rope_a(slab(1, hh), gains_ref[1:2, :])
        va_ref[hh] = with_ones(slab(2, hh), HEAD_DIM)
    bu = p[:, PROJ_OFF[3]:PROJ_OFF[4]]
    bg = p[:, PROJ_OFF[4]:PROJ_OFF[5]]
    z_ref[...] = bu * (1.0 / (1.0 + jnp.exp(-bg)))
    cu_ref[...] = p[:, PROJ_OFF[5]:PROJ_OFF[6]].astype(BF16)
    for hh in range(D_HEADS):
        qd_ref[hh] = norm_rope_d(slab(6, hh), gains_ref[2:3, :])
        kd_ref[hh] = norm_rope_d(slab(7, hh), gains_ref[3:4, :])
        vd_ref[hh] = with_ones(slab(8, hh), D_VDIM)


def _proj_call(x_all, a1, sh1, w_in_p, gains, rope, dims):
    B, NT, NL = dims
    T = x_all.shape[0]
    S = NT * TILE
    sel = lambda b, i: (jnp.where(i >= NL, B, b), 0, 0)
    row = lambda b, i: (b * NT + i, 0)
    hrow = lambda b, i: (0, b * NT + i, 0)
    tab = lambda b, i: (i, 0)
    head = lambda n: pl.BlockSpec((n, TILE, LANES), hrow)
    hshape = lambda n: jax.ShapeDtypeStruct((n, T, LANES), BF16)
    nw = w_in_p.shape[1]
    return pl.pallas_call(
        _proj_kernel,
        grid=(B, NT),
        in_specs=[pl.BlockSpec((TILE, D_MODEL), row),
                  pl.BlockSpec((None, 1, D_MODEL), sel),
                  pl.BlockSpec((None, 1, D_MODEL), sel),
                  pl.BlockSpec((D_MODEL, nw), lambda b, i: (0, 0)),
                  pl.BlockSpec((8, LANES), lambda b, i: (0, 0)),
                  pl.BlockSpec((TILE, LANES), tab), pl.BlockSpec((TILE, LANES), tab),
                  pl.BlockSpec((TILE, LANES), tab), pl.BlockSpec((TILE, LANES), tab)],
        out_specs=[head(A_HEADS), head(A_KV_HEADS), head(A_KV_HEADS),
                   pl.BlockSpec((TILE, GROUP_W), row), pl.BlockSpec((TILE, GROUP_W), row),
                   head(D_HEADS), head(D_HEADS), head(D_HEADS)],
        out_shape=[hshape(A_HEADS), hshape(A_KV_HEADS), hshape(A_KV_HEADS),
                   jax.ShapeDtypeStruct((T, GROUP_W), F32), jax.ShapeDtypeStruct((T, GROUP_W), BF16),
                   hshape(D_HEADS), hshape(D_HEADS), hshape(D_HEADS)],
        compiler_params=_cparams(("arbitrary", "arbitrary")),
        name="in_proj",
    )(x_all, a1, sh1, w_in_p, gains, *rope)


def _softmax_pv(load_q, k_ref, v_ref, k0, nchunks, s_scr, m_scr, acc_scr):
    rows = m_scr.shape[0]
    m_scr[...] = jnp.full((rows, LANES), NEG_BIG, F32)
    acc_scr[...] = jnp.zeros((rows, LANES), F32)

    def scores(c, carry):
        start = pl.multiple_of(k0 + c * KEY_CHUNK, KEY_CHUNK)
        kc = k_ref[pl.ds(start, KEY_CHUNK), :]
        s = lax.dot_general(load_q(), kc, (((1,), (1,)), ((), ())), preferred_element_type=F32)
        s_scr[c] = s
        part = s[:, 0:LANES]
        for j in range(1, KEY_CHUNK // LANES):
            part = jnp.maximum(part, s[:, j * LANES:(j + 1) * LANES])
        m_scr[...] = jnp.maximum(m_scr[...], part)
        return carry

    lax.fori_loop(0, nchunks, scores, 0)
    mrow = jnp.max(m_scr[...], axis=-1, keepdims=True)
    m_scr[...] = jnp.broadcast_to(mrow, (rows, LANES))

    def weighted(c, carry):
        start = pl.multiple_of(k0 + c * KEY_CHUNK, KEY_CHUNK)
        vc = v_ref[pl.ds(start, KEY_CHUNK), :]
        s = s_scr[c]
        mb = m_scr[...]
        pr = jnp.concatenate([jnp.exp(s[:, j * LANES:(j + 1) * LANES] - mb)
                              for j in range(KEY_CHUNK // LANES)], axis=1).astype(BF16)
        acc_scr[...] += jnp.dot(pr, vc, preferred_element_type=F32)
        return carry

    lax.fori_loop(0, nchunks, weighted, 0)


def _key_range(i, NL, NT):
    is_ctx = i >= NL
    k0 = jnp.where(is_ctx, NL * TILE, 0)
    n = jnp.where(is_ctx, (NT - NL) * TILE // KEY_CHUNK, NT * TILE // KEY_CHUNK)
    return k0, n


def _gqa_kernel(q_ref, k_ref, v_ref, o_ref, s_scr, m_scr, acc_scr, *, NL, NT):
    group = A_HEADS // A_KV_HEADS
    k0, n = _key_range(pl.program_id(2), NL, NT)
    load_q = lambda: q_ref[...].reshape(group * TILE, LANES)
    _softmax_pv(load_q, k_ref, v_ref, k0, n, s_scr, m_scr, acc_scr)
    acc = acc_scr[...]
    o = acc * (1.0 / acc[:, HEAD_DIM:HEAD_DIM + 1])
    o_ref[...] = o.reshape(group, TILE, LANES).astype(BF16)


def _gqa_call(qa, ka, va, dims, ntq):
    B, NT, NL = dims
    T = qa.shape[1]
    S = NT * TILE
    group = A_HEADS // A_KV_HEADS
    qmap = lambda b, g, i: (g, b * NT + i, 0)
    kmap = lambda b, g, i: (g, b, 0)
    return pl.pallas_call(
        functools.partial(_gqa_kernel, NL=NL, NT=NT),
        grid=(B, A_KV_HEADS, ntq),
        in_specs=[pl.BlockSpec((group, TILE, LANES), qmap),
                  pl.BlockSpec((None, S, LANES), kmap),
                  pl.BlockSpec((None, S, LANES), kmap)],
        out_specs=pl.BlockSpec((group, TILE, LANES), qmap),
        out_shape=jax.ShapeDtypeStruct((A_HEADS, T, LANES), BF16),
        scratch_shapes=[pltpu.VMEM((S // KEY_CHUNK, group * TILE, KEY_CHUNK), F32),
                        pltpu.VMEM((group * TILE, LANES), F32),
                        pltpu.VMEM((group * TILE, LANES), F32)],
        compiler_params=_cparams(("arbitrary", "arbitrary", "arbitrary")),
        name="gqa_attn",
    )(qa, ka, va)


def _diff_kernel(q_ref, k_ref, v_ref, par_ref, o_ref, q2_scr, s_scr, m_scr, acc_scr, *, NL, NT):
    k0, n = _key_range(pl.program_id(2), NL, NT)
    q = q_ref[...]
    lane = _lane_iota(q.shape)
    zero = jnp.zeros_like(q)
    q2_scr[0:TILE, :] = jnp.where(lane < D_QKDIM, q, zero)
    q2_scr[TILE:2 * TILE, :] = jnp.where(lane >= D_QKDIM, q, zero)
    _softmax_pv(lambda: q2_scr[...], k_ref, v_ref, k0, n, s_scr, m_scr, acc_scr)
    acc = acc_scr[...]
    o1 = acc[:TILE] * (1.0 / acc[:TILE, D_VDIM:D_VDIM + 1])
    o2 = acc[TILE:] * (1.0 / acc[TILE:, D_VDIM:D_VDIM + 1])
    o = jnp.where(lane < D_VDIM, o1 - par_ref[0:1, :] * o2, 0.0)
    msq = jnp.sum(o * o, axis=-1, keepdims=True) * (1.0 / D_VDIM)
    o_ref[...] = (o * lax.rsqrt(msq + EPS) * par_ref[1:2, :]).astype(BF16)


def _diff_call(qd, kd, vd, dpar, dims, ntq):
    B, NT, NL = dims
    T = qd.shape[1]
    S = NT * TILE
    qmap = lambda b, h, i: (h, b * NT + i, 0)
    kmap = lambda b, h, i: (h, b, 0)
    return pl.pallas_call(
        functools.partial(_diff_kernel, NL=NL, NT=NT),
        grid=(B, D_HEADS, ntq),
        in_specs=[pl.BlockSpec((None, TILE, LANES), qmap),
                  pl.BlockSpec((None, S, LANES), kmap),
                  pl.BlockSpec((None, S, LANES), kmap),
                  pl.BlockSpec((8, LANES), lambda b, h, i: (0, 0))],
        out_specs=pl.BlockSpec((None, TILE, LANES), qmap),
        out_shape=jax.ShapeDtypeStruct((D_HEADS, T, LANES), BF16),
        scratch_shapes=[pltpu.VMEM((2 * TILE, LANES), BF16),
                        pltpu.VMEM((S // KEY_CHUNK, 2 * TILE, KEY_CHUNK), F32),
                        pltpu.VMEM((2 * TILE, LANES), F32),
                        pltpu.VMEM((2 * TILE, LANES), F32)],
        compiler_params=_cparams(("arbitrary", "arbitrary", "arbitrary")),
        name="diff_attn",
    )(qd, kd, vd, dpar)


def _conv_kernel(z_ref, w_ref, par_ref, o_ref, zp_ref, *, segs):
    zeros = jnp.zeros((CONV_HALO, GROUP_W), F32)
    pos = 0
    starts = []
    for (r0, n) in segs:
        zp_ref[pos:pos + CONV_HALO, :] = zeros
        zp_ref[pos + CONV_HALO:pos + CONV_HALO + n, :] = z_ref[r0:r0 + n, :]
        starts.append(pos + CONV_HALO)
        pos += CONV_HALO + n
    zp_ref[pos:pos + CONV_HALO, :] = zeros
    bias, ln_g, ln_b = par_ref[0:1, :], par_ref[1:2, :], par_ref[2:3, :]

    for (r0, n), p0 in zip(segs, starts):
        def chunk(j, carry, r0=r0, p0=p0):
            base = pl.multiple_of(j * CONV_ROWS, CONV_ROWS)
            win = zp_ref[pl.ds(base + (p0 - CONV_HALO), CONV_ROWS + 2 * CONV_HALO), :]
            acc = jnp.zeros((CONV_ROWS, GROUP_W), F32)
            for k in range(CONV_W):
                o = CONV_HALO - CONV_W // 2 + k
                acc = acc + win[o:o + CONV_ROWS, :] * w_ref[k:k + 1, :]
            u = acc + bias
            mu = jnp.mean(u, axis=-1, keepdims=True)
            uc = u - mu
            var = jnp.mean(uc * uc, axis=-1, keepdims=True)
            y = uc * lax.rsqrt(var + EPS) * ln_g + ln_b
            o_ref[pl.ds(pl.multiple_of(base + r0, CONV_ROWS), CONV_ROWS), :] = (
                y * (1.0 / (1.0 + jnp.exp(-y)))).astype(BF16)
            return carry

        lax.fori_loop(0, n // CONV_ROWS, chunk, 0)


def _conv_call(z, conv_w, conv_par, dims, with_ctx):
    B, NT, NL = dims
    T = z.shape[0]
    S = NT * TILE
    L = NL * TILE
    segs = ((0, L), (L, S - L)) if with_ctx else ((0, L),)
    pad_rows = sum(n for _, n in segs) + CONV_HALO * (len(segs) + 1)
    return pl.pallas_call(
        functools.partial(_conv_kernel, segs=segs),
        grid=(B,),
        in_specs=[pl.BlockSpec((S, GROUP_W), lambda b: (b, 0)),
                  pl.BlockSpec((32, GROUP_W), lambda b: (0, 0)),
                  pl.BlockSpec((8, GROUP_W), lambda b: (0, 0))],
        out_specs=pl.BlockSpec((S, GROUP_W), lambda b: (b, 0)),
        out_shape=jax.ShapeDtypeStruct((T, GROUP_W), BF16),
        scratch_shapes=[pltpu.VMEM((pad_rows, GROUP_W), F32)],
        compiler_params=_cparams(("arbitrary",)),
        name="conformer_conv",
    )(z, conv_w, conv_par)


def _dft_kernel(c_ref, s_ref, x_ref, cb_ref, sb_ref, o_ref, *, r0, n):
    xs = x_ref[r0:r0 + n, :]
    u = jnp.dot(c_ref[...], xs, preferred_element_type=F32).astype(BF16)
    w = jnp.dot(s_ref[...], xs, preferred_element_type=F32).astype(BF16)
    y = (jnp.dot(u, cb_ref[...], preferred_element_type=F32)
         - jnp.dot(w, sb_ref[...], preferred_element_type=F32))
    o_ref[...] = y.astype(BF16)


def _dft_call(cu, cmat, smat, cbd, sbd, dims, r0, n, prev=None):
    B, NT, NL = dims
    T = cu.shape[0]
    S = NT * TILE
    nr = n // TILE
    t0 = r0 // TILE
    args = [cmat, smat, cu, cbd, sbd]
    in_specs = [pl.BlockSpec((TILE, n), lambda r, b: (r, 0)),
                pl.BlockSpec((TILE, n), lambda r, b: (r, 0)),
                pl.BlockSpec((S, GROUP_W), lambda r, b: (b, 0)),
                pl.BlockSpec((GROUP_W, GROUP_W), lambda r, b: (0, 0)),
                pl.BlockSpec((GROUP_W, GROUP_W), lambda r, b: (0, 0))]
    kern = functools.partial(_dft_kernel, r0=r0, n=n)
    aliases = {}
    if prev is not None:
        args.append(prev)
        in_specs.append(pl.BlockSpec(memory_space=pl.ANY))
        aliases = {5: 0}
        kern = lambda c, s, x, cb, sb, _prev, o: _dft_kernel(c, s, x, cb, sb, o, r0=r0, n=n)
    return pl.pallas_call(
        kern,
        grid=(nr, B),
        in_specs=in_specs,
        out_specs=pl.BlockSpec((TILE, GROUP_W), lambda r, b: (b * NT + t0 + r, 0)),
        out_shape=jax.ShapeDtypeStruct((T, GROUP_W), BF16),
        input_output_aliases=aliases,
        compiler_params=_cparams(("arbitrary", "arbitrary")),
        name="fourier_mix",
    )(*args)


def _out_kernel(x_ref, oa_ref, ob_ref, oc_ref, od_ref, w_ref, g1_ref, a2_ref, sh2_ref, rw_ref, rb_ref,
                xn_ref, h2_ref, ti_ref, gt_ref, cnt_ref, slab_ref):
    ocat = jnp.concatenate([oa_ref[hh] for hh in range(A_HEADS)] + [ob_ref[...], oc_ref[...]]
                           + [od_ref[hh] for hh in range(D_HEADS)], axis=1)
    r = jnp.dot(ocat, w_ref[...], preferred_element_type=F32)
    xn = x_ref[...] + g1_ref[...] * r
    xn_ref[...] = xn
    ms = jnp.mean(xn * xn, axis=-1, keepdims=True)
    h2 = xn * lax.rsqrt(ms + EPS) * a2_ref[...] + sh2_ref[...]
    _rows_to_tiles(h2, h2_ref, slab_ref)
    logits = jnp.dot(h2.astype(BF16), rw_ref[...], preferred_element_type=F32) + rb_ref[...]
    lane = _lane_iota(logits.shape)
    lanef = lane.astype(F32)
    vals, idxs = [], []
    cur = logits
    for _ in range(TOP_K):
        mx = jnp.max(cur, axis=-1, keepdims=True)
        idx = jnp.min(jnp.where(cur == mx, lanef, float(LANES)), axis=-1, keepdims=True)
        vals.append(mx)
        idxs.append(idx)
        cur = jnp.where(lanef == idx, NEG_BIG * 2, cur)
    exps = [jnp.exp(v - vals[0]) for v in vals]
    inv = 1.0 / (exps[0] + exps[1] + exps[2] + exps[3])
    ti = jnp.zeros(logits.shape, F32)
    gt = jnp.zeros(logits.shape, F32)
    onehot = jnp.zeros(logits.shape, F32)
    for k in range(TOP_K):
        ti = jnp.where(lane == k, idxs[k], ti)
        gt = jnp.where(lane == k, exps[k] * inv, gt)
        onehot = onehot + jnp.where(lanef == idxs[k], 1.0, 0.0)
    ti_ref[...] = ti.astype(I32)
    gt_ref[...] = gt

    @pl.when((pl.program_id(0) == 0) & (pl.program_id(1) == 0))
    def _():
        cnt_ref[...] = jnp.zeros(cnt_ref.shape, F32)

    cnt_ref[0:1, :] += jnp.sum(onehot, axis=0, keepdims=True)


def _out_call(x_all, oa, ob, oc, od, w_out_p, g1, a2, sh2, rw, rb, dims, ntq):
    B, NT, NL = dims
    n_tok = B * ntq * TILE
    sel = lambda b, i: (jnp.where(i >= NL, B, b), 0, 0)
    row = lambda b, i: (b * NT + i, 0)
    hrow = lambda b, i: (0, b * NT + i, 0)
    crow = lambda b, i: (b * ntq + i, 0)
    const = lambda b, i: (0, 0)
    return pl.pallas_call(
        _out_kernel,
        grid=(B, ntq),
        in_specs=[pl.BlockSpec((TILE, D_MODEL), row),
                  pl.BlockSpec((A_HEADS, TILE, LANES), hrow),
                  pl.BlockSpec((TILE, GROUP_W), row),
                  pl.BlockSpec((TILE, GROUP_W), row),
                  pl.BlockSpec((D_HEADS, TILE, LANES), hrow),
                  pl.BlockSpec((OUT_K, D_MODEL), const),
                  pl.BlockSpec((None, 1, D_MODEL), sel),
                  pl.BlockSpec((None, 1, D_MODEL), sel),
                  pl.BlockSpec((None, 1, D_MODEL), sel),
                  pl.BlockSpec((D_MODEL, LANES), const),
                  pl.BlockSpec((1, LANES), const)],
        out_specs=[pl.BlockSpec((TILE, D_MODEL), crow),
                   pl.BlockSpec((TILE * ROW_SUB, LANES), crow),
                   pl.BlockSpec((TILE, LANES), crow),
                   pl.BlockSpec((TILE, LANES), crow),
                   pl.BlockSpec((8, LANES), const)],
        out_shape=[jax.ShapeDtypeStruct((n_tok, D_MODEL), F32),
                   jax.ShapeDtypeStruct((n_tok * ROW_SUB, LANES), F32),
                   jax.ShapeDtypeStruct((n_tok, LANES), I32),
                   jax.ShapeDtypeStruct((n_tok, LANES), F32),
                   jax.ShapeDtypeStruct((8, LANES), F32)],
        scratch_shapes=[_slab_scratch()],
        compiler_params=_cparams(("arbitrary", "arbitrary")),
        name="out_proj_router",
    )(x_all, oa, ob, oc, od, w_out_p, g1, a2, sh2, rw, rb)


def _rank_kernel(ti_ref, ps_ref, tri_ref, dest_ref, carry_ref):
    @pl.when(pl.program_id(0) == 0)
    def _():
        carry_ref[...] = jnp.zeros(carry_ref.shape, F32)

    ti = ti_ref[...]
    lane = _lane_iota(ti.shape)
    hits = [lane == ti[:, k:k + 1] for k in range(TOP_K)]
    onehot = jnp.zeros(ti.shape, F32)
    for hk in hits:
        onehot = onehot + jnp.where(hk, 1.0, 0.0)
    before = jnp.dot(tri_ref[...], onehot.astype(BF16), preferred_element_type=F32)
    base = before + carry_ref[0:1, :] + ps_ref[...]
    dest = jnp.zeros(ti.shape, F32)
    for k, hk in enumerate(hits):
        dk = jnp.sum(jnp.where(hk, base, 0.0), axis=-1, keepdims=True)
        dest = jnp.where(lane == k, dk, dest)
    dest_ref[...] = dest.astype(I32)
    carry_ref[0:1, :] += jnp.sum(onehot, axis=0, keepdims=True)


def _rank_call(topi, pad_start, tri):
    n_tok = topi.shape[0]
    return pl.pallas_call(
        _rank_kernel,
        grid=(n_tok // TILE,),
        in_specs=[pl.BlockSpec((TILE, LANES), lambda t: (t, 0)),
                  pl.BlockSpec((1, LANES), lambda t: (0, 0)),
                  pl.BlockSpec((TILE, TILE), lambda t: (0, 0))],
        out_specs=pl.BlockSpec((TILE, LANES), lambda t: (t, 0)),
        out_shape=jax.ShapeDtypeStruct((n_tok, LANES), I32),
        scratch_shapes=[pltpu.VMEM((8, LANES), F32)],
        compiler_params=_cparams(("arbitrary",)),
        name="slot_rank",
    )(topi, pad_start, tri)


def _tile_copy(src_ref, src_row, dst_ref, dst_row, sem):
    src = src_ref.at[pl.ds(pl.multiple_of(src_row * ROW_SUB, ROW_SUB), ROW_SUB)]
    dst = dst_ref.at[pl.ds(pl.multiple_of(dst_row * ROW_SUB, ROW_SUB), ROW_SUB)]
    return pltpu.make_async_copy(src, dst, sem)


def _dispatch_kernel(fill_start_ref, fill_on_ref, dest_ref, h2_ref, zero_ref, xs_ref, sem):
    @pl.when(pl.program_id(0) == 0)
    def _():
        def fill(e, wait):
            @pl.when(fill_on_ref[e] > 0)
            def _():
                start = pl.multiple_of(fill_start_ref[e] * ROW_SUB, MOE_BLOCK * ROW_SUB)
                cp = pltpu.make_async_copy(zero_ref, xs_ref.at[pl.ds(start, MOE_BLOCK * ROW_SUB)], sem.at[1])
                if wait:
                    cp.wait()
                else:
                    cp.start()

        lax.fori_loop(0, N_EXPERTS, lambda e, c: (fill(e, False), c)[1], 0)
        lax.fori_loop(0, N_EXPERTS, lambda e, c: (fill(e, True), c)[1], 0)

    def issue(r, c):
        for k in range(TOP_K):
            _tile_copy(h2_ref, r, xs_ref, dest_ref[0, 0, TOP_K * r + k], sem.at[0]).start()
        return c

    lax.fori_loop(0, TILE, issue, 0)

    def drain(r, c):
        for k in range(TOP_K):
            _tile_copy(h2_ref, 0, xs_ref, 0, sem.at[0]).wait()
        return c

    lax.fori_loop(0, TILE, drain, 0)


def _dispatch_call(fill_start, fill_on, dest3, h2t, n_slots):
    n_tok = h2t.shape[0] // ROW_SUB
    zero_blk = jnp.zeros((MOE_BLOCK * ROW_SUB, LANES), F32)
    return pl.pallas_call(
        _dispatch_kernel,
        grid_spec=pltpu.PrefetchScalarGridSpec(
            num_scalar_prefetch=2,
            grid=(n_tok // TILE,),
            in_specs=[pl.BlockSpec((1, 1, TOP_K * TILE), lambda t, fs, fo: (t, 0, 0), memory_space=pltpu.SMEM),
                      pl.BlockSpec((TILE * ROW_SUB, LANES), lambda t, fs, fo: (t, 0)),
                      pl.BlockSpec((MOE_BLOCK * ROW_SUB, LANES), lambda t, fs, fo: (0, 0))],
            out_specs=pl.BlockSpec(memory_space=pl.ANY),
            scratch_shapes=[pltpu.SemaphoreType.DMA((2,))]),
        out_shape=jax.ShapeDtypeStruct((n_slots * ROW_SUB, LANES), F32),
        compiler_params=_cparams(("arbitrary",)),
        name="moe_dispatch",
    )(fill_start, fill_on, dest3, h2t, zero_blk)


def _expert_kernel(blk_e_ref, nact_ref, xs_ref, wgu_ref, bgu_ref, wd_ref, bd_ref, yb_ref, slab_ref):
    @pl.when(pl.program_id(0) < nact_ref[0])
    def _():
        x = _tiles_to_rows(xs_ref, slab_ref).astype(BF16)
        gu = jnp.dot(x, wgu_ref[...], preferred_element_type=F32) + bgu_ref[...]
        g = jnp.minimum(gu[:, :D_FF], SWIGLU_LIMIT)
        u = jnp.clip(gu[:, D_FF:], -SWIGLU_LIMIT, SWIGLU_LIMIT)
        act = g * (1.0 / (1.0 + jnp.exp(-SWIGLU_ALPHA * g))) * (u + 1.0)
        y = jnp.dot(act.astype(BF16), wd_ref[...], preferred_element_type=F32) + bd_ref[...]
        _rows_to_tiles(y, yb_ref, slab_ref)


def _expert_call(blk_e, n_active, xs, w_gu, b_gu, w_down, b_down):
    n_slots = xs.shape[0] // ROW_SUB
    n_blocks = n_slots // MOE_BLOCK
    blk = lambda j, be, na: (jnp.minimum(j, na[0] - 1), 0)
    wsel = lambda j, be, na: (be[jnp.minimum(j, na[0] - 1)], 0, 0)
    return pl.pallas_call(
        _expert_kernel,
        grid_spec=pltpu.PrefetchScalarGridSpec(
            num_scalar_prefetch=2,
            grid=(n_blocks,),
            in_specs=[pl.BlockSpec((MOE_BLOCK * ROW_SUB, LANES), blk),
                      pl.BlockSpec((None, D_MODEL, 2 * D_FF), wsel),
                      pl.BlockSpec((None, 1, 2 * D_FF), wsel),
                      pl.BlockSpec((None, D_FF, D_MODEL), wsel),
                      pl.BlockSpec((None, 1, D_MODEL), wsel)],
            out_specs=pl.BlockSpec((MOE_BLOCK * ROW_SUB, LANES), blk),
            scratch_shapes=[_slab_scratch()]),
        out_shape=jax.ShapeDtypeStruct((n_slots * ROW_SUB, LANES), F32),
        compiler_params=_cparams(("arbitrary",)),
        name="moe_experts",
    )(blk_e, n_active, xs, w_gu, b_gu, w_down, b_down)


def _combine_kernel(dest_ref, x_ref, gt_ref, g2_ref, yb_ref, o_ref, buf, slab_ref, sem):
    def issue(r, c):
        for k in range(TOP_K):
            _tile_copy(yb_ref, dest_ref[0, 0, TOP_K * r + k], buf.at[k], r, sem.at[0]).start()
        return c

    lax.fori_loop(0, TILE, issue, 0)

    def drain(r, c):
        for k in range(TOP_K):
            _tile_copy(yb_ref, 0, buf.at[k], 0, sem.at[0]).wait()
        return c

    lax.fori_loop(0, TILE, drain, 0)
    gt = gt_ref[...]
    y = gt[:, 0:1] * _tiles_to_rows(buf.at[0], slab_ref)
    for k in range(1, TOP_K):
        y = y + gt[:, k:k + 1] * _tiles_to_rows(buf.at[k], slab_ref)
    o_ref[...] = x_ref[...] + g2_ref[...] * y


def _combine_call(dest3, xn, gates, g2, yb, dims, ntq):
    B, NT, NL = dims
    n_tok = xn.shape[0]
    sel = lambda b, i: (jnp.where(i >= NL, B, b), 0, 0)
    crow = lambda b, i: (b * ntq + i, 0)
    return pl.pallas_call(
        _combine_kernel,
        grid=(B, ntq),
        in_specs=[pl.BlockSpec((1, 1, TOP_K * TILE), lambda b, i: (b * ntq + i, 0, 0), memory_space=pltpu.SMEM),
                  pl.BlockSpec((TILE, D_MODEL), crow),
                  pl.BlockSpec((TILE, LANES), crow),
                  pl.BlockSpec((None, 1, D_MODEL), sel),
                  pl.BlockSpec(memory_space=pl.ANY)],
        out_specs=pl.BlockSpec((TILE, D_MODEL), crow),
        out_shape=jax.ShapeDtypeStruct((n_tok, D_MODEL), F32),
        scratch_shapes=[pltpu.VMEM((TOP_K, TILE * ROW_SUB, LANES), F32), _slab_scratch(),
                        pltpu.SemaphoreType.DMA((1,))],
        compiler_params=_cparams(("arbitrary", "arbitrary")),
        name="moe_combine",
    )(dest3, xn, gates, g2, yb)


def _pad_heads_cols(w, n_heads, width):
    w = w.reshape(w.shape[0], n_heads, width)
    return jnp.pad(w, ((0, 0), (0, 0), (0, LANES - width))).reshape(w.shape[0], n_heads * LANES)


def _pad_heads_rows(w, n_heads, width):
    w = w.reshape(n_heads, width, w.shape[1])
    return jnp.pad(w, ((0, 0), (0, LANES - width), (0, 0))).reshape(n_heads * LANES, w.shape[2])


def _prep_w_in(w):
    splits = (256, 128, 128, 256, 256, 256, 256, 256, 256)
    parts, o = [], 0
    for s in splits:
        parts.append(w[:, o:o + s])
        o += s
    aq, ak, av, bu, bg, cu, dq, dk, dv = parts
    return jnp.concatenate([
        _pad_heads_cols(aq, A_HEADS, HEAD_DIM), _pad_heads_cols(ak, A_KV_HEADS, HEAD_DIM),
        _pad_heads_cols(av, A_KV_HEADS, HEAD_DIM), bu, bg, cu,
        _pad_heads_cols(dq, D_HEADS, 2 * D_QKDIM), _pad_heads_cols(dk, D_HEADS, 2 * D_QKDIM),
        _pad_heads_cols(dv, D_HEADS, D_VDIM)], axis=1).astype(BF16)


def _prep_w_out(w):
    return jnp.concatenate([
        _pad_heads_rows(w[0:GROUP_W], A_HEADS, HEAD_DIM), w[GROUP_W:3 * GROUP_W],
        _pad_heads_rows(w[3 * GROUP_W:], D_HEADS, D_VDIM)], axis=0).astype(BF16)


def _pad_lanes(v):
    return jnp.pad(v.astype(F32), (0, LANES - v.shape[0]))


def _rope_tables(L, C):
    t = jnp.arange(L, dtype=jnp.int32)
    rows = (t // GRID_W).astype(F32)[:, None]
    cols = (t % GRID_W).astype(F32)[:, None]
    lane = jnp.arange(LANES)

    def table(group):
        nfreq = group // 2
        j = lane % nfreq
        inv = jnp.power(ROPE_THETA, -(2.0 * j.astype(F32)) / group)[None, :]
        use_cols = ((lane // group) % 2) == 1
        ang = jnp.where(use_cols[None, :], cols, rows) * inv
        sign = jnp.where((lane % group) < nfreq, -1.0, 1.0)[None, :]
        live = (lane < 2 * D_QKDIM)[None, :]
        cos = jnp.where(live, jnp.cos(ang), 1.0)
        sin = jnp.where(live, jnp.sin(ang) * sign, 0.0)
        cos = jnp.concatenate([cos, jnp.ones((C, LANES), F32)], axis=0)
        sin = jnp.concatenate([sin, jnp.zeros((C, LANES), F32)], axis=0)
        return cos, sin

    ca, sa = table(HEAD_DIM // 2)
    cd, sd = table(D_QKDIM // 2)
    return ca, sa, cd, sd


def _dft_mats(n):
    k = jnp.arange(n, dtype=jnp.int32)
    prod = (k[:, None] * k[None, :]) % n
    ang = prod.astype(F32) * (2.0 * math.pi / n)
    scale = 1.0 / math.sqrt(n)
    return (jnp.cos(ang) * scale).astype(BF16), (jnp.sin(ang) * scale).astype(BF16)


def _channel_dft_mats():
    w = GROUP_W // FNET_GROUPS
    k = jnp.arange(GROUP_W, dtype=jnp.int32)
    same = (k[:, None] // w) == (k[None, :] // w)
    prod = ((k[:, None] % w) * (k[None, :] % w)) % w
    ang = prod.astype(F32) * (2.0 * math.pi / w)
    scale = 1.0 / math.sqrt(w)
    return (jnp.where(same, jnp.cos(ang) * scale, 0.0).astype(BF16),
            jnp.where(same, jnp.sin(ang) * scale, 0.0).astype(BF16))


def _slot_plan(counts, n_blocks):
    cnt = counts[0, :N_EXPERTS].astype(I32)
    padded = (cnt + MOE_BLOCK - 1) // MOE_BLOCK * MOE_BLOCK
    pad_ends = jnp.cumsum(padded)
    pad_starts = pad_ends - padded
    n_active = (pad_ends[-1] // MOE_BLOCK).reshape(1).astype(I32)
    blk_start = jnp.arange(n_blocks, dtype=I32) * MOE_BLOCK
    blk_e = jnp.minimum(jnp.searchsorted(pad_ends, blk_start, side='right'), N_EXPERTS - 1).astype(I32)
    fill_start = jnp.maximum(pad_ends - MOE_BLOCK, 0).astype(I32)
    fill_on = (cnt > 0).astype(I32)
    ps_vec = jnp.pad(pad_starts.astype(F32), (0, LANES - N_EXPERTS)).reshape(1, LANES)
    return ps_vec, blk_e, n_active, fill_start, fill_on


def kernel(x, c, ctx, c_ctx, norm1_g, norm2_g, ada_w, ada_b, w_in, a_qnorm_g, a_knorm_g, conv_dw_w, conv_dw_b,
           conv_ln_g, conv_ln_b, d_qnorm_g, d_knorm_g, d_lambda, d_subln_g, w_out, router_w, router_b,
           exp_w_gu, exp_b_gu, exp_w_down, exp_b_down):
    B, L, D = x.shape
    C = ctx.shape[1]
    depth = ada_w.shape[0]
    assert D == D_MODEL and L % TILE == 0 and C % TILE == 0 and L % GRID_W == 0
    NL, NT = L // TILE, (L + C) // TILE
    dims = (B, NT, NL)
    S = NT * TILE

    x_all = jnp.concatenate([x, ctx], axis=1).reshape(B * S, D)

    a = jnp.concatenate([c, c_ctx[None, :]], axis=0)
    a = jnp.pad(a * jax.nn.sigmoid(a), ((0, 16 - (B + 1) % 16 if (B + 1) % 16 else 0), (0, 0)))
    mod_all = _ada_call(a, ada_w, ada_b)

    rope = _rope_tables(L, C)
    cmat, smat = _dft_mats(L)
    cmat_c, smat_c = _dft_mats(C)
    cbd, sbd = _channel_dft_mats()
    tri = jnp.tril(jnp.ones((TILE, TILE), F32), -1).astype(BF16)

    x_lat = None
    for li in range(depth):
        last = li == depth - 1
        ntq = NL if last else NT
        mod = mod_all[li, :B + 1].reshape(B + 1, 6, 1, D)
        sh1, sc1, g1, sh2, sc2, g2 = (mod[:, j] for j in range(6))
        a1 = norm1_g[li][None, None, :] * (1.0 + sc1)
        a2 = norm2_g[li][None, None, :] * (1.0 + sc2)

        gains = jnp.stack([
            _pad_lanes(a_qnorm_g[li]) * (HEAD_DIM ** -0.5), _pad_lanes(a_knorm_g[li]),
            _pad_lanes(jnp.tile(d_qnorm_g[li], 2)) * (D_QKDIM ** -0.5), _pad_lanes(jnp.tile(d_knorm_g[li], 2)),
        ] + [jnp.zeros((LANES,), F32)] * 4)
        qa, ka, va, z, cu, qd, kd, vd = _proj_call(x_all, a1, sh1, _prep_w_in(w_in[li]), gains, rope, dims)

        lam_init = 0.8 - 0.6 * math.exp(-0.3 * li)
        lq1, lk1, lq2, lk2 = d_lambda[li].astype(F32)
        lam = jnp.exp(jnp.sum(lq1 * lk1)) - jnp.exp(jnp.sum(lq2 * lk2)) + lam_init
        dpar = jnp.stack([jnp.full((LANES,), lam, F32), _pad_lanes(d_subln_g[li]) * (1.0 - lam_init)]
                         + [jnp.zeros((LANES,), F32)] * 6)

        oa = _gqa_call(qa, ka, va, dims, ntq)
        od = _diff_call(qd, kd, vd, dpar, dims, ntq)
        conv_w = jnp.pad(conv_dw_w[li], ((0, 32 - CONV_W), (0, 0)))
        conv_par = jnp.stack([conv_dw_b[li], conv_ln_g[li], conv_ln_b[li]] + [jnp.zeros((GROUP_W,), F32)] * 5)
        ob = _conv_call(z, conv_w, conv_par, dims, with_ctx=not last)
        oc = _dft_call(cu, cmat, smat, cbd, sbd, dims, 0, L)
        if not last:
            oc = _dft_call(cu, cmat_c, smat_c, cbd, sbd, dims, L, C, prev=oc)

        rw = jnp.pad(router_w[li], ((0, 0), (0, LANES - N_EXPERTS))).astype(BF16)
        rb = jnp.concatenate([router_b[li].astype(F32), jnp.full((LANES - N_EXPERTS,), NEG_BIG, F32)]).reshape(1, LANES)
        xn, h2t, topi, gates, counts = _out_call(x_all, oa, ob, oc, od, _prep_w_out(w_out[li]), g1, a2, sh2,
                                                 rw, rb, dims, ntq)

        n_tok = B * ntq * TILE
        n_blocks = -(-(n_tok * TOP_K + N_EXPERTS * (MOE_BLOCK - 1)) // MOE_BLOCK)
        ps_vec, blk_e, n_active, fill_start, fill_on = _slot_plan(counts, n_blocks)
        dest = _rank_call(topi, ps_vec, tri)
        dest3 = dest[:, :TOP_K].reshape(n_tok // TILE, 1, TOP_K * TILE)
        xs = _dispatch_call(fill_start, fill_on, dest3, h2t, n_blocks * MOE_BLOCK)
        yb = _expert_call(blk_e, n_active, xs, exp_w_gu[li].astype(BF16), exp_b_gu[li][:, None, :],
                          exp_w_down[li].astype(BF16), exp_b_down[li][:, None, :])
        x_next = _combine_call(dest3, xn, gates, g2, yb, dims, ntq)
        if last:
            x_lat = x_next
        else:
            x_all = x_next
    return x_lat.reshape(B, L, D)
```

```python
import functools
import math

import jax
import jax.numpy as jnp
from jax import lax
from jax.experimental import pallas as pl
from jax.experimental.pallas import tpu as pltpu

F32 = jnp.float32
BF16 = jnp.bfloat16
I32 = jnp.int32

D_MODEL = 1024
TILE = 256
LANES = 128
GRID_W = 64
HEAD_DIM = 64
A_HEADS = 4
A_KV_HEADS = 2
CONV_W = 31
CONV_HALO = 16
CONV_ROWS = 128
FNET_GROUPS = 4
D_HEADS = 4
D_QKDIM = 32
D_VDIM = 64
GROUP_W = 256
N_EXPERTS = 32
TOP_K = 4
D_FF = D_MODEL
SWIGLU_LIMIT = 7.0
SWIGLU_ALPHA = 1.702
MOE_BLOCK = 256
KEY_CHUNK = 256
DMA_UNROLL = 8
ROPE_THETA = 10000.0
EPS = 1e-6
NEG_BIG = -1e30
LOG2E = math.log2(math.e)
SUBLANES = 8
ROW_SUB = D_MODEL // LANES
assert ROW_SUB == SUBLANES
ROW_TILE = TILE
ROW_PITCH = ROW_TILE + SUBLANES

PROJ_W = (A_HEADS * LANES, A_KV_HEADS * LANES, A_KV_HEADS * LANES, GROUP_W, GROUP_W, GROUP_W,
          D_HEADS * LANES, D_HEADS * LANES, D_HEADS * LANES)
PROJ_OFF = tuple(sum(PROJ_W[:i]) for i in range(len(PROJ_W) + 1))
OUT_K = A_HEADS * LANES + 2 * GROUP_W + D_HEADS * LANES

VMEM_LIMIT = 52 * 1024 * 1024


def _cparams(sem):
    return pltpu.CompilerParams(dimension_semantics=sem, vmem_limit_bytes=VMEM_LIMIT)


def _lane_iota(shape):
    return lax.broadcasted_iota(I32, shape, len(shape) - 1)


def _rows_to_tiles(h, out_ref, slab_ref):
    for j in range(ROW_SUB):
        slab_ref[j * ROW_PITCH:j * ROW_PITCH + ROW_TILE, :] = h[:, j * LANES:(j + 1) * LANES]
    for t in range(ROW_TILE):
        out_ref[ROW_SUB * t:ROW_SUB * (t + 1), :] = slab_ref[pl.ds(t, ROW_SUB, stride=ROW_PITCH), :]


def _tiles_to_rows(in_ref, slab_ref):
    for t in range(ROW_TILE):
        slab_ref[pl.ds(t, ROW_SUB, stride=ROW_PITCH), :] = in_ref[ROW_SUB * t:ROW_SUB * (t + 1), :]
    return jnp.concatenate([slab_ref[j * ROW_PITCH:j * ROW_PITCH + ROW_TILE, :] for j in range(ROW_SUB)], axis=1)


def _slab_scratch():
    return pltpu.VMEM((ROW_SUB * ROW_PITCH, LANES), F32)


def _ada_kernel(a_ref, w_ref, b_ref, o_ref):
    o_ref[...] = jnp.dot(a_ref[...], w_ref[...], preferred_element_type=F32,
                         precision=lax.Precision.HIGHEST) + b_ref[...]


def _ada_call(a, ada_w, ada_b):
    depth, d, n = ada_w.shape
    rows = a.shape[0]
    nb = n // d
    return pl.pallas_call(
        _ada_kernel,
        grid=(depth, nb),
        in_specs=[pl.BlockSpec((rows, d), lambda l, j: (0, 0)),
                  pl.BlockSpec((None, d, d), lambda l, j: (l, 0, j)),
                  pl.BlockSpec((None, 1, d), lambda l, j: (l, 0, j))],
        out_specs=pl.BlockSpec((None, rows, d), lambda l, j: (l, 0, j)),
        out_shape=jax.ShapeDtypeStruct((depth, rows, n), F32),
        compiler_params=_cparams(("arbitrary", "arbitrary")),
        name="ada_mod",
    )(a, ada_w, ada_b.reshape(depth, 1, n))


def _swap_halves(x, half):
    lane = _lane_iota(x.shape)
    first = (lane & (2 * half - 1)) < half
    return jnp.where(first, pltpu.roll(x, LANES - half, 1), pltpu.roll(x, half, 1))


def _proj_kernel(x_ref, a_ref, sh_ref, w_ref, gains_ref, ca_ref, sa_ref, cd_ref, sd_ref,
                 qa_ref, ka_ref, va_ref, z_ref, cu_ref, qd_ref, kd_ref, vd_ref):
    x = x_ref[...]
    ms = jnp.mean(x * x, axis=-1, keepdims=True)
    h = (x * lax.rsqrt(ms + EPS) * a_ref[...] + sh_ref[...]).astype(BF16)
    p = jnp.dot(h, w_ref[...], preferred_element_type=F32)
    lane = _lane_iota((TILE, LANES))
    ca, sa, cd, sd = ca_ref[...], sa_ref[...], cd_ref[...], sd_ref[...]

    def slab(sec, i):
        o = PROJ_OFF[sec] + i * LANES
        return p[:, o:o + LANES]

    def norm_rope_a(s, gain):
        msq = jnp.sum(s * s, axis=-1, keepdims=True) * (1.0 / HEAD_DIM)
        sn = s * lax.rsqrt(msq + EPS) * gain
        return (sn * ca + _swap_halves(sn, HEAD_DIM // 4) * sa).astype(BF16)

    def norm_rope_d(s, gain):
        s2 = s * s
        m1 = jnp.sum(jnp.where(lane < D_QKDIM, s2, 0.0), axis=-1, keepdims=True)
        m2 = jnp.sum(jnp.where(lane >= D_QKDIM, s2, 0.0), axis=-1, keepdims=True)
        msq = jnp.where(lane < D_QKDIM, m1, m2) * (1.0 / D_QKDIM)
        sn = s * lax.rsqrt(msq + EPS) * gain
        return (sn * cd + _swap_halves(sn, D_QKDIM // 4) * sd).astype(BF16)

    def with_ones(s, width):
        return jnp.where(lane == width, 1.0, s).astype(BF16)

    for hh in range(A_HEADS):
        qa_ref[hh] = norm_rope_a(slab(0, hh), gains_ref[0:1, :])
    for hh in range(A_KV_HEADS):
        ka_ref[hh] = norm_rope_a(slab(1, hh), gains_ref[1:2, :])
        va_ref[hh] = with_ones(slab(2, hh), HEAD_DIM)
    bu = p[:, PROJ_OFF[3]:PROJ_OFF[4]]
    bg = p[:, PROJ_OFF[4]:PROJ_OFF[5]]
    z_ref[...] = bu * (1.0 / (1.0 + jnp.exp(-bg)))
    cu_ref[...] = p[:, PROJ_OFF[5]:PROJ_OFF[6]].astype(BF16)
    for hh in range(D_HEADS):
        qd_ref[hh] = norm_rope_d(slab(6, hh), gains_ref[2:3, :])
        kd_ref[hh] = norm_rope_d(slab(7, hh), gains_ref[3:4, :])
        vd_ref[hh] = with_ones(slab(8, hh), D_VDIM)


def _proj_call(x_all, a1, sh1, w_in_p, gains, rope, dims):
    B, NT, NL = dims
    T = x_all.shape[0]
    S = NT * TILE
    sel = lambda b, i: (jnp.where(i >= NL, B, b), 0, 0)
    row = lambda b, i: (b * NT + i, 0)
    hrow = lambda b, i: (0, b * NT + i, 0)
    tab = lambda b, i: (i, 0)
    head = lambda n: pl.BlockSpec((n, TILE, LANES), hrow)
    hshape = lambda n: jax.ShapeDtypeStruct((n, T, LANES), BF16)
    nw = w_in_p.shape[1]
    return pl.pallas_call(
        _proj_kernel,
        grid=(B, NT),
        in_specs=[pl.BlockSpec((TILE, D_MODEL), row),
                  pl.BlockSpec((None, 1, D_MODEL), sel),
                  pl.BlockSpec((None, 1, D_MODEL), sel),
                  pl.BlockSpec((D_MODEL, nw), lambda b, i: (0, 0)),
                  pl.BlockSpec((8, LANES), lambda b, i: (0, 0)),
                  pl.BlockSpec((TILE, LANES), tab), pl.BlockSpec((TILE, LANES), tab),
                  pl.BlockSpec((TILE, LANES), tab), pl.BlockSpec((TILE, LANES), tab)],
        out_specs=[head(A_HEADS), head(A_KV_HEADS), head(A_KV_HEADS),
                   pl.BlockSpec((TILE, GROUP_W), row), pl.BlockSpec((TILE, GROUP_W), row),
                   head(D_HEADS), head(D_HEADS), head(D_HEADS)],
        out_shape=[hshape(A_HEADS), hshape(A_KV_HEADS), hshape(A_KV_HEADS),
                   jax.ShapeDtypeStruct((T, GROUP_W), F32), jax.ShapeDtypeStruct((T, GROUP_W), BF16),
                   hshape(D_HEADS), hshape(D_HEADS), hshape(D_HEADS)],
        compiler_params=_cparams(("arbitrary", "arbitrary")),
        name="in_proj",
    )(x_all, a1, sh1, w_in_p, gains, *rope)


def _softmax_pv(load_q, k_ref, v_ref, k0, nchunks, s_scr, m_scr):
    rows = m_scr.shape[0]
    halves = [(h * rows // 2, (h + 1) * rows // 2) for h in range(2)]
    for c in range(nchunks):
        kc = k_ref[k0 + c * KEY_CHUNK:k0 + (c + 1) * KEY_CHUNK, :]
        q = load_q()
        for (r0, r1) in halves:
            s = lax.dot_general(q[r0:r1], kc, (((1,), (1,)), ((), ())), preferred_element_type=F32)
            s_scr[c, r0:r1, :] = s
            part = s[:, 0:LANES]
            for j in range(1, KEY_CHUNK // LANES):
                part = jnp.maximum(part, s[:, j * LANES:(j + 1) * LANES])
            m_scr[r0:r1, :] = part if c == 0 else jnp.maximum(m_scr[r0:r1, :], part)
    mrow = jnp.max(m_scr[...], axis=-1, keepdims=True)
    m_scr[...] = jnp.broadcast_to(mrow, (rows, LANES))
    accs = [jnp.zeros((r1 - r0, LANES), F32) for (r0, r1) in halves]
    for c in range(nchunks):
        vc = v_ref[k0 + c * KEY_CHUNK:k0 + (c + 1) * KEY_CHUNK, :]
        for h, (r0, r1) in enumerate(halves):
            s = s_scr[c, r0:r1, :]
            mb = m_scr[r0:r1, :]
            pr = jnp.concatenate([jnp.exp2(s[:, j * LANES:(j + 1) * LANES] - mb)
                                  for j in range(KEY_CHUNK // LANES)], axis=1).astype(BF16)
            accs[h] = accs[h] + jnp.dot(pr, vc, preferred_element_type=F32)
    return jnp.concatenate(accs, axis=0)


def _attend(i, NL, NT, with_ctx, load_q, k_ref, v_ref, s_scr, m_scr, finish):
    @pl.when(i < NL)
    def _():
        finish(_softmax_pv(load_q, k_ref, v_ref, 0, NT * TILE // KEY_CHUNK, s_scr, m_scr))

    if with_ctx:
        @pl.when(i >= NL)
        def _():
            finish(_softmax_pv(load_q, k_ref, v_ref, NL * TILE, (NT - NL) * TILE // KEY_CHUNK, s_scr, m_scr))


def _gqa_kernel(q_ref, k_ref, v_ref, o_ref, s_scr, m_scr, *, NL, NT, with_ctx):
    group = A_HEADS // A_KV_HEADS
    load_q = lambda: q_ref[...].reshape(group * TILE, LANES)

    def finish(acc):
        o = acc * (1.0 / acc[:, HEAD_DIM:HEAD_DIM + 1])
        o_ref[...] = o.reshape(group, TILE, LANES).astype(BF16)

    _attend(pl.program_id(2), NL, NT, with_ctx, load_q, k_ref, v_ref, s_scr, m_scr, finish)


def _gqa_call(qa, ka, va, dims, ntq):
    B, NT, NL = dims
    T = qa.shape[1]
    S = NT * TILE
    group = A_HEADS // A_KV_HEADS
    qmap = lambda b, g, i: (g, b * NT + i, 0)
    kmap = lambda b, g, i: (g, b, 0)
    return pl.pallas_call(
        functools.partial(_gqa_kernel, NL=NL, NT=NT, with_ctx=ntq > NL),
        grid=(B, A_KV_HEADS, ntq),
        in_specs=[pl.BlockSpec((group, TILE, LANES), qmap),
                  pl.BlockSpec((None, S, LANES), kmap),
                  pl.BlockSpec((None, S, LANES), kmap)],
        out_specs=pl.BlockSpec((group, TILE, LANES), qmap),
        out_shape=jax.ShapeDtypeStruct((A_HEADS, T, LANES), BF16),
        scratch_shapes=[pltpu.VMEM((S // KEY_CHUNK, group * TILE, KEY_CHUNK), F32),
                        pltpu.VMEM((group * TILE, LANES), F32)],
        compiler_params=_cparams(("arbitrary", "arbitrary", "arbitrary")),
        name="gqa_attn",
    )(qa, ka, va)


def _diff_kernel(q_ref, k_ref, v_ref, par_ref, o_ref, q2_scr, s_scr, m_scr, *, NL, NT, with_ctx):
    q = q_ref[...]
    lane = _lane_iota(q.shape)
    zero = jnp.zeros_like(q)
    q2_scr[0:TILE, :] = jnp.where(lane < D_QKDIM, q, zero)
    q2_scr[TILE:2 * TILE, :] = jnp.where(lane >= D_QKDIM, q, zero)

    def finish(acc):
        o1 = acc[:TILE] * (1.0 / acc[:TILE, D_VDIM:D_VDIM + 1])
        o2 = acc[TILE:] * (1.0 / acc[TILE:, D_VDIM:D_VDIM + 1])
        o = jnp.where(lane < D_VDIM, o1 - par_ref[0:1, :] * o2, 0.0)
        msq = jnp.sum(o * o, axis=-1, keepdims=True) * (1.0 / D_VDIM)
        o_ref[...] = (o * lax.rsqrt(msq + EPS) * par_ref[1:2, :]).astype(BF16)

    _attend(pl.program_id(2), NL, NT, with_ctx, lambda: q2_scr[...], k_ref, v_ref, s_scr, m_scr, finish)


def _diff_call(qd, kd, vd, dpar, dims, ntq):
    B, NT, NL = dims
    T = qd.shape[1]
    S = NT * TILE
    qmap = lambda b, h, i: (h, b * NT + i, 0)
    kmap = lambda b, h, i: (h, b, 0)
    return pl.pallas_call(
        functools.partial(_diff_kernel, NL=NL, NT=NT, with_ctx=ntq > NL),
        grid=(B, D_HEADS, ntq),
        in_specs=[pl.BlockSpec((None, TILE, LANES), qmap),
                  pl.BlockSpec((None, S, LANES), kmap),
                  pl.BlockSpec((None, S, LANES), kmap),
                  pl.BlockSpec((8, LANES), lambda b, h, i: (0, 0))],
        out_specs=pl.BlockSpec((None, TILE, LANES), qmap),
        out_shape=jax.ShapeDtypeStruct((D_HEADS, T, LANES), BF16),
        scratch_shapes=[pltpu.VMEM((2 * TILE, LANES), BF16),
                        pltpu.VMEM((S // KEY_CHUNK, 2 * TILE, KEY_CHUNK), F32),
                        pltpu.VMEM((2 * TILE, LANES), F32)],
        compiler_params=_cparams(("arbitrary", "arbitrary", "arbitrary")),
        name="diff_attn",
    )(qd, kd, vd, dpar)


def _conv_kernel(z_ref, w_ref, par_ref, o_ref, zp_ref, *, segs):
    zeros = jnp.zeros((CONV_HALO, GROUP_W), F32)
    pos = 0
    starts = []
    for (r0, n) in segs:
        zp_ref[pos:pos + CONV_HALO, :] = zeros
        zp_ref[pos + CONV_HALO:pos + CONV_HALO + n, :] = z_ref[r0:r0 + n, :]
        starts.append(pos + CONV_HALO)
        pos += CONV_HALO + n
    zp_ref[pos:pos + CONV_HALO, :] = zeros
    bias, ln_g, ln_b = par_ref[0:1, :], par_ref[1:2, :], par_ref[2:3, :]

    for (r0, n), p0 in zip(segs, starts):
        def chunk(j, carry, r0=r0, p0=p0):
            base = pl.multiple_of(j * CONV_ROWS, CONV_ROWS)
            win = zp_ref[pl.ds(base + (p0 - CONV_HALO), CONV_ROWS + 2 * CONV_HALO), :]
            acc = jnp.zeros((CONV_ROWS, GROUP_W), F32)
            for k in range(CONV_W):
                o = CONV_HALO - CONV_W // 2 + k
                acc = acc + win[o:o + CONV_ROWS, :] * w_ref[k:k + 1, :]
            u = acc + bias
            mu = jnp.mean(u, axis=-1, keepdims=True)
            uc = u - mu
            var = jnp.mean(uc * uc, axis=-1, keepdims=True)
            y = uc * lax.rsqrt(var + EPS) * ln_g + ln_b
            o_ref[pl.ds(pl.multiple_of(base + r0, CONV_ROWS), CONV_ROWS), :] = (
                y * (1.0 / (1.0 + jnp.exp(-y)))).astype(BF16)
            return carry

        lax.fori_loop(0, n // CONV_ROWS, chunk, 0)


def _conv_call(z, conv_w, conv_par, dims, with_ctx):
    B, NT, NL = dims
    T = z.shape[0]
    S = NT * TILE
    L = NL * TILE
    segs = ((0, L), (L, S - L)) if with_ctx else ((0, L),)
    pad_rows = sum(n for _, n in segs) + CONV_HALO * (len(segs) + 1)
    return pl.pallas_call(
        functools.partial(_conv_kernel, segs=segs),
        grid=(B,),
        in_specs=[pl.BlockSpec((S, GROUP_W), lambda b: (b, 0)),
                  pl.BlockSpec((32, GROUP_W), lambda b: (0, 0)),
                  pl.BlockSpec((8, GROUP_W), lambda b: (0, 0))],
        out_specs=pl.BlockSpec((S, GROUP_W), lambda b: (b, 0)),
        out_shape=jax.ShapeDtypeStruct((T, GROUP_W), BF16),
        scratch_shapes=[pltpu.VMEM((pad_rows, GROUP_W), F32)],
        compiler_params=_cparams(("arbitrary",)),
        name="conformer_conv",
    )(z, conv_w, conv_par)


def _dft_kernel(c_ref, s_ref, x_ref, cb_ref, sb_ref, o_ref, *, r0, n):
    xs = x_ref[r0:r0 + n, :]
    u = jnp.dot(c_ref[...], xs, preferred_element_type=F32).astype(BF16)
    w = jnp.dot(s_ref[...], xs, preferred_element_type=F32).astype(BF16)
    y = (jnp.dot(u, cb_ref[...], preferred_element_type=F32)
         - jnp.dot(w, sb_ref[...], preferred_element_type=F32))
    o_ref[...] = y.astype(BF16)


def _dft_call(cu, cmat, smat, cbd, sbd, dims, r0, n, prev=None):
    B, NT, NL = dims
    T = cu.shape[0]
    S = NT * TILE
    nr = n // TILE
    t0 = r0 // TILE
    args = [cmat, smat, cu, cbd, sbd]
    in_specs = [pl.BlockSpec((TILE, n), lambda r, b: (r, 0)),
                pl.BlockSpec((TILE, n), lambda r, b: (r, 0)),
                pl.BlockSpec((S, GROUP_W), lambda r, b: (b, 0)),
                pl.BlockSpec((GROUP_W, GROUP_W), lambda r, b: (0, 0)),
                pl.BlockSpec((GROUP_W, GROUP_W), lambda r, b: (0, 0))]
    kern = functools.partial(_dft_kernel, r0=r0, n=n)
    aliases = {}
    if prev is not None:
        args.append(prev)
        in_specs.append(pl.BlockSpec(memory_space=pl.ANY))
        aliases = {5: 0}
        kern = lambda c, s, x, cb, sb, _prev, o: _dft_kernel(c, s, x, cb, sb, o, r0=r0, n=n)
    return pl.pallas_call(
        kern,
        grid=(nr, B),
        in_specs=in_specs,
        out_specs=pl.BlockSpec((TILE, GROUP_W), lambda r, b: (b * NT + t0 + r, 0)),
        out_shape=jax.ShapeDtypeStruct((T, GROUP_W), BF16),
        input_output_aliases=aliases,
        compiler_params=_cparams(("arbitrary", "arbitrary")),
        name="fourier_mix",
    )(*args)


def _out_kernel(x_ref, oa_ref, ob_ref, oc_ref, od_ref, w_ref, g1_ref, a2_ref, sh2_ref, rw_ref, rb_ref,
                xn_ref, h2_ref, ti_ref, gt_ref, cnt_ref, slab_ref):
    ocat = jnp.concatenate([oa_ref[hh] for hh in range(A_HEADS)] + [ob_ref[...], oc_ref[...]]
                           + [od_ref[hh] for hh in range(D_HEADS)], axis=1)
    r = jnp.dot(ocat, w_ref[...], preferred_element_type=F32)
    xn = x_ref[...] + g1_ref[...] * r
    xn_ref[...] = xn
    ms = jnp.mean(xn * xn, axis=-1, keepdims=True)
    h2 = xn * lax.rsqrt(ms + EPS) * a2_ref[...] + sh2_ref[...]
    _rows_to_tiles(h2, h2_ref, slab_ref)
    logits = jnp.dot(h2.astype(BF16), rw_ref[...], preferred_element_type=F32) + rb_ref[...]
    lane = _lane_iota(logits.shape)
    lanef = lane.astype(F32)
    vals, idxs = [], []
    cur = logits
    for _ in range(TOP_K):
        mx = jnp.max(cur, axis=-1, keepdims=True)
        idx = jnp.min(jnp.where(cur == mx, lanef, float(LANES)), axis=-1, keepdims=True)
        vals.append(mx)
        idxs.append(idx)
        cur = jnp.where(lanef == idx, NEG_BIG * 2, cur)
    exps = [jnp.exp(v - vals[0]) for v in vals]
    inv = 1.0 / (exps[0] + exps[1] + exps[2] + exps[3])
    ti = jnp.zeros(logits.shape, F32)
    gt = jnp.zeros(logits.shape, F32)
    onehot = jnp.zeros(logits.shape, F32)
    for k in range(TOP_K):
        ti = jnp.where(lane == k, idxs[k], ti)
        gt = jnp.where(lane == k, exps[k] * inv, gt)
        onehot = onehot + jnp.where(lanef == idxs[k], 1.0, 0.0)
    ti_ref[...] = ti.astype(I32)
    gt_ref[...] = gt

    @pl.when((pl.program_id(0) == 0) & (pl.program_id(1) == 0))
    def _():
        cnt_ref[...] = jnp.zeros(cnt_ref.shape, F32)

    cnt_ref[0:1, :] += jnp.sum(onehot, axis=0, keepdims=True)


def _out_call(x_all, oa, ob, oc, od, w_out_p, g1, a2, sh2, rw, rb, dims, ntq):
    B, NT, NL = dims
    n_tok = B * ntq * TILE
    sel = lambda b, i: (jnp.where(i >= NL, B, b), 0, 0)
    row = lambda b, i: (b * NT + i, 0)
    hrow = lambda b, i: (0, b * NT + i, 0)
    crow = lambda b, i: (b * ntq + i, 0)
    const = lambda b, i: (0, 0)
    return pl.pallas_call(
        _out_kernel,
        grid=(B, ntq),
        in_specs=[pl.BlockSpec((TILE, D_MODEL), row),
                  pl.BlockSpec((A_HEADS, TILE, LANES), hrow),
                  pl.BlockSpec((TILE, GROUP_W), row),
                  pl.BlockSpec((TILE, GROUP_W), row),
                  pl.BlockSpec((D_HEADS, TILE, LANES), hrow),
                  pl.BlockSpec((OUT_K, D_MODEL), const),
                  pl.BlockSpec((None, 1, D_MODEL), sel),
                  pl.BlockSpec((None, 1, D_MODEL), sel),
                  pl.BlockSpec((None, 1, D_MODEL), sel),
                  pl.BlockSpec((D_MODEL, LANES), const),
                  pl.BlockSpec((1, LANES), const)],
        out_specs=[pl.BlockSpec((TILE, D_MODEL), crow),
                   pl.BlockSpec((TILE * ROW_SUB, LANES), crow),
                   pl.BlockSpec((TILE, LANES), crow),
                   pl.BlockSpec((TILE, LANES), crow),
                   pl.BlockSpec((8, LANES), const)],
        out_shape=[jax.ShapeDtypeStruct((n_tok, D_MODEL), F32),
                   jax.ShapeDtypeStruct((n_tok * ROW_SUB, LANES), F32),
                   jax.ShapeDtypeStruct((n_tok, LANES), I32),
                   jax.ShapeDtypeStruct((n_tok, LANES), F32),
                   jax.ShapeDtypeStruct((8, LANES), F32)],
        scratch_shapes=[_slab_scratch()],
        compiler_params=_cparams(("arbitrary", "arbitrary")),
        name="out_proj_router",
    )(x_all, oa, ob, oc, od, w_out_p, g1, a2, sh2, rw, rb)


def _rank_kernel(ti_ref, ps_ref, tri_ref, dest_ref, carry_ref):
    @pl.when(pl.program_id(0) == 0)
    def _():
        carry_ref[...] = jnp.zeros(carry_ref.shape, F32)

    ti = ti_ref[...]
    lane = _lane_iota(ti.shape)
    hits = [lane == ti[:, k:k + 1] for k in range(TOP_K)]
    onehot = jnp.zeros(ti.shape, F32)
    for hk in hits:
        onehot = onehot + jnp.where(hk, 1.0, 0.0)
    before = jnp.dot(tri_ref[...], onehot.astype(BF16), preferred_element_type=F32)
    base = before + carry_ref[0:1, :] + ps_ref[...]
    dest = jnp.zeros(ti.shape, F32)
    for k, hk in enumerate(hits):
        dk = jnp.sum(jnp.where(hk, base, 0.0), axis=-1, keepdims=True)
        dest = jnp.where(lane == k, dk, dest)
    dest_ref[...] = dest.astype(I32)
    carry_ref[0:1, :] += jnp.sum(onehot, axis=0, keepdims=True)


def _rank_call(topi, pad_start, tri):
    n_tok = topi.shape[0]
    return pl.pallas_call(
        _rank_kernel,
        grid=(n_tok // TILE,),
        in_specs=[pl.BlockSpec((TILE, LANES), lambda t: (t, 0)),
                  pl.BlockSpec((1, LANES), lambda t: (0, 0)),
                  pl.BlockSpec((TILE, TILE), lambda t: (0, 0))],
        out_specs=pl.BlockSpec((TILE, LANES), lambda t: (t, 0)),
        out_shape=jax.ShapeDtypeStruct((n_tok, LANES), I32),
        scratch_shapes=[pltpu.VMEM((8, LANES), F32)],
        compiler_params=_cparams(("arbitrary",)),
        name="slot_rank",
    )(topi, pad_start, tri)


def _tile_copy(src_ref, src_row, dst_ref, dst_row, sem):
    src = src_ref.at[pl.ds(pl.multiple_of(src_row * ROW_SUB, ROW_SUB), ROW_SUB)]
    dst = dst_ref.at[pl.ds(pl.multiple_of(dst_row * ROW_SUB, ROW_SUB), ROW_SUB)]
    return pltpu.make_async_copy(src, dst, sem)


def _dispatch_kernel(fill_start_ref, fill_on_ref, dest_ref, h2_ref, zero_ref, xs_ref, sem):
    @pl.when(pl.program_id(0) == 0)
    def _():
        def fill(e, wait):
            @pl.when(fill_on_ref[e] > 0)
            def _():
                start = pl.multiple_of(fill_start_ref[e] * ROW_SUB, MOE_BLOCK * ROW_SUB)
                cp = pltpu.make_async_copy(zero_ref, xs_ref.at[pl.ds(start, MOE_BLOCK * ROW_SUB)], sem.at[1])
                if wait:
                    cp.wait()
                else:
                    cp.start()

        lax.fori_loop(0, N_EXPERTS, lambda e, c: (fill(e, False), c)[1], 0)
        lax.fori_loop(0, N_EXPERTS, lambda e, c: (fill(e, True), c)[1], 0)

    def issue(r, c):
        for k in range(TOP_K):
            _tile_copy(h2_ref, r, xs_ref, dest_ref[0, 0, TOP_K * r + k], sem.at[0]).start(priority=k % 2)
        return c

    lax.fori_loop(0, TILE, issue, 0, unroll=DMA_UNROLL)
    for k in range(TOP_K):
        pltpu.make_async_copy(h2_ref, xs_ref.at[pl.ds(0, TILE * ROW_SUB)], sem.at[0]).wait()


def _dispatch_call(fill_start, fill_on, dest3, h2t, n_slots):
    n_tok = h2t.shape[0] // ROW_SUB
    zero_blk = jnp.zeros((MOE_BLOCK * ROW_SUB, LANES), F32)
    return pl.pallas_call(
        _dispatch_kernel,
        grid_spec=pltpu.PrefetchScalarGridSpec(
            num_scalar_prefetch=2,
            grid=(n_tok // TILE,),
            in_specs=[pl.BlockSpec((1, 1, TOP_K * TILE), lambda t, fs, fo: (t, 0, 0), memory_space=pltpu.SMEM),
                      pl.BlockSpec((TILE * ROW_SUB, LANES), lambda t, fs, fo: (t, 0)),
                      pl.BlockSpec((MOE_BLOCK * ROW_SUB, LANES), lambda t, fs, fo: (0, 0))],
            out_specs=pl.BlockSpec(memory_space=pl.ANY),
            scratch_shapes=[pltpu.SemaphoreType.DMA((2,))]),
        out_shape=jax.ShapeDtypeStruct((n_slots * ROW_SUB, LANES), F32),
        compiler_params=_cparams(("arbitrary",)),
        name="moe_dispatch",
    )(fill_start, fill_on, dest3, h2t, zero_blk)


def _expert_kernel(blk_e_ref, nact_ref, xs_ref, wgu_ref, bgu_ref, wd_ref, bd_ref, yb_ref, slab_ref):
    @pl.when(pl.program_id(0) < nact_ref[0])
    def _():
        x = _tiles_to_rows(xs_ref, slab_ref).astype(BF16)
        gu = jnp.dot(x, wgu_ref[...], preferred_element_type=F32) + bgu_ref[...]
        g = jnp.minimum(gu[:, :D_FF], SWIGLU_LIMIT)
        u = jnp.clip(gu[:, D_FF:], -SWIGLU_LIMIT, SWIGLU_LIMIT)
        act = g * (1.0 / (1.0 + jnp.exp(-SWIGLU_ALPHA * g))) * (u + 1.0)
        y = jnp.dot(act.astype(BF16), wd_ref[...], preferred_element_type=F32) + bd_ref[...]
        _rows_to_tiles(y, yb_ref, slab_ref)


def _expert_call(blk_e, n_active, xs, w_gu, b_gu, w_down, b_down):
    n_slots = xs.shape[0] // ROW_SUB
    n_blocks = n_slots // MOE_BLOCK
    blk = lambda j, be, na: (jnp.minimum(j, na[0] - 1), 0)
    wsel = lambda j, be, na: (be[jnp.minimum(j, na[0] - 1)], 0, 0)
    return pl.pallas_call(
        _expert_kernel,
        grid_spec=pltpu.PrefetchScalarGridSpec(
            num_scalar_prefetch=2,
            grid=(n_blocks,),
            in_specs=[pl.BlockSpec((MOE_BLOCK * ROW_SUB, LANES), blk),
                      pl.BlockSpec((None, D_MODEL, 2 * D_FF), wsel),
                      pl.BlockSpec((None, 1, 2 * D_FF), wsel),
                      pl.BlockSpec((None, D_FF, D_MODEL), wsel),
                      pl.BlockSpec((None, 1, D_MODEL), wsel)],
            out_specs=pl.BlockSpec((MOE_BLOCK * ROW_SUB, LANES), blk),
            scratch_shapes=[_slab_scratch()]),
        out_shape=jax.ShapeDtypeStruct((n_slots * ROW_SUB, LANES), F32),
        compiler_params=_cparams(("arbitrary",)),
        name="moe_experts",
    )(blk_e, n_active, xs, w_gu, b_gu, w_down, b_down)


def _combine_kernel(dest_ref, x_ref, gt_ref, g2_ref, yb_ref, o_ref, buf, slab_ref, sem):
    def issue(r, c):
        for k in range(TOP_K):
            _tile_copy(yb_ref, dest_ref[0, 0, TOP_K * r + k], buf.at[k], r, sem.at[0]).start(priority=k % 2)
        return c

    lax.fori_loop(0, TILE, issue, 0, unroll=DMA_UNROLL)
    for k in range(TOP_K):
        pltpu.make_async_copy(yb_ref.at[pl.ds(0, TILE * ROW_SUB)], buf.at[k], sem.at[0]).wait()
    gt = gt_ref[...]
    y = gt[:, 0:1] * _tiles_to_rows(buf.at[0], slab_ref)
    for k in range(1, TOP_K):
        y = y + gt[:, k:k + 1] * _tiles_to_rows(buf.at[k], slab_ref)
    o_ref[...] = x_ref[...] + g2_ref[...] * y


def _combine_call(dest3, xn, gates, g2, yb, dims, ntq):
    B, NT, NL = dims
    n_tok = xn.shape[0]
    sel = lambda b, i: (jnp.where(i >= NL, B, b), 0, 0)
    crow = lambda b, i: (b * ntq + i, 0)
    return pl.pallas_call(
        _combine_kernel,
        grid=(B, ntq),
        in_specs=[pl.BlockSpec((1, 1, TOP_K * TILE), lambda b, i: (b * ntq + i, 0, 0), memory_space=pltpu.SMEM),
                  pl.BlockSpec((TILE, D_MODEL), crow),
                  pl.BlockSpec((TILE, LANES), crow),
                  pl.BlockSpec((None, 1, D_MODEL), sel),
                  pl.BlockSpec(memory_space=pl.ANY)],
        out_specs=pl.BlockSpec((TILE, D_MODEL), crow),
        out_shape=jax.ShapeDtypeStruct((n_tok, D_MODEL), F32),
        scratch_shapes=[pltpu.VMEM((TOP_K, TILE * ROW_SUB, LANES), F32), _slab_scratch(),
                        pltpu.SemaphoreType.DMA((1,))],
        compiler_params=_cparams(("arbitrary", "arbitrary")),
        name="moe_combine",
    )(dest3, xn, gates, g2, yb)


def _pad_heads_cols(w, n_heads, width):
    w = w.reshape(w.shape[0], n_heads, width)
    return jnp.pad(w, ((0, 0), (0, 0), (0, LANES - width))).reshape(w.shape[0], n_heads * LANES)


def _pad_heads_rows(w, n_heads, width):
    w = w.reshape(n_heads, width, w.shape[1])
    return jnp.pad(w, ((0, 0), (0, LANES - width), (0, 0))).reshape(n_heads * LANES, w.shape[2])


def _prep_w_in(w):
    splits = (256, 128, 128, 256, 256, 256, 256, 256, 256)
    parts, o = [], 0
    for s in splits:
        parts.append(w[:, o:o + s])
        o += s
    aq, ak, av, bu, bg, cu, dq, dk, dv = parts
    return jnp.concatenate([
        _pad_heads_cols(aq, A_HEADS, HEAD_DIM), _pad_heads_cols(ak, A_KV_HEADS, HEAD_DIM),
        _pad_heads_cols(av, A_KV_HEADS, HEAD_DIM), bu, bg, cu,
        _pad_heads_cols(dq, D_HEADS, 2 * D_QKDIM), _pad_heads_cols(dk, D_HEADS, 2 * D_QKDIM),
        _pad_heads_cols(dv, D_HEADS, D_VDIM)], axis=1).astype(BF16)


def _prep_w_out(w):
    return jnp.concatenate([
        _pad_heads_rows(w[0:GROUP_W], A_HEADS, HEAD_DIM), w[GROUP_W:3 * GROUP_W],
        _pad_heads_rows(w[3 * GROUP_W:], D_HEADS, D_VDIM)], axis=0).astype(BF16)


def _pad_lanes(v):
    return jnp.pad(v.astype(F32), (0, LANES - v.shape[0]))


def _rope_tables(L, C):
    t = jnp.arange(L, dtype=jnp.int32)
    rows = (t // GRID_W).astype(F32)[:, None]
    cols = (t % GRID_W).astype(F32)[:, None]
    lane = jnp.arange(LANES)

    def table(group):
        nfreq = group // 2
        j = lane % nfreq
        inv = jnp.power(ROPE_THETA, -(2.0 * j.astype(F32)) / group)[None, :]
        use_cols = ((lane // group) % 2) == 1
        ang = jnp.where(use_cols[None, :], cols, rows) * inv
        sign = jnp.where((lane % group) < nfreq, -1.0, 1.0)[None, :]
        live = (lane < 2 * D_QKDIM)[None, :]
        cos = jnp.where(live, jnp.cos(ang), 1.0)
        sin = jnp.where(live, jnp.sin(ang) * sign, 0.0)
        cos = jnp.concatenate([cos, jnp.ones((C, LANES), F32)], axis=0)
        sin = jnp.concatenate([sin, jnp.zeros((C, LANES), F32)], axis=0)
        return cos, sin

    ca, sa = table(HEAD_DIM // 2)
    cd, sd = table(D_QKDIM // 2)
    return ca, sa, cd, sd


def _dft_mats(n):
    k = jnp.arange(n, dtype=jnp.int32)
    prod = (k[:, None] * k[None, :]) % n
    ang = prod.astype(F32) * (2.0 * math.pi / n)
    scale = 1.0 / math.sqrt(n)
    return (jnp.cos(ang) * scale).astype(BF16), (jnp.sin(ang) * scale).astype(BF16)


def _channel_dft_mats():
    w = GROUP_W // FNET_GROUPS
    k = jnp.arange(GROUP_W, dtype=jnp.int32)
    same = (k[:, None] // w) == (k[None, :] // w)
    prod = ((k[:, None] % w) * (k[None, :] % w)) % w
    ang = prod.astype(F32) * (2.0 * math.pi / w)
    scale = 1.0 / math.sqrt(w)
    return (jnp.where(same, jnp.cos(ang) * scale, 0.0).astype(BF16),
            jnp.where(same, jnp.sin(ang) * scale, 0.0).astype(BF16))


def _slot_plan(counts, n_blocks):
    cnt = counts[0, :N_EXPERTS].astype(I32)
    padded = (cnt + MOE_BLOCK - 1) // MOE_BLOCK * MOE_BLOCK
    pad_ends = jnp.cumsum(padded)
    pad_starts = pad_ends - padded
    n_active = (pad_ends[-1] // MOE_BLOCK).reshape(1).astype(I32)
    blk_start = jnp.arange(n_blocks, dtype=I32) * MOE_BLOCK
    blk_e = jnp.minimum(jnp.sum(pad_ends[None, :] <= blk_start[:, None], axis=1), N_EXPERTS - 1).astype(I32)
    fill_start = jnp.maximum(pad_ends - MOE_BLOCK, 0).astype(I32)
    fill_on = (cnt > 0).astype(I32)
    ps_vec = jnp.pad(pad_starts.astype(F32), (0, LANES - N_EXPERTS)).reshape(1, LANES)
    return ps_vec, blk_e, n_active, fill_start, fill_on


def kernel(x, c, ctx, c_ctx, norm1_g, norm2_g, ada_w, ada_b, w_in, a_qnorm_g, a_knorm_g, conv_dw_w, conv_dw_b,
           conv_ln_g, conv_ln_b, d_qnorm_g, d_knorm_g, d_lambda, d_subln_g, w_out, router_w, router_b,
           exp_w_gu, exp_b_gu, exp_w_down, exp_b_down):
    B, L, D = x.shape
    C = ctx.shape[1]
    depth = ada_w.shape[0]
    assert D == D_MODEL and L % TILE == 0 and C % TILE == 0 and L % GRID_W == 0
    NL, NT = L // TILE, (L + C) // TILE
    dims = (B, NT, NL)
    S = NT * TILE

    x_all = jnp.concatenate([x, ctx], axis=1).reshape(B * S, D)

    a = jnp.concatenate([c, c_ctx[None, :]], axis=0)
    a = jnp.pad(a * jax.nn.sigmoid(a), ((0, 16 - (B + 1) % 16 if (B + 1) % 16 else 0), (0, 0)))
    mod_all = _ada_call(a, ada_w, ada_b)

    rope = _rope_tables(L, C)
    cmat, smat = _dft_mats(L)
    cmat_c, smat_c = _dft_mats(C)
    cbd, sbd = _channel_dft_mats()
    tri = jnp.tril(jnp.ones((TILE, TILE), F32), -1).astype(BF16)

    x_lat = None
    for li in range(depth):
        last = li == depth - 1
        ntq = NL if last else NT
        mod = mod_all[li, :B + 1].reshape(B + 1, 6, 1, D)
        sh1, sc1, g1, sh2, sc2, g2 = (mod[:, j] for j in range(6))
        a1 = norm1_g[li][None, None, :] * (1.0 + sc1)
        a2 = norm2_g[li][None, None, :] * (1.0 + sc2)

        gains = jnp.stack([
            _pad_lanes(a_qnorm_g[li]) * (HEAD_DIM ** -0.5 * LOG2E), _pad_lanes(a_knorm_g[li]),
            _pad_lanes(jnp.tile(d_qnorm_g[li], 2)) * (D_QKDIM ** -0.5 * LOG2E), _pad_lanes(jnp.tile(d_knorm_g[li], 2)),
        ] + [jnp.zeros((LANES,), F32)] * 4)
        qa, ka, va, z, cu, qd, kd, vd = _proj_call(x_all, a1, sh1, _prep_w_in(w_in[li]), gains, rope, dims)

        lam_init = 0.8 - 0.6 * math.exp(-0.3 * li)
        lq1, lk1, lq2, lk2 = d_lambda[li].astype(F32)
        lam = jnp.exp(jnp.sum(lq1 * lk1)) - jnp.exp(jnp.sum(lq2 * lk2)) + lam_init
        dpar = jnp.stack([jnp.full((LANES,), lam, F32), _pad_lanes(d_subln_g[li]) * (1.0 - lam_init)]
                         + [jnp.zeros((LANES,), F32)] * 6)

        oa = _gqa_call(qa, ka, va, dims, ntq)
        od = _diff_call(qd, kd, vd, dpar, dims, ntq)
        conv_w = jnp.pad(conv_dw_w[li], ((0, 32 - CONV_W), (0, 0)))
        conv_par = jnp.stack([conv_dw_b[li], conv_ln_g[li], conv_ln_b[li]] + [jnp.zeros((GROUP_W,), F32)] * 5)
        ob = _conv_call(z, conv_w, conv_par, dims, with_ctx=not last)
        oc = _dft_call(cu, cmat, smat, cbd, sbd, dims, 0, L)
        if not last:
            oc = _dft_call(cu, cmat_c, smat_c, cbd, sbd, dims, L, C, prev=oc)

        rw = jnp.pad(router_w[li], ((0, 0), (0, LANES - N_EXPERTS))).astype(BF16)
        rb = jnp.concatenate([router_b[li].astype(F32), jnp.full((LANES - N_EXPERTS,), NEG_BIG, F32)]).reshape(1, LANES)
        xn, h2t, topi, gates, counts = _out_call(x_all, oa, ob, oc, od, _prep_w_out(w_out[li]), g1, a2, sh2,
                                                 rw, rb, dims, ntq)

        n_tok = B * ntq * TILE
        n_blocks = -(-(n_tok * TOP_K + N_EXPERTS * (MOE_BLOCK - 1)) // MOE_BLOCK)
        ps_vec, blk_e, n_active, fill_start, fill_on = _slot_plan(counts, n_blocks)
        dest = _rank_call(topi, ps_vec, tri)
        dest3 = dest[:, :TOP_K].reshape(n_tok // TILE, 1, TOP_K * TILE)
        xs = _dispatch_call(fill_start, fill_on, dest3, h2t, n_blocks * MOE_BLOCK)
        yb = _expert_call(blk_e, n_active, xs, exp_w_gu[li].astype(BF16), exp_b_gu[li][:, None, :],
                          exp_w_down[li].astype(BF16), exp_b_down[li][:, None, :])
        x_next = _combine_call(dest3, xn, gates, g2, yb, dims, ntq)
        if last:
            x_lat = x_next
        else:
            x_all = x_next
    return x_lat.reshape(B, L, D)
```

```python
import functools
import math

import jax
import jax.numpy as jnp
from jax import lax
from jax.experimental import pallas as pl
from jax.experimental.pallas import tpu as pltpu

F32 = jnp.float32
BF16 = jnp.bfloat16
I32 = jnp.int32

D_MODEL = 1024
TILE = 256
LANES = 128
GRID_W = 64
HEAD_DIM = 64
A_HEADS = 4
A_KV_HEADS = 2
CONV_W = 31
CONV_HALO = 16
CONV_ROWS = 128
FNET_GROUPS = 4
D_HEADS = 4
D_QKDIM = 32
D_VDIM = 64
GROUP_W = 256
N_EXPERTS = 32
TOP_K = 4
D_FF = D_MODEL
SWIGLU_LIMIT = 7.0
SWIGLU_ALPHA = 1.702
MOE_BLOCK = 256
KEY_CHUNK = 256
DMA_UNROLL = 8
ROPE_THETA = 10000.0
EPS = 1e-6
NEG_BIG = -1e30
LOG2E = math.log2(math.e)
SUBLANES = 8
ROW_SUB = D_MODEL // LANES
assert ROW_SUB == SUBLANES
ROW_TILE = TILE
ROW_PITCH = ROW_TILE + SUBLANES

PROJ_W = (A_HEADS * LANES, A_KV_HEADS * LANES, A_KV_HEADS * LANES, GROUP_W, GROUP_W, GROUP_W,
          D_HEADS * LANES, D_HEADS * LANES, D_HEADS * LANES)
PROJ_OFF = tuple(sum(PROJ_W[:i]) for i in range(len(PROJ_W) + 1))
OUT_K = A_HEADS * LANES + 2 * GROUP_W + D_HEADS * LANES

VMEM_LIMIT = 52 * 1024 * 1024


def _cparams(sem):
    return pltpu.CompilerParams(dimension_semantics=sem, vmem_limit_bytes=VMEM_LIMIT)


def _lane_iota(shape):
    return lax.broadcasted_iota(I32, shape, len(shape) - 1)


def _rows_to_tiles(h, out_ref, slab_ref):
    for j in range(ROW_SUB):
        slab_ref[j * ROW_PITCH:j * ROW_PITCH + ROW_TILE, :] = h[:, j * LANES:(j + 1) * LANES]
    for t in range(ROW_TILE):
        out_ref[ROW_SUB * t:ROW_SUB * (t + 1), :] = slab_ref[pl.ds(t, ROW_SUB, stride=ROW_PITCH), :]


def _tiles_to_rows(in_ref, slab_ref):
    for t in range(ROW_TILE):
        slab_ref[pl.ds(t, ROW_SUB, stride=ROW_PITCH), :] = in_ref[ROW_SUB * t:ROW_SUB * (t + 1), :]
    return jnp.concatenate([slab_ref[j * ROW_PITCH:j * ROW_PITCH + ROW_TILE, :] for j in range(ROW_SUB)], axis=1)


def _slab_scratch():
    return pltpu.VMEM((ROW_SUB * ROW_PITCH, LANES), F32)


def _ada_kernel(a_ref, w_ref, b_ref, o_ref):
    o_ref[...] = jnp.dot(a_ref[...], w_ref[...], preferred_element_type=F32,
                         precision=lax.Precision.HIGHEST) + b_ref[...]


def _ada_call(a, ada_w, ada_b):
    depth, d, n = ada_w.shape
    rows = a.shape[0]
    nb = n // d
    return pl.pallas_call(
        _ada_kernel,
        grid=(depth, nb),
        in_specs=[pl.BlockSpec((rows, d), lambda l, j: (0, 0)),
                  pl.BlockSpec((None, d, d), lambda l, j: (l, 0, j)),
                  pl.BlockSpec((None, 1, d), lambda l, j: (l, 0, j))],
        out_specs=pl.BlockSpec((None, rows, d), lambda l, j: (l, 0, j)),
        out_shape=jax.ShapeDtypeStruct((depth, rows, n), F32),
        compiler_params=_cparams(("arbitrary", "arbitrary")),
        name="ada_mod",
    )(a, ada_w, ada_b.reshape(depth, 1, n))


def _swap_halves(x, half):
    lane = _lane_iota(x.shape)
    first = (lane & (2 * half - 1)) < half
    return jnp.where(first, pltpu.roll(x, LANES - half, 1), pltpu.roll(x, half, 1))


def _proj_kernel(x_ref, a_ref, sh_ref, w_ref, gains_ref, ca_ref, sa_ref, cd_ref, sd_ref,
                 qa_ref, ka_ref, va_ref, z_ref, cu_ref, qd_ref, kd_ref, vd_ref):
    x = x_ref[...]
    ms = jnp.mean(x * x, axis=-1, keepdims=True)
    h = (x * lax.rsqrt(ms + EPS) * a_ref[...] + sh_ref[...]).astype(BF16)
    p = jnp.dot(h, w_ref[...], preferred_element_type=F32)
    lane = _lane_iota((TILE, LANES))
    ca, sa, cd, sd = ca_ref[...], sa_ref[...], cd_ref[...], sd_ref[...]

    def slab(sec, i):
        o = PROJ_OFF[sec] + i * LANES
        return p[:, o:o + LANES]

    def norm_rope_a(s, gain):
        msq = jnp.sum(s * s, axis=-1, keepdims=True) * (1.0 / HEAD_DIM)
        sn = s * lax.rsqrt(msq + EPS) * gain
        return (sn * ca + _swap_halves(sn, HEAD_DIM // 4) * sa).astype(BF16)

    def norm_rope_d(s, gain):
        s2 = s * s
        m1 = jnp.sum(jnp.where(lane < D_QKDIM, s2, 0.0), axis=-1, keepdims=True)
        m2 = jnp.sum(jnp.where(lane >= D_QKDIM, s2, 0.0), axis=-1, keepdims=True)
        msq = jnp.where(lane < D_QKDIM, m1, m2) * (1.0 / D_QKDIM)
        sn = s * lax.rsqrt(msq + EPS) * gain
        return (sn * cd + _swap_halves(sn, D_QKDIM // 4) * sd).astype(BF16)

    def with_ones(s, width):
        st = s.T
        chan = lax.broadcasted_iota(I32, st.shape, 0)
        return jnp.where(chan == width, 1.0, st).astype(BF16)

    for hh in range(A_HEADS):
        qa_ref[hh] = norm_rope_a(slab(0, hh), gains_ref[0:1, :])
    for hh in range(A_KV_HEADS):
        ka_ref[hh] = norm_rope_a(slab(1, hh), gains_ref[1:2, :])
        va_ref[hh] = with_ones(slab(2, hh), HEAD_DIM)
    bu = p[:, PROJ_OFF[3]:PROJ_OFF[4]]
    bg = p[:, PROJ_OFF[4]:PROJ_OFF[5]]
    z_ref[...] = bu * (1.0 / (1.0 + jnp.exp(-bg)))
    cu_ref[...] = p[:, PROJ_OFF[5]:PROJ_OFF[6]].astype(BF16)
    for hh in range(D_HEADS):
        qd_ref[hh] = norm_rope_d(slab(6, hh), gains_ref[2:3, :])
        kd_ref[hh] = norm_rope_d(slab(7, hh), gains_ref[3:4, :])
        vd_ref[hh] = with_ones(slab(8, hh), D_VDIM)


def _proj_call(x_all, a1, sh1, w_in_p, gains, rope, dims):
    B, NT, NL = dims
    T = x_all.shape[0]
    S = NT * TILE
    sel = lambda b, i: (jnp.where(i >= NL, B, b), 0, 0)
    row = lambda b, i: (b * NT + i, 0)
    hrow = lambda b, i: (0, b * NT + i, 0)
    tab = lambda b, i: (i, 0)
    head = lambda n: pl.BlockSpec((n, TILE, LANES), hrow)
    hshape = lambda n: jax.ShapeDtypeStruct((n, T, LANES), BF16)
    head_t = lambda n: pl.BlockSpec((n, None, LANES, TILE), lambda b, i: (0, b, 0, i))
    hshape_t = lambda n: jax.ShapeDtypeStruct((n, B, LANES, S), BF16)
    nw = w_in_p.shape[1]
    return pl.pallas_call(
        _proj_kernel,
        grid=(B, NT),
        in_specs=[pl.BlockSpec((TILE, D_MODEL), row),
                  pl.BlockSpec((None, 1, D_MODEL), sel),
                  pl.BlockSpec((None, 1, D_MODEL), sel),
                  pl.BlockSpec((D_MODEL, nw), lambda b, i: (0, 0)),
                  pl.BlockSpec((8, LANES), lambda b, i: (0, 0)),
                  pl.BlockSpec((TILE, LANES), tab), pl.BlockSpec((TILE, LANES), tab),
                  pl.BlockSpec((TILE, LANES), tab), pl.BlockSpec((TILE, LANES), tab)],
        out_specs=[head(A_HEADS), head(A_KV_HEADS), head_t(A_KV_HEADS),
                   pl.BlockSpec((TILE, GROUP_W), row), pl.BlockSpec((TILE, GROUP_W), row),
                   head(D_HEADS), head(D_HEADS), head_t(D_HEADS)],
        out_shape=[hshape(A_HEADS), hshape(A_KV_HEADS), hshape_t(A_KV_HEADS),
                   jax.ShapeDtypeStruct((T, GROUP_W), F32), jax.ShapeDtypeStruct((T, GROUP_W), BF16),
                   hshape(D_HEADS), hshape(D_HEADS), hshape_t(D_HEADS)],
        compiler_params=_cparams(("arbitrary", "arbitrary")),
        name="in_proj",
    )(x_all, a1, sh1, w_in_p, gains, *rope)


def _scores_phase(q, k_ref, k0, nchunks, s_ref, m_ref):
    half = q.shape[0] // 2
    nk = nchunks * KEY_CHUNK
    for h in range(2):
        s_ref[h, 0:nk, :] = lax.dot_general(k_ref[k0:k0 + nk, :], q[h * half:(h + 1) * half],
                                            (((1,), (1,)), ((), ())), preferred_element_type=F32)
    for h in range(2):
        m8 = None
        for c in range(nchunks):
            blk = s_ref[h, c * KEY_CHUNK:(c + 1) * KEY_CHUNK, :]
            part = jnp.max(blk.reshape(KEY_CHUNK // SUBLANES, SUBLANES, half), axis=0)
            m8 = part if m8 is None else jnp.maximum(m8, part)
        m_ref[h] = m8


def _values_phase(vt_ref, k0, nchunks, s_ref, m_ref):
    half = s_ref.shape[2]
    maxes = [jnp.max(m_ref[h], axis=0, keepdims=True) for h in range(2)]
    accs = [jnp.zeros((LANES, half), F32) for _ in range(2)]
    for c in range(nchunks):
        vc = vt_ref[:, k0 + c * KEY_CHUNK:k0 + (c + 1) * KEY_CHUNK]
        for h in range(2):
            pr = jnp.exp2(s_ref[h, c * KEY_CHUNK:(c + 1) * KEY_CHUNK, :] - maxes[h]).astype(BF16)
            accs[h] = accs[h] + jnp.dot(vc, pr, preferred_element_type=F32)
    return jnp.concatenate(accs, axis=1)


def _attend(i, NL, NT, ntq, load_q, k_ref, vt_ref, s_scr, m_scr, a_scr, finish):
    lat = (0, NT * TILE // KEY_CHUNK)
    ctx = (NL * TILE, (NT - NL) * TILE // KEY_CHUNK)

    def case(step, score_keys, value_keys, store):
        w = step % 2
        if store:
            finish(a_scr[w][...])
        if score_keys is not None:
            _scores_phase(load_q(), k_ref, score_keys[0], score_keys[1], s_scr[w], m_scr[w])
        if value_keys is not None:
            a_scr[1 - w][...] = _values_phase(vt_ref, value_keys[0], value_keys[1], s_scr[1 - w], m_scr[1 - w])

    def at(cond, step, score_keys, value_keys, store):
        pl.when(cond)(lambda: case(step, score_keys, value_keys, store))

    at(i == 0, 0, lat, None, False)
    at(i == 1, 1, lat, lat, False)
    for parity in range(2):
        at((i >= 2) & (i < NL) & (lax.rem(i, 2) == parity), parity, lat, lat, True)
    if ntq > NL:
        at(i == NL, NL, ctx, lat, True)
        at(i == NL + 1, NL + 1, None, ctx, True)
        at(i == NL + 2, NL + 2, None, None, True)
    else:
        at(i == NL, NL, None, lat, True)
        at(i == NL + 1, NL + 1, None, None, True)


def _gqa_kernel(q_ref, k_ref, vt_ref, o_ref, s0, s1, m0, m1, a0, a1, *, NL, NT, ntq):
    group = A_HEADS // A_KV_HEADS
    load_q = lambda: q_ref[...].reshape(group * TILE, LANES)

    def finish(acc_t):
        o_t = acc_t * (1.0 / acc_t[HEAD_DIM:HEAD_DIM + 1, :])
        o_ref[...] = o_t.T.reshape(group, TILE, LANES).astype(BF16)

    _attend(pl.program_id(2), NL, NT, ntq, load_q, k_ref, vt_ref, (s0, s1), (m0, m1), (a0, a1), finish)


def _gqa_call(qa, ka, va, dims, ntq):
    B, NT, NL = dims
    T = qa.shape[1]
    S = NT * TILE
    group = A_HEADS // A_KV_HEADS
    qmap = lambda b, g, i: (g, b * NT + jnp.minimum(i, ntq - 1), 0)
    omap = lambda b, g, i: (g, b * NT + jnp.maximum(i - 2, 0), 0)
    kmap = lambda b, g, i: (g, b, 0)
    vmap = lambda b, g, i: (g, b, 0, 0)
    half = group * TILE // 2
    return pl.pallas_call(
        functools.partial(_gqa_kernel, NL=NL, NT=NT, ntq=ntq),
        grid=(B, A_KV_HEADS, ntq + 2),
        in_specs=[pl.BlockSpec((group, TILE, LANES), qmap),
                  pl.BlockSpec((None, S, LANES), kmap),
                  pl.BlockSpec((None, None, LANES, S), vmap)],
        out_specs=pl.BlockSpec((group, TILE, LANES), omap),
        out_shape=jax.ShapeDtypeStruct((A_HEADS, T, LANES), BF16),
        scratch_shapes=([pltpu.VMEM((2, S, half), F32)] * 2 + [pltpu.VMEM((2, SUBLANES, half), F32)] * 2
                        + [pltpu.VMEM((LANES, 2 * half), F32)] * 2),
        compiler_params=_cparams(("arbitrary", "arbitrary", "arbitrary")),
        name="gqa_attn",
    )(qa, ka, va)


def _diff_kernel(q_ref, k_ref, vt_ref, par_ref, o_ref, q2_scr, s0, s1, m0, m1, a0, a1, *, NL, NT, ntq):
    q = q_ref[...]
    lane = _lane_iota(q.shape)
    zero = jnp.zeros_like(q)
    q2_scr[0:TILE, :] = jnp.where(lane < D_QKDIM, q, zero)
    q2_scr[TILE:2 * TILE, :] = jnp.where(lane >= D_QKDIM, q, zero)

    def finish(acc_t):
        o1 = acc_t[:, :TILE] * (1.0 / acc_t[D_VDIM:D_VDIM + 1, :TILE])
        o2 = acc_t[:, TILE:] * (1.0 / acc_t[D_VDIM:D_VDIM + 1, TILE:])
        o = jnp.where(lane < D_VDIM, (o1 - par_ref[0:1, 0:1] * o2).T, 0.0)
        msq = jnp.sum(o * o, axis=-1, keepdims=True) * (1.0 / D_VDIM)
        o_ref[...] = (o * lax.rsqrt(msq + EPS) * par_ref[1:2, :]).astype(BF16)

    _attend(pl.program_id(2), NL, NT, ntq, lambda: q2_scr[...], k_ref, vt_ref, (s0, s1), (m0, m1), (a0, a1), finish)


def _diff_call(qd, kd, vd, dpar, dims, ntq):
    B, NT, NL = dims
    T = qd.shape[1]
    S = NT * TILE
    qmap = lambda b, h, i: (h, b * NT + jnp.minimum(i, ntq - 1), 0)
    omap = lambda b, h, i: (h, b * NT + jnp.maximum(i - 2, 0), 0)
    kmap = lambda b, h, i: (h, b, 0)
    vmap = lambda b, h, i: (h, b, 0, 0)
    return pl.pallas_call(
        functools.partial(_diff_kernel, NL=NL, NT=NT, ntq=ntq),
        grid=(B, D_HEADS, ntq + 2),
        in_specs=[pl.BlockSpec((None, TILE, LANES), qmap),
                  pl.BlockSpec((None, S, LANES), kmap),
                  pl.BlockSpec((None, None, LANES, S), vmap),
                  pl.BlockSpec((8, LANES), lambda b, h, i: (0, 0))],
        out_specs=pl.BlockSpec((None, TILE, LANES), omap),
        out_shape=jax.ShapeDtypeStruct((D_HEADS, T, LANES), BF16),
        scratch_shapes=([pltpu.VMEM((2 * TILE, LANES), BF16)] + [pltpu.VMEM((2, S, TILE), F32)] * 2
                        + [pltpu.VMEM((2, SUBLANES, TILE), F32)] * 2 + [pltpu.VMEM((LANES, 2 * TILE), F32)] * 2),
        compiler_params=_cparams(("arbitrary", "arbitrary", "arbitrary")),
        name="diff_attn",
    )(qd, kd, vd, dpar)


def _conv_kernel(z_ref, w_ref, par_ref, o_ref, zp_ref, sh_ref, *, segs):
    zeros = jnp.zeros((CONV_HALO, GROUP_W), F32)
    pos = 0
    starts = []
    for (r0, n) in segs:
        zp_ref[pos:pos + CONV_HALO, :] = zeros
        zp_ref[pos + CONV_HALO:pos + CONV_HALO + n, :] = z_ref[r0:r0 + n, :]
        starts.append(pos + CONV_HALO)
        pos += CONV_HALO + n
    zp_ref[pos:pos + CONV_HALO, :] = zeros
    bias, ln_g, ln_b = par_ref[0:1, :], par_ref[1:2, :], par_ref[2:3, :]

    for (r0, n), p0 in zip(segs, starts):
        def chunk(j, carry, r0=r0, p0=p0):
            base = pl.multiple_of(j * CONV_ROWS, CONV_ROWS)
            win = zp_ref[pl.ds(base + (p0 - CONV_HALO), CONV_ROWS + 2 * CONV_HALO), :]
            acc = jnp.zeros((CONV_ROWS, GROUP_W), F32)
            first = CONV_HALO - CONV_W // 2
            for res in range(SUBLANES):
                taps = [k for k in range(CONV_W) if (first + k) % SUBLANES == res]
                if not taps:
                    continue
                span = (first + taps[-1]) - res + CONV_ROWS
                sh_ref[res, 0:span, :] = win[res:res + span, :]
                for k in taps:
                    a = first + k - res
                    acc = acc + sh_ref[res, a:a + CONV_ROWS, :] * w_ref[k:k + 1, :]
            u = acc + bias
            mu = jnp.mean(u, axis=-1, keepdims=True)
            uc = u - mu
            var = jnp.mean(uc * uc, axis=-1, keepdims=True)
            y = uc * lax.rsqrt(var + EPS) * ln_g + ln_b
            o_ref[pl.ds(pl.multiple_of(base + r0, CONV_ROWS), CONV_ROWS), :] = (
                y * (1.0 / (1.0 + jnp.exp(-y)))).astype(BF16)
            return carry

        lax.fori_loop(0, n // CONV_ROWS, chunk, 0)


def _conv_call(z, conv_w, conv_par, dims, with_ctx):
    B, NT, NL = dims
    T = z.shape[0]
    S = NT * TILE
    L = NL * TILE
    segs = ((0, L), (L, S - L)) if with_ctx else ((0, L),)
    pad_rows = sum(n for _, n in segs) + CONV_HALO * (len(segs) + 1)
    return pl.pallas_call(
        functools.partial(_conv_kernel, segs=segs),
        grid=(B,),
        in_specs=[pl.BlockSpec((S, GROUP_W), lambda b: (b, 0)),
                  pl.BlockSpec((32, GROUP_W), lambda b: (0, 0)),
                  pl.BlockSpec((8, GROUP_W), lambda b: (0, 0))],
        out_specs=pl.BlockSpec((S, GROUP_W), lambda b: (b, 0)),
        out_shape=jax.ShapeDtypeStruct((T, GROUP_W), BF16),
        scratch_shapes=[pltpu.VMEM((pad_rows, GROUP_W), F32),
                        pltpu.VMEM((SUBLANES, CONV_ROWS + 2 * CONV_HALO, GROUP_W), F32)],
        compiler_params=_cparams(("arbitrary",)),
        name="conformer_conv",
    )(z, conv_w, conv_par)


def _dft_kernel(c_ref, s_ref, x_ref, cb_ref, sb_ref, o_ref, *, r0, n):
    xs = x_ref[r0:r0 + n, :]
    u = jnp.dot(c_ref[...], xs, preferred_element_type=F32).astype(BF16)
    w = jnp.dot(s_ref[...], xs, preferred_element_type=F32).astype(BF16)
    y = (jnp.dot(u, cb_ref[...], preferred_element_type=F32)
         - jnp.dot(w, sb_ref[...], preferred_element_type=F32))
    o_ref[...] = y.astype(BF16)


def _dft_call(cu, cmat, smat, cbd, sbd, dims, r0, n, prev=None):
    B, NT, NL = dims
    T = cu.shape[0]
    S = NT * TILE
    nr = n // TILE
    t0 = r0 // TILE
    args = [cmat, smat, cu, cbd, sbd]
    in_specs = [pl.BlockSpec((TILE, n), lambda r, b: (r, 0)),
                pl.BlockSpec((TILE, n), lambda r, b: (r, 0)),
                pl.BlockSpec((S, GROUP_W), lambda r, b: (b, 0)),
                pl.BlockSpec((GROUP_W, GROUP_W), lambda r, b: (0, 0)),
                pl.BlockSpec((GROUP_W, GROUP_W), lambda r, b: (0, 0))]
    kern = functools.partial(_dft_kernel, r0=r0, n=n)
    aliases = {}
    if prev is not None:
        args.append(prev)
        in_specs.append(pl.BlockSpec(memory_space=pl.ANY))
        aliases = {5: 0}
        kern = lambda c, s, x, cb, sb, _prev, o: _dft_kernel(c, s, x, cb, sb, o, r0=r0, n=n)
    return pl.pallas_call(
        kern,
        grid=(nr, B),
        in_specs=in_specs,
        out_specs=pl.BlockSpec((TILE, GROUP_W), lambda r, b: (b * NT + t0 + r, 0)),
        out_shape=jax.ShapeDtypeStruct((T, GROUP_W), BF16),
        input_output_aliases=aliases,
        compiler_params=_cparams(("arbitrary", "arbitrary")),
        name="fourier_mix",
    )(*args)


def _out_kernel(x_ref, oa_ref, ob_ref, oc_ref, od_ref, w_ref, g1_ref, a2_ref, sh2_ref, rw_ref, rb_ref,
                xn_ref, h2_ref, ti_ref, gt_ref, cnt_ref, slab_ref):
    ocat = jnp.concatenate([oa_ref[hh] for hh in range(A_HEADS)] + [ob_ref[...], oc_ref[...]]
                           + [od_ref[hh] for hh in range(D_HEADS)], axis=1)
    r = jnp.dot(ocat, w_ref[...], preferred_element_type=F32)
    xn = x_ref[...] + g1_ref[...] * r
    xn_ref[...] = xn
    ms = jnp.mean(xn * xn, axis=-1, keepdims=True)
    h2 = xn * lax.rsqrt(ms + EPS) * a2_ref[...] + sh2_ref[...]
    _rows_to_tiles(h2, h2_ref, slab_ref)
    logits = jnp.dot(h2.astype(BF16), rw_ref[...], preferred_element_type=F32) + rb_ref[...]
    lane = _lane_iota(logits.shape)
    lanef = lane.astype(F32)
    vals, idxs = [], []
    cur = logits
    for _ in range(TOP_K):
        mx = jnp.max(cur, axis=-1, keepdims=True)
        idx = jnp.min(jnp.where(cur == mx, lanef, float(LANES)), axis=-1, keepdims=True)
        vals.append(mx)
        idxs.append(idx)
        cur = jnp.where(lanef == idx, NEG_BIG * 2, cur)
    exps = [jnp.exp(v - vals[0]) for v in vals]
    inv = 1.0 / (exps[0] + exps[1] + exps[2] + exps[3])
    ti = jnp.zeros(logits.shape, F32)
    gt = jnp.zeros(logits.shape, F32)
    onehot = jnp.zeros(logits.shape, F32)
    for k in range(TOP_K):
        ti = jnp.where(lane == k, idxs[k], ti)
        gt = jnp.where(lane == k, exps[k] * inv, gt)
        onehot = onehot + jnp.where(lanef == idxs[k], 1.0, 0.0)
    ti_ref[...] = ti.astype(I32)
    gt_ref[...] = gt

    @pl.when((pl.program_id(0) == 0) & (pl.program_id(1) == 0))
    def _():
        cnt_ref[...] = jnp.zeros(cnt_ref.shape, F32)

    cnt_ref[0:1, :] += jnp.sum(onehot, axis=0, keepdims=True)


def _out_call(x_all, oa, ob, oc, od, w_out_p, g1, a2, sh2, rw, rb, dims, ntq):
    B, NT, NL = dims
    n_tok = B * ntq * TILE
    sel = lambda b, i: (jnp.where(i >= NL, B, b), 0, 0)
    row = lambda b, i: (b * NT + i, 0)
    hrow = lambda b, i: (0, b * NT + i, 0)
    crow = lambda b, i: (b * ntq + i, 0)
    const = lambda b, i: (0, 0)
    return pl.pallas_call(
        _out_kernel,
        grid=(B, ntq),
        in_specs=[pl.BlockSpec((TILE, D_MODEL), row),
                  pl.BlockSpec((A_HEADS, TILE, LANES), hrow),
                  pl.BlockSpec((TILE, GROUP_W), row),
                  pl.BlockSpec((TILE, GROUP_W), row),
                  pl.BlockSpec((D_HEADS, TILE, LANES), hrow),
                  pl.BlockSpec((OUT_K, D_MODEL), const),
                  pl.BlockSpec((None, 1, D_MODEL), sel),
                  pl.BlockSpec((None, 1, D_MODEL), sel),
                  pl.BlockSpec((None, 1, D_MODEL), sel),
                  pl.BlockSpec((D_MODEL, LANES), const),
                  pl.BlockSpec((1, LANES), const)],
        out_specs=[pl.BlockSpec((TILE, D_MODEL), crow),
                   pl.BlockSpec((TILE * ROW_SUB, LANES), crow),
                   pl.BlockSpec((TILE, LANES), crow),
                   pl.BlockSpec((TILE, LANES), crow),
                   pl.BlockSpec((8, LANES), const)],
        out_shape=[jax.ShapeDtypeStruct((n_tok, D_MODEL), F32),
                   jax.ShapeDtypeStruct((n_tok * ROW_SUB, LANES), F32),
                   jax.ShapeDtypeStruct((n_tok, LANES), I32),
                   jax.ShapeDtypeStruct((n_tok, LANES), F32),
                   jax.ShapeDtypeStruct((8, LANES), F32)],
        scratch_shapes=[_slab_scratch()],
        compiler_params=_cparams(("arbitrary", "arbitrary")),
        name="out_proj_router",
    )(x_all, oa, ob, oc, od, w_out_p, g1, a2, sh2, rw, rb)


def _rank_kernel(ti_ref, ps_ref, tri_ref, dest_ref, carry_ref):
    @pl.when(pl.program_id(0) == 0)
    def _():
        carry_ref[...] = jnp.zeros(carry_ref.shape, F32)

    ti = ti_ref[...]
    lane = _lane_iota(ti.shape)
    hits = [lane == ti[:, k:k + 1] for k in range(TOP_K)]
    onehot = jnp.zeros(ti.shape, F32)
    for hk in hits:
        onehot = onehot + jnp.where(hk, 1.0, 0.0)
    before = jnp.dot(tri_ref[...], onehot.astype(BF16), preferred_element_type=F32)
    base = before + carry_ref[0:1, :] + ps_ref[...]
    dest = jnp.zeros(ti.shape, F32)
    for k, hk in enumerate(hits):
        dk = jnp.sum(jnp.where(hk, base, 0.0), axis=-1, keepdims=True)
        dest = jnp.where(lane == k, dk, dest)
    dest_ref[...] = dest.astype(I32)
    carry_ref[0:1, :] += jnp.sum(onehot, axis=0, keepdims=True)


def _rank_call(topi, pad_start, tri):
    n_tok = topi.shape[0]
    return pl.pallas_call(
        _rank_kernel,
        grid=(n_tok // TILE,),
        in_specs=[pl.BlockSpec((TILE, LANES), lambda t: (t, 0)),
                  pl.BlockSpec((1, LANES), lambda t: (0, 0)),
                  pl.BlockSpec((TILE, TILE), lambda t: (0, 0))],
        out_specs=pl.BlockSpec((TILE, LANES), lambda t: (t, 0)),
        out_shape=jax.ShapeDtypeStruct((n_tok, LANES), I32),
        scratch_shapes=[pltpu.VMEM((8, LANES), F32)],
        compiler_params=_cparams(("arbitrary",)),
        name="slot_rank",
    )(topi, pad_start, tri)


def _tile_copy(src_ref, src_row, dst_ref, dst_row, sem):
    src = src_ref.at[pl.ds(pl.multiple_of(src_row * ROW_SUB, ROW_SUB), ROW_SUB)]
    dst = dst_ref.at[pl.ds(pl.multiple_of(dst_row * ROW_SUB, ROW_SUB), ROW_SUB)]
    return pltpu.make_async_copy(src, dst, sem)


def _dispatch_kernel(fill_start_ref, fill_on_ref, dest_ref, h2_ref, zero_ref, xs_ref, sem):
    @pl.when(pl.program_id(0) == 0)
    def _():
        def fill(e, wait):
            @pl.when(fill_on_ref[e] > 0)
            def _():
                start = pl.multiple_of(fill_start_ref[e] * ROW_SUB, MOE_BLOCK * ROW_SUB)
                cp = pltpu.make_async_copy(zero_ref, xs_ref.at[pl.ds(start, MOE_BLOCK * ROW_SUB)], sem.at[1])
                if wait:
                    cp.wait()
                else:
                    cp.start()

        lax.fori_loop(0, N_EXPERTS, lambda e, c: (fill(e, False), c)[1], 0)
        lax.fori_loop(0, N_EXPERTS, lambda e, c: (fill(e, True), c)[1], 0)

    def issue(r, c):
        for k in range(TOP_K):
            _tile_copy(h2_ref, r, xs_ref, dest_ref[0, 0, TOP_K * r + k], sem.at[0]).start(priority=k % 2)
        return c

    lax.fori_loop(0, TILE, issue, 0, unroll=DMA_UNROLL)
    for k in range(TOP_K):
        pltpu.make_async_copy(h2_ref, xs_ref.at[pl.ds(0, TILE * ROW_SUB)], sem.at[0]).wait()


def _dispatch_call(fill_start, fill_on, dest3, h2t, n_slots):
    n_tok = h2t.shape[0] // ROW_SUB
    zero_blk = jnp.zeros((MOE_BLOCK * ROW_SUB, LANES), F32)
    return pl.pallas_call(
        _dispatch_kernel,
        grid_spec=pltpu.PrefetchScalarGridSpec(
            num_scalar_prefetch=2,
            grid=(n_tok // TILE,),
            in_specs=[pl.BlockSpec((1, 1, TOP_K * TILE), lambda t, fs, fo: (t, 0, 0), memory_space=pltpu.SMEM),
                      pl.BlockSpec((TILE * ROW_SUB, LANES), lambda t, fs, fo: (t, 0)),
                      pl.BlockSpec((MOE_BLOCK * ROW_SUB, LANES), lambda t, fs, fo: (0, 0))],
            out_specs=pl.BlockSpec(memory_space=pl.ANY),
            scratch_shapes=[pltpu.SemaphoreType.DMA((2,))]),
        out_shape=jax.ShapeDtypeStruct((n_slots * ROW_SUB, LANES), F32),
        compiler_params=_cparams(("arbitrary",)),
        name="moe_dispatch",
    )(fill_start, fill_on, dest3, h2t, zero_blk)


def _expert_kernel(blk_e_ref, nact_ref, xs_ref, wgu_ref, bgu_ref, wd_ref, bd_ref, yb_ref, slab_ref):
    @pl.when(pl.program_id(0) < nact_ref[0])
    def _():
        x = _tiles_to_rows(xs_ref, slab_ref).astype(BF16)
        gu = jnp.dot(x, wgu_ref[...], preferred_element_type=F32) + bgu_ref[...]
        g = jnp.minimum(gu[:, :D_FF], SWIGLU_LIMIT)
        u = jnp.clip(gu[:, D_FF:], -SWIGLU_LIMIT, SWIGLU_LIMIT)
        act = g * (1.0 / (1.0 + jnp.exp(-SWIGLU_ALPHA * g))) * (u + 1.0)
        y = jnp.dot(act.astype(BF16), wd_ref[...], preferred_element_type=F32) + bd_ref[...]
        _rows_to_tiles(y, yb_ref, slab_ref)


def _expert_call(blk_e, n_active, xs, w_gu, b_gu, w_down, b_down):
    n_slots = xs.shape[0] // ROW_SUB
    n_blocks = n_slots // MOE_BLOCK
    blk = lambda j, be, na: (jnp.minimum(j, na[0] - 1), 0)
    wsel = lambda j, be, na: (be[jnp.minimum(j, na[0] - 1)], 0, 0)
    return pl.pallas_call(
        _expert_kernel,
        grid_spec=pltpu.PrefetchScalarGridSpec(
            num_scalar_prefetch=2,
            grid=(n_blocks,),
            in_specs=[pl.BlockSpec((MOE_BLOCK * ROW_SUB, LANES), blk),
                      pl.BlockSpec((None, D_MODEL, 2 * D_FF), wsel),
                      pl.BlockSpec((None, 1, 2 * D_FF), wsel),
                      pl.BlockSpec((None, D_FF, D_MODEL), wsel),
                      pl.BlockSpec((None, 1, D_MODEL), wsel)],
            out_specs=pl.BlockSpec((MOE_BLOCK * ROW_SUB, LANES), blk),
            scratch_shapes=[_slab_scratch()]),
        out_shape=jax.ShapeDtypeStruct((n_slots * ROW_SUB, LANES), F32),
        compiler_params=_cparams(("arbitrary",)),
        name="moe_experts",
    )(blk_e, n_active, xs, w_gu, b_gu, w_down, b_down)


def _combine_kernel(dest_ref, x_ref, gt_ref, g2_ref, yb_ref, o_ref, buf, slab_ref, sem):
    def issue(r, c):
        for k in range(TOP_K):
            _tile_copy(yb_ref, dest_ref[0, 0, TOP_K * r + k], buf.at[k], r, sem.at[0]).start(priority=k % 2)
        return c

    lax.fori_loop(0, TILE, issue, 0, unroll=DMA_UNROLL)
    for k in range(TOP_K):
        pltpu.make_async_copy(yb_ref.at[pl.ds(0, TILE * ROW_SUB)], buf.at[k], sem.at[0]).wait()
    gt = gt_ref[...]
    y = gt[:, 0:1] * _tiles_to_rows(buf.at[0], slab_ref)
    for k in range(1, TOP_K):
        y = y + gt[:, k:k + 1] * _tiles_to_rows(buf.at[k], slab_ref)
    o_ref[...] = x_ref[...] + g2_ref[...] * y


def _combine_call(dest3, xn, gates, g2, yb, dims, ntq):
    B, NT, NL = dims
    n_tok = xn.shape[0]
    sel = lambda b, i: (jnp.where(i >= NL, B, b), 0, 0)
    crow = lambda b, i: (b * ntq + i, 0)
    return pl.pallas_call(
        _combine_kernel,
        grid=(B, ntq),
        in_specs=[pl.BlockSpec((1, 1, TOP_K * TILE), lambda b, i: (b * ntq + i, 0, 0), memory_space=pltpu.SMEM),
                  pl.BlockSpec((TILE, D_MODEL), crow),
                  pl.BlockSpec((TILE, LANES), crow),
                  pl.BlockSpec((None, 1, D_MODEL), sel),
                  pl.BlockSpec(memory_space=pl.ANY)],
        out_specs=pl.BlockSpec((TILE, D_MODEL), crow),
        out_shape=jax.ShapeDtypeStruct((n_tok, D_MODEL), F32),
        scratch_shapes=[pltpu.VMEM((TOP_K, TILE * ROW_SUB, LANES), F32), _slab_scratch(),
                        pltpu.SemaphoreType.DMA((1,))],
        compiler_params=_cparams(("arbitrary", "arbitrary")),
        name="moe_combine",
    )(dest3, xn, gates, g2, yb)


def _pad_heads_cols(w, n_heads, width):
    w = w.reshape(w.shape[0], n_heads, width)
    return jnp.pad(w, ((0, 0), (0, 0), (0, LANES - width))).reshape(w.shape[0], n_heads * LANES)


def _pad_heads_rows(w, n_heads, width):
    w = w.reshape(n_heads, width, w.shape[1])
    return jnp.pad(w, ((0, 0), (0, LANES - width), (0, 0))).reshape(n_heads * LANES, w.shape[2])


def _prep_w_in(w):
    splits = (256, 128, 128, 256, 256, 256, 256, 256, 256)
    parts, o = [], 0
    for s in splits:
        parts.append(w[:, o:o + s])
        o += s
    aq, ak, av, bu, bg, cu, dq, dk, dv = parts
    return jnp.concatenate([
        _pad_heads_cols(aq, A_HEADS, HEAD_DIM), _pad_heads_cols(ak, A_KV_HEADS, HEAD_DIM),
        _pad_heads_cols(av, A_KV_HEADS, HEAD_DIM), bu, bg, cu,
        _pad_heads_cols(dq, D_HEADS, 2 * D_QKDIM), _pad_heads_cols(dk, D_HEADS, 2 * D_QKDIM),
        _pad_heads_cols(dv, D_HEADS, D_VDIM)], axis=1).astype(BF16)


def _prep_w_out(w):
    return jnp.concatenate([
        _pad_heads_rows(w[0:GROUP_W], A_HEADS, HEAD_DIM), w[GROUP_W:3 * GROUP_W],
        _pad_heads_rows(w[3 * GROUP_W:], D_HEADS, D_VDIM)], axis=0).astype(BF16)


def _pad_lanes(v):
    return jnp.pad(v.astype(F32), (0, LANES - v.shape[0]))


def _rope_tables(L, C):
    t = jnp.arange(L, dtype=jnp.int32)
    rows = (t // GRID_W).astype(F32)[:, None]
    cols = (t % GRID_W).astype(F32)[:, None]
    lane = jnp.arange(LANES)

    def table(group):
        nfreq = group // 2
        j = lane % nfreq
        inv = jnp.power(ROPE_THETA, -(2.0 * j.astype(F32)) / group)[None, :]
        use_cols = ((lane // group) % 2) == 1
        ang = jnp.where(use_cols[None, :], cols, rows) * inv
        sign = jnp.where((lane % group) < nfreq, -1.0, 1.0)[None, :]
        live = (lane < 2 * D_QKDIM)[None, :]
        cos = jnp.where(live, jnp.cos(ang), 1.0)
        sin = jnp.where(live, jnp.sin(ang) * sign, 0.0)
        cos = jnp.concatenate([cos, jnp.ones((C, LANES), F32)], axis=0)
        sin = jnp.concatenate([sin, jnp.zeros((C, LANES), F32)], axis=0)
        return cos, sin

    ca, sa = table(HEAD_DIM // 2)
    cd, sd = table(D_QKDIM // 2)
    return ca, sa, cd, sd


def _dft_mats(n):
    r = 1 << (int(math.log2(n)) // 2)
    assert n % r == 0
    k = jnp.arange(n, dtype=jnp.int32)[None, :]

    def table(rows, period):
        idx = jnp.arange(rows, dtype=jnp.int32)[:, None]
        ang = ((idx * k) % period).astype(F32) * (2.0 * math.pi / period)
        return jnp.cos(ang), jnp.sin(ang)

    c1, s1 = table(n // r, n // r)
    c2, s2 = table(r, n)
    c1, s1, c2, s2 = c1[:, None, :], s1[:, None, :], c2[None, :, :], s2[None, :, :]
    scale = 1.0 / math.sqrt(n)
    cos = ((c1 * c2 - s1 * s2) * scale).reshape(n, n).astype(BF16)
    sin = ((s1 * c2 + c1 * s2) * scale).reshape(n, n).astype(BF16)
    return cos, sin


def _channel_dft_mats():
    w = GROUP_W // FNET_GROUPS
    k = jnp.arange(GROUP_W, dtype=jnp.int32)
    same = (k[:, None] // w) == (k[None, :] // w)
    prod = ((k[:, None] % w) * (k[None, :] % w)) % w
    ang = prod.astype(F32) * (2.0 * math.pi / w)
    scale = 1.0 / math.sqrt(w)
    return (jnp.where(same, jnp.cos(ang) * scale, 0.0).astype(BF16),
            jnp.where(same, jnp.sin(ang) * scale, 0.0).astype(BF16))


def _slot_plan(counts, n_blocks):
    cnt = counts[0, :N_EXPERTS].astype(I32)
    padded = (cnt + MOE_BLOCK - 1) // MOE_BLOCK * MOE_BLOCK
    pad_ends = jnp.cumsum(padded)
    pad_starts = pad_ends - padded
    n_active = (pad_ends[-1] // MOE_BLOCK).reshape(1).astype(I32)
    blk_start = jnp.arange(n_blocks, dtype=I32) * MOE_BLOCK
    blk_e = jnp.minimum(jnp.sum(pad_ends[None, :] <= blk_start[:, None], axis=1), N_EXPERTS - 1).astype(I32)
    fill_start = jnp.maximum(pad_ends - MOE_BLOCK, 0).astype(I32)
    fill_on = (cnt > 0).astype(I32)
    ps_vec = jnp.pad(pad_starts.astype(F32), (0, LANES - N_EXPERTS)).reshape(1, LANES)
    return ps_vec, blk_e, n_active, fill_start, fill_on


def kernel(x, c, ctx, c_ctx, norm1_g, norm2_g, ada_w, ada_b, w_in, a_qnorm_g, a_knorm_g, conv_dw_w, conv_dw_b,
           conv_ln_g, conv_ln_b, d_qnorm_g, d_knorm_g, d_lambda, d_subln_g, w_out, router_w, router_b,
           exp_w_gu, exp_b_gu, exp_w_down, exp_b_down):
    B, L, D = x.shape
    C = ctx.shape[1]
    depth = ada_w.shape[0]
    assert D == D_MODEL and L % TILE == 0 and C == TILE and L % GRID_W == 0
    NL, NT = L // TILE, (L + C) // TILE
    dims = (B, NT, NL)
    S = NT * TILE

    x_all = jnp.concatenate([x, ctx], axis=1).reshape(B * S, D)

    a = jnp.concatenate([c, c_ctx[None, :]], axis=0)
    a = jnp.pad(a * jax.nn.sigmoid(a), ((0, 16 - (B + 1) % 16 if (B + 1) % 16 else 0), (0, 0)))
    mod_all = _ada_call(a, ada_w, ada_b)

    rope = _rope_tables(L, C)
    cmat, smat = _dft_mats(L)
    cmat_c, smat_c = _dft_mats(C)
    cbd, sbd = _channel_dft_mats()
    tri = jnp.tril(jnp.ones((TILE, TILE), F32), -1).astype(BF16)

    x_lat = None
    for li in range(depth):
        last = li == depth - 1
        ntq = NL if last else NT
        mod = mod_all[li, :B + 1].reshape(B + 1, 6, 1, D)
        sh1, sc1, g1, sh2, sc2, g2 = (mod[:, j] for j in range(6))
        a1 = norm1_g[li][None, None, :] * (1.0 + sc1)
        a2 = norm2_g[li][None, None, :] * (1.0 + sc2)

        gains = jnp.stack([
            _pad_lanes(a_qnorm_g[li]) * (HEAD_DIM ** -0.5 * LOG2E), _pad_lanes(a_knorm_g[li]),
            _pad_lanes(jnp.tile(d_qnorm_g[li], 2)) * (D_QKDIM ** -0.5 * LOG2E), _pad_lanes(jnp.tile(d_knorm_g[li], 2)),
        ] + [jnp.zeros((LANES,), F32)] * 4)
        qa, ka, va, z, cu, qd, kd, vd = _proj_call(x_all, a1, sh1, _prep_w_in(w_in[li]), gains, rope, dims)

        lam_init = 0.8 - 0.6 * math.exp(-0.3 * li)
        lq1, lk1, lq2, lk2 = d_lambda[li].astype(F32)
        lam = jnp.exp(jnp.sum(lq1 * lk1)) - jnp.exp(jnp.sum(lq2 * lk2)) + lam_init
        dpar = jnp.stack([jnp.full((LANES,), lam, F32), _pad_lanes(d_subln_g[li]) * (1.0 - lam_init)]
                         + [jnp.zeros((LANES,), F32)] * 6)

        oa = _gqa_call(qa, ka, va, dims, ntq)
        od = _diff_call(qd, kd, vd, dpar, dims, ntq)
        conv_w = jnp.pad(conv_dw_w[li], ((0, 32 - CONV_W), (0, 0)))
        conv_par = jnp.stack([conv_dw_b[li], conv_ln_g[li], conv_ln_b[li]] + [jnp.zeros((GROUP_W,), F32)] * 5)
        ob = _conv_call(z, conv_w, conv_par, dims, with_ctx=not last)
        oc = _dft_call(cu, cmat, smat, cbd, sbd, dims, 0, L)
        if not last:
            oc = _dft_call(cu, cmat_c, smat_c, cbd, sbd, dims, L, C, prev=oc)

        rw = jnp.pad(router_w[li], ((0, 0), (0, LANES - N_EXPERTS))).astype(BF16)
        rb = jnp.concatenate([router_b[li].astype(F32), jnp.full((LANES - N_EXPERTS,), NEG_BIG, F32)]).reshape(1, LANES)
        xn, h2t, topi, gates, counts = _out_call(x_all, oa, ob, oc, od, _prep_w_out(w_out[li]), g1, a2, sh2,
                                                 rw, rb, dims, ntq)

        n_tok = B * ntq * TILE
        n_blocks = -(-(n_tok * TOP_K + N_EXPERTS * (MOE_BLOCK - 1)) // MOE_BLOCK)
        ps_vec, blk_e, n_active, fill_start, fill_on = _slot_plan(counts, n_blocks)
        dest = _rank_call(topi, ps_vec, tri)
        dest3 = dest[:, :TOP_K].reshape(n_tok // TILE, 1, TOP_K * TILE)
        xs = _dispatch_call(fill_start, fill_on, dest3, h2t, n_blocks * MOE_BLOCK)
        yb = _expert_call(blk_e, n_active, xs, exp_w_gu[li].astype(BF16), exp_b_gu[li][:, None, :],
                          exp_w_down[li].astype(BF16), exp_b_down[li][:, None, :])
        x_next = _combine_call(dest3, xn, gates, g2, yb, dims, ntq)
        if last:
            x_lat = x_next
        else:
            x_all = x_next
    return x_lat.reshape(B, L, D)
```

```python
import functools
import math

import jax
import jax.numpy as jnp
from jax import lax
from jax.experimental import pallas as pl
from jax.experimental.pallas import tpu as pltpu

F32 = jnp.float32
BF16 = jnp.bfloat16
I32 = jnp.int32

D_MODEL = 1024
TILE = 256
LANES = 128
GRID_W = 64
HEAD_DIM = 64
A_HEADS = 4
A_KV_HEADS = 2
CONV_W = 31
CONV_HALO = 16
CONV_ROWS = 128
FNET_GROUPS = 4
D_HEADS = 4
D_QKDIM = 32
D_VDIM = 64
GROUP_W = 256
N_EXPERTS = 32
TOP_K = 4
D_FF = D_MODEL
SWIGLU_LIMIT = 7.0
SWIGLU_ALPHA = 1.702
MOE_BLOCK = 256
KEY_CHUNK = 256
DMA_UNROLL = 8
ROPE_THETA = 10000.0
EPS = 1e-6
NEG_BIG = -1e30
LOG2E = math.log2(math.e)
SUBLANES = 8
ROW_SUB = D_MODEL // LANES
assert ROW_SUB == SUBLANES
ROW_TILE = TILE
ROW_PITCH = ROW_TILE + SUBLANES

PROJ_W = (A_HEADS * LANES, A_KV_HEADS * LANES, A_KV_HEADS * LANES, GROUP_W, GROUP_W, GROUP_W,
          D_HEADS * LANES, D_HEADS * LANES, D_HEADS * LANES)
PROJ_OFF = tuple(sum(PROJ_W[:i]) for i in range(len(PROJ_W) + 1))
OUT_K = A_HEADS * LANES + 2 * GROUP_W + D_HEADS * LANES

VMEM_LIMIT = 52 * 1024 * 1024


def _cparams(sem):
    return pltpu.CompilerParams(dimension_semantics=sem, vmem_limit_bytes=VMEM_LIMIT)


def _lane_iota(shape):
    return lax.broadcasted_iota(I32, shape, len(shape) - 1)


def _sigmoid(x):
    return 0.5 * jnp.tanh(0.5 * x) + 0.5


def _rows_to_tiles(h, out_ref, slab_ref):
    for j in range(ROW_SUB):
        slab_ref[j * ROW_PITCH:j * ROW_PITCH + ROW_TILE, :] = h[:, j * LANES:(j + 1) * LANES]
    for t in range(ROW_TILE):
        out_ref[ROW_SUB * t:ROW_SUB * (t + 1), :] = slab_ref[pl.ds(t, ROW_SUB, stride=ROW_PITCH), :]


def _tiles_to_rows(in_ref, slab_ref):
    for t in range(ROW_TILE):
        slab_ref[pl.ds(t, ROW_SUB, stride=ROW_PITCH), :] = in_ref[ROW_SUB * t:ROW_SUB * (t + 1), :]
    return jnp.concatenate([slab_ref[j * ROW_PITCH:j * ROW_PITCH + ROW_TILE, :] for j in range(ROW_SUB)], axis=1)


def _slab_scratch():
    return pltpu.VMEM((ROW_SUB * ROW_PITCH, LANES), F32)


def _ada_kernel(a_ref, w_ref, b_ref, o_ref):
    o_ref[...] = jnp.dot(a_ref[...], w_ref[...], preferred_element_type=F32,
                         precision=lax.Precision.HIGHEST) + b_ref[...]


def _ada_call(a, ada_w, ada_b):
    depth, d, n = ada_w.shape
    rows = a.shape[0]
    nb = n // d
    return pl.pallas_call(
        _ada_kernel,
        grid=(depth, nb),
        in_specs=[pl.BlockSpec((rows, d), lambda l, j: (0, 0)),
                  pl.BlockSpec((None, d, d), lambda l, j: (l, 0, j)),
                  pl.BlockSpec((None, 1, d), lambda l, j: (l, 0, j))],
        out_specs=pl.BlockSpec((None, rows, d), lambda l, j: (l, 0, j)),
        out_shape=jax.ShapeDtypeStruct((depth, rows, n), F32),
        compiler_params=_cparams(("arbitrary", "arbitrary")),
        name="ada_mod",
    )(a, ada_w, ada_b.reshape(depth, 1, n))


def _swap_halves(x, half):
    lane = _lane_iota(x.shape)
    first = (lane & (2 * half - 1)) < half
    return jnp.where(first, pltpu.roll(x, LANES - half, 1), pltpu.roll(x, half, 1))


def _proj_kernel(x_ref, a_ref, sh_ref, w_ref, gains_ref, ca_ref, sa_ref, cd_ref, sd_ref,
                 qa_ref, ka_ref, va_ref, z_ref, cu_ref, qd_ref, kd_ref, vd_ref):
    x = x_ref[...]
    ms = jnp.mean(x * x, axis=-1, keepdims=True)
    h = (x * lax.rsqrt(ms + EPS) * a_ref[...] + sh_ref[...]).astype(BF16)
    p = jnp.dot(h, w_ref[...], preferred_element_type=F32)
    lane = _lane_iota((TILE, LANES))
    ca, sa, cd, sd = ca_ref[...], sa_ref[...], cd_ref[...], sd_ref[...]

    def slab(sec, i):
        o = PROJ_OFF[sec] + i * LANES
        return p[:, o:o + LANES]

    def norm_rope_a(s, gain):
        msq = jnp.sum(s * s, axis=-1, keepdims=True) * (1.0 / HEAD_DIM)
        sn = s * lax.rsqrt(msq + EPS) * gain
        return (sn * ca + _swap_halves(sn, HEAD_DIM // 4) * sa).astype(BF16)

    def norm_rope_d(s, gain):
        s2 = s * s
        m1 = jnp.sum(jnp.where(lane < D_QKDIM, s2, 0.0), axis=-1, keepdims=True)
        m2 = jnp.sum(jnp.where(lane >= D_QKDIM, s2, 0.0), axis=-1, keepdims=True)
        msq = jnp.where(lane < D_QKDIM, m1, m2) * (1.0 / D_QKDIM)
        sn = s * lax.rsqrt(msq + EPS) * gain
        return (sn * cd + _swap_halves(sn, D_QKDIM // 4) * sd).astype(BF16)

    def with_ones(s, width):
        st = s.T
        chan = lax.broadcasted_iota(I32, st.shape, 0)
        return jnp.where(chan == width, 1.0, st).astype(BF16)

    for hh in range(A_HEADS):
        qa_ref[hh] = norm_rope_a(slab(0, hh), gains_ref[0:1, :])
    for hh in range(A_KV_HEADS):
        ka_ref[hh] = norm_rope_a(slab(1, hh), gains_ref[1:2, :])
        va_ref[hh] = with_ones(slab(2, hh), HEAD_DIM)
    bu = p[:, PROJ_OFF[3]:PROJ_OFF[4]]
    bg = p[:, PROJ_OFF[4]:PROJ_OFF[5]]
    z_ref[...] = bu * _sigmoid(bg)
    cu_ref[...] = p[:, PROJ_OFF[5]:PROJ_OFF[6]].astype(BF16)
    for hh in range(D_HEADS):
        qd_ref[hh] = norm_rope_d(slab(6, hh), gains_ref[2:3, :])
        kd_ref[hh] = norm_rope_d(slab(7, hh), gains_ref[3:4, :])
        vd_ref[hh] = with_ones(slab(8, hh), D_VDIM)


def _proj_call(x_all, a1, sh1, w_in_p, gains, rope, dims):
    B, NT, NL = dims
    T = x_all.shape[0]
    S = NT * TILE
    sel = lambda b, i: (jnp.where(i >= NL, B, b), 0, 0)
    row = lambda b, i: (b * NT + i, 0)
    hrow = lambda b, i: (0, b * NT + i, 0)
    tab = lambda b, i: (i, 0)
    head = lambda n: pl.BlockSpec((n, TILE, LANES), hrow)
    hshape = lambda n: jax.ShapeDtypeStruct((n, T, LANES), BF16)
    head_t = lambda n: pl.BlockSpec((n, None, LANES, TILE), lambda b, i: (0, b, 0, i))
    hshape_t = lambda n: jax.ShapeDtypeStruct((n, B, LANES, S), BF16)
    nw = w_in_p.shape[1]
    return pl.pallas_call(
        _proj_kernel,
        grid=(B, NT),
        in_specs=[pl.BlockSpec((TILE, D_MODEL), row),
                  pl.BlockSpec((None, 1, D_MODEL), sel),
                  pl.BlockSpec((None, 1, D_MODEL), sel),
                  pl.BlockSpec((D_MODEL, nw), lambda b, i: (0, 0)),
                  pl.BlockSpec((8, LANES), lambda b, i: (0, 0)),
                  pl.BlockSpec((TILE, LANES), tab), pl.BlockSpec((TILE, LANES), tab),
                  pl.BlockSpec((TILE, LANES), tab), pl.BlockSpec((TILE, LANES), tab)],
        out_specs=[head(A_HEADS), head(A_KV_HEADS), head_t(A_KV_HEADS),
                   pl.BlockSpec((TILE, GROUP_W), row), pl.BlockSpec((TILE, GROUP_W), row),
                   head(D_HEADS), head(D_HEADS), head_t(D_HEADS)],
        out_shape=[hshape(A_HEADS), hshape(A_KV_HEADS), hshape_t(A_KV_HEADS),
                   jax.ShapeDtypeStruct((T, GROUP_W), F32), jax.ShapeDtypeStruct((T, GROUP_W), BF16),
                   hshape(D_HEADS), hshape(D_HEADS), hshape_t(D_HEADS)],
        compiler_params=_cparams(("arbitrary", "arbitrary")),
        name="in_proj",
    )(x_all, a1, sh1, w_in_p, gains, *rope)


def _scores_phase(q, k_ref, k0, nchunks, s_ref, m_ref):
    half = q.shape[0] // 2
    nk = nchunks * KEY_CHUNK
    for h in range(2):
        s_ref[h, 0:nk, :] = lax.dot_general(k_ref[k0:k0 + nk, :], q[h * half:(h + 1) * half],
                                            (((1,), (1,)), ((), ())), preferred_element_type=F32)
    for h in range(2):
        m8 = None
        for c in range(nchunks):
            blk = s_ref[h, c * KEY_CHUNK:(c + 1) * KEY_CHUNK, :]
            part = jnp.max(blk.reshape(KEY_CHUNK // SUBLANES, SUBLANES, half), axis=0)
            m8 = part if m8 is None else jnp.maximum(m8, part)
        m_ref[h] = m8


def _values_phase(vt_ref, k0, nchunks, s_ref, m_ref):
    half = s_ref.shape[2]
    maxes = [jnp.max(m_ref[h], axis=0, keepdims=True) for h in range(2)]
    accs = [jnp.zeros((LANES, half), F32) for _ in range(2)]
    for c in range(nchunks):
        vc = vt_ref[:, k0 + c * KEY_CHUNK:k0 + (c + 1) * KEY_CHUNK]
        for h in range(2):
            pr = jnp.exp2((s_ref[h, c * KEY_CHUNK:(c + 1) * KEY_CHUNK, :] - maxes[h]).astype(BF16))
            accs[h] = accs[h] + jnp.dot(vc, pr, preferred_element_type=F32)
    return jnp.concatenate(accs, axis=1)


def _attend(i, NL, NT, ntq, load_q, k_ref, vt_ref, s_scr, m_scr, a_scr, finish):
    lat = (0, NT * TILE // KEY_CHUNK)
    ctx = (NL * TILE, (NT - NL) * TILE // KEY_CHUNK)

    def case(step, score_keys, value_keys, store):
        w = step % 2
        if store:
            finish(a_scr[w][...])
        if score_keys is not None:
            _scores_phase(load_q(), k_ref, score_keys[0], score_keys[1], s_scr[w], m_scr[w])
        if value_keys is not None:
            a_scr[1 - w][...] = _values_phase(vt_ref, value_keys[0], value_keys[1], s_scr[1 - w], m_scr[1 - w])

    def at(cond, step, score_keys, value_keys, store):
        pl.when(cond)(lambda: case(step, score_keys, value_keys, store))

    at(i == 0, 0, lat, None, False)
    at(i == 1, 1, lat, lat, False)
    for parity in range(2):
        at((i >= 2) & (i < NL) & (lax.rem(i, 2) == parity), parity, lat, lat, True)
    if ntq > NL:
        at(i == NL, NL, ctx, lat, True)
        at(i == NL + 1, NL + 1, None, ctx, True)
        at(i == NL + 2, NL + 2, None, None, True)
    else:
        at(i == NL, NL, None, lat, True)
        at(i == NL + 1, NL + 1, None, None, True)


def _gqa_kernel(q_ref, k_ref, vt_ref, o_ref, s0, s1, m0, m1, a0, a1, *, NL, NT, ntq):
    group = A_HEADS // A_KV_HEADS
    load_q = lambda: q_ref[...].reshape(group * TILE, LANES)

    def finish(acc_t):
        o_t = acc_t * (1.0 / acc_t[HEAD_DIM:HEAD_DIM + 1, :])
        o_ref[...] = o_t.T.reshape(group, TILE, LANES).astype(BF16)

    _attend(pl.program_id(2), NL, NT, ntq, load_q, k_ref, vt_ref, (s0, s1), (m0, m1), (a0, a1), finish)


def _gqa_call(qa, ka, va, dims, ntq):
    B, NT, NL = dims
    T = qa.shape[1]
    S = NT * TILE
    group = A_HEADS // A_KV_HEADS
    qmap = lambda b, g, i: (g, b * NT + jnp.minimum(i, ntq - 1), 0)
    omap = lambda b, g, i: (g, b * NT + jnp.maximum(i - 2, 0), 0)
    kmap = lambda b, g, i: (g, b, 0)
    vmap = lambda b, g, i: (g, b, 0, 0)
    half = group * TILE // 2
    return pl.pallas_call(
        functools.partial(_gqa_kernel, NL=NL, NT=NT, ntq=ntq),
        grid=(B, A_KV_HEADS, ntq + 2),
        in_specs=[pl.BlockSpec((group, TILE, LANES), qmap),
                  pl.BlockSpec((None, S, LANES), kmap),
                  pl.BlockSpec((None, None, LANES, S), vmap)],
        out_specs=pl.BlockSpec((group, TILE, LANES), omap),
        out_shape=jax.ShapeDtypeStruct((A_HEADS, T, LANES), BF16),
        scratch_shapes=([pltpu.VMEM((2, S, half), F32)] * 2 + [pltpu.VMEM((2, SUBLANES, half), F32)] * 2
                        + [pltpu.VMEM((LANES, 2 * half), F32)] * 2),
        compiler_params=_cparams(("arbitrary", "arbitrary", "arbitrary")),
        name="gqa_attn",
    )(qa, ka, va)


def _diff_kernel(q_ref, k_ref, vt_ref, par_ref, o_ref, q2_scr, s0, s1, m0, m1, a0, a1, *, NL, NT, ntq):
    q = q_ref[...]
    lane = _lane_iota(q.shape)
    zero = jnp.zeros_like(q)
    q2_scr[0:TILE, :] = jnp.where(lane < D_QKDIM, q, zero)
    q2_scr[TILE:2 * TILE, :] = jnp.where(lane >= D_QKDIM, q, zero)

    def finish(acc_t):
        o1 = acc_t[:, :TILE] * (1.0 / acc_t[D_VDIM:D_VDIM + 1, :TILE])
        o2 = acc_t[:, TILE:] * (1.0 / acc_t[D_VDIM:D_VDIM + 1, TILE:])
        o = jnp.where(lane < D_VDIM, (o1 - par_ref[0:1, 0:1] * o2).T, 0.0)
        msq = jnp.sum(o * o, axis=-1, keepdims=True) * (1.0 / D_VDIM)
        o_ref[...] = (o * lax.rsqrt(msq + EPS) * par_ref[1:2, :]).astype(BF16)

    _attend(pl.program_id(2), NL, NT, ntq, lambda: q2_scr[...], k_ref, vt_ref, (s0, s1), (m0, m1), (a0, a1), finish)


def _diff_call(qd, kd, vd, dpar, dims, ntq):
    B, NT, NL = dims
    T = qd.shape[1]
    S = NT * TILE
    qmap = lambda b, h, i: (h, b * NT + jnp.minimum(i, ntq - 1), 0)
    omap = lambda b, h, i: (h, b * NT + jnp.maximum(i - 2, 0), 0)
    kmap = lambda b, h, i: (h, b, 0)
    vmap = lambda b, h, i: (h, b, 0, 0)
    return pl.pallas_call(
        functools.partial(_diff_kernel, NL=NL, NT=NT, ntq=ntq),
        grid=(B, D_HEADS, ntq + 2),
        in_specs=[pl.BlockSpec((None, TILE, LANES), qmap),
                  pl.BlockSpec((None, S, LANES), kmap),
                  pl.BlockSpec((None, None, LANES, S), vmap),
                  pl.BlockSpec((8, LANES), lambda b, h, i: (0, 0))],
        out_specs=pl.BlockSpec((None, TILE, LANES), omap),
        out_shape=jax.ShapeDtypeStruct((D_HEADS, T, LANES), BF16),
        scratch_shapes=([pltpu.VMEM((2 * TILE, LANES), BF16)] + [pltpu.VMEM((2, S, TILE), F32)] * 2
                        + [pltpu.VMEM((2, SUBLANES, TILE), F32)] * 2 + [pltpu.VMEM((LANES, 2 * TILE), F32)] * 2),
        compiler_params=_cparams(("arbitrary", "arbitrary", "arbitrary")),
        name="diff_attn",
    )(qd, kd, vd, dpar)


def _conv_kernel(z_ref, w_ref, par_ref, o_ref, zp_ref, sh_ref, *, segs):
    zeros = jnp.zeros((CONV_HALO, GROUP_W), F32)
    pos = 0
    starts = []
    for (r0, n) in segs:
        zp_ref[pos:pos + CONV_HALO, :] = zeros
        zp_ref[pos + CONV_HALO:pos + CONV_HALO + n, :] = z_ref[r0:r0 + n, :]
        starts.append(pos + CONV_HALO)
        pos += CONV_HALO + n
    zp_ref[pos:pos + CONV_HALO, :] = zeros
    bias, ln_g, ln_b = par_ref[0:1, :], par_ref[1:2, :], par_ref[2:3, :]

    for (r0, n), p0 in zip(segs, starts):
        def chunk(j, carry, r0=r0, p0=p0):
            base = pl.multiple_of(j * CONV_ROWS, CONV_ROWS)
            win = zp_ref[pl.ds(base + (p0 - CONV_HALO), CONV_ROWS + 2 * CONV_HALO), :]
            acc = jnp.zeros((CONV_ROWS, GROUP_W), F32)
            first = CONV_HALO - CONV_W // 2
            for res in range(SUBLANES):
                taps = [k for k in range(CONV_W) if (first + k) % SUBLANES == res]
                if not taps:
                    continue
                span = (first + taps[-1]) - res + CONV_ROWS
                sh_ref[res, 0:span, :] = win[res:res + span, :]
                for k in taps:
                    a = first + k - res
                    acc = acc + sh_ref[res, a:a + CONV_ROWS, :] * w_ref[k:k + 1, :]
            u = acc + bias
            mu = jnp.mean(u, axis=-1, keepdims=True)
            uc = u - mu
            var = jnp.mean(uc * uc, axis=-1, keepdims=True)
            y = uc * lax.rsqrt(var + EPS) * ln_g + ln_b
            o_ref[pl.ds(pl.multiple_of(base + r0, CONV_ROWS), CONV_ROWS), :] = (
                y * _sigmoid(y)).astype(BF16)
            return carry

        lax.fori_loop(0, n // CONV_ROWS, chunk, 0)


def _conv_call(z, conv_w, conv_par, dims, with_ctx):
    B, NT, NL = dims
    T = z.shape[0]
    S = NT * TILE
    L = NL * TILE
    segs = ((0, L), (L, S - L)) if with_ctx else ((0, L),)
    pad_rows = sum(n for _, n in segs) + CONV_HALO * (len(segs) + 1)
    return pl.pallas_call(
        functools.partial(_conv_kernel, segs=segs),
        grid=(B,),
        in_specs=[pl.BlockSpec((S, GROUP_W), lambda b: (b, 0)),
                  pl.BlockSpec((32, GROUP_W), lambda b: (0, 0)),
                  pl.BlockSpec((8, GROUP_W), lambda b: (0, 0))],
        out_specs=pl.BlockSpec((S, GROUP_W), lambda b: (b, 0)),
        out_shape=jax.ShapeDtypeStruct((T, GROUP_W), BF16),
        scratch_shapes=[pltpu.VMEM((pad_rows, GROUP_W), F32),
                        pltpu.VMEM((SUBLANES, CONV_ROWS + 2 * CONV_HALO, GROUP_W), F32)],
        compiler_params=_cparams(("arbitrary",)),
        name="conformer_conv",
    )(z, conv_w, conv_par)


def _dft_kernel(c_ref, s_ref, x_ref, cb_ref, sb_ref, o_ref, *, r0, n):
    xs = x_ref[r0:r0 + n, :]
    u = jnp.dot(c_ref[...], xs, preferred_element_type=F32).astype(BF16)
    w = jnp.dot(s_ref[...], xs, preferred_element_type=F32).astype(BF16)
    y = (jnp.dot(u, cb_ref[...], preferred_element_type=F32)
         - jnp.dot(w, sb_ref[...], preferred_element_type=F32))
    o_ref[...] = y.astype(BF16)


def _dft_call(cu, cmat, smat, cbd, sbd, dims, r0, n, prev=None):
    B, NT, NL = dims
    T = cu.shape[0]
    S = NT * TILE
    nr = n // TILE
    t0 = r0 // TILE
    args = [cmat, smat, cu, cbd, sbd]
    in_specs = [pl.BlockSpec((TILE, n), lambda r, b: (r, 0)),
                pl.BlockSpec((TILE, n), lambda r, b: (r, 0)),
                pl.BlockSpec((S, GROUP_W), lambda r, b: (b, 0)),
                pl.BlockSpec((GROUP_W, GROUP_W), lambda r, b: (0, 0)),
                pl.BlockSpec((GROUP_W, GROUP_W), lambda r, b: (0, 0))]
    kern = functools.partial(_dft_kernel, r0=r0, n=n)
    aliases = {}
    if prev is not None:
        args.append(prev)
        in_specs.append(pl.BlockSpec(memory_space=pl.ANY))
        aliases = {5: 0}
        kern = lambda c, s, x, cb, sb, _prev, o: _dft_kernel(c, s, x, cb, sb, o, r0=r0, n=n)
    return pl.pallas_call(
        kern,
        grid=(nr, B),
        in_specs=in_specs,
        out_specs=pl.BlockSpec((TILE, GROUP_W), lambda r, b: (b * NT + t0 + r, 0)),
        out_shape=jax.ShapeDtypeStruct((T, GROUP_W), BF16),
        input_output_aliases=aliases,
        compiler_params=_cparams(("arbitrary", "arbitrary")),
        name="fourier_mix",
    )(*args)


def _out_kernel(x_ref, oa_ref, ob_ref, oc_ref, od_ref, w_ref, g1_ref, a2_ref, sh2_ref, rw_ref, rb_ref,
                xn_ref, h2_ref, ti_ref, gt_ref, cnt_ref, slab_ref):
    ocat = jnp.concatenate([oa_ref[hh] for hh in range(A_HEADS)] + [ob_ref[...], oc_ref[...]]
                           + [od_ref[hh] for hh in range(D_HEADS)], axis=1)
    r = jnp.dot(ocat, w_ref[...], preferred_element_type=F32)
    xn = x_ref[...] + g1_ref[...] * r
    xn_ref[...] = xn
    ms = jnp.mean(xn * xn, axis=-1, keepdims=True)
    h2 = xn * lax.rsqrt(ms + EPS) * a2_ref[...] + sh2_ref[...]
    _rows_to_tiles(h2, h2_ref, slab_ref)
    logits = jnp.dot(h2.astype(BF16), rw_ref[...], preferred_element_type=F32) + rb_ref[...]
    lane = _lane_iota(logits.shape)
    lanef = lane.astype(F32)
    vals, idxs = [], []
    cur = logits
    for _ in range(TOP_K):
        mx = jnp.max(cur, axis=-1, keepdims=True)
        idx = jnp.min(jnp.where(cur == mx, lanef, float(LANES)), axis=-1, keepdims=True)
        vals.append(mx)
        idxs.append(idx)
        cur = jnp.where(lanef == idx, NEG_BIG * 2, cur)
    exps = [jnp.exp(v - vals[0]) for v in vals]
    inv = 1.0 / (exps[0] + exps[1] + exps[2] + exps[3])
    ti = jnp.zeros(logits.shape, F32)
    gt = jnp.zeros(logits.shape, F32)
    onehot = jnp.zeros(logits.shape, F32)
    for k in range(TOP_K):
        ti = jnp.where(lane == k, idxs[k], ti)
        gt = jnp.where(lane == k, exps[k] * inv, gt)
        onehot = onehot + jnp.where(lanef == idxs[k], 1.0, 0.0)
    ti_ref[...] = ti.astype(I32)
    gt_ref[...] = gt

    @pl.when((pl.program_id(0) == 0) & (pl.program_id(1) == 0))
    def _():
        cnt_ref[...] = jnp.zeros(cnt_ref.shape, F32)

    cnt_ref[0:1, :] += jnp.sum(onehot, axis=0, keepdims=True)


def _out_call(x_all, oa, ob, oc, od, w_out_p, g1, a2, sh2, rw, rb, dims, ntq):
    B, NT, NL = dims
    n_tok = B * ntq * TILE
    sel = lambda b, i: (jnp.where(i >= NL, B, b), 0, 0)
    row = lambda b, i: (b * NT + i, 0)
    hrow = lambda b, i: (0, b * NT + i, 0)
    crow = lambda b, i: (b * ntq + i, 0)
    const = lambda b, i: (0, 0)
    return pl.pallas_call(
        _out_kernel,
        grid=(B, ntq),
        in_specs=[pl.BlockSpec((TILE, D_MODEL), row),
                  pl.BlockSpec((A_HEADS, TILE, LANES), hrow),
                  pl.BlockSpec((TILE, GROUP_W), row),
                  pl.BlockSpec((TILE, GROUP_W), row),
                  pl.BlockSpec((D_HEADS, TILE, LANES), hrow),
                  pl.BlockSpec((OUT_K, D_MODEL), const),
                  pl.BlockSpec((None, 1, D_MODEL), sel),
                  pl.BlockSpec((None, 1, D_MODEL), sel),
                  pl.BlockSpec((None, 1, D_MODEL), sel),
                  pl.BlockSpec((D_MODEL, LANES), const),
                  pl.BlockSpec((1, LANES), const)],
        out_specs=[pl.BlockSpec((TILE, D_MODEL), crow),
                   pl.BlockSpec((TILE * ROW_SUB, LANES), crow),
                   pl.BlockSpec((TILE, LANES), crow),
                   pl.BlockSpec((TILE, LANES), crow),
                   pl.BlockSpec((8, LANES), const)],
        out_shape=[jax.ShapeDtypeStruct((n_tok, D_MODEL), F32),
                   jax.ShapeDtypeStruct((n_tok * ROW_SUB, LANES), F32),
                   jax.ShapeDtypeStruct((n_tok, LANES), I32),
                   jax.ShapeDtypeStruct((n_tok, LANES), F32),
                   jax.ShapeDtypeStruct((8, LANES), F32)],
        scratch_shapes=[_slab_scratch()],
        compiler_params=_cparams(("arbitrary", "arbitrary")),
        name="out_proj_router",
    )(x_all, oa, ob, oc, od, w_out_p, g1, a2, sh2, rw, rb)


def _rank_kernel(ti_ref, ps_ref, tri_ref, dest_ref, carry_ref):
    @pl.when(pl.program_id(0) == 0)
    def _():
        carry_ref[...] = jnp.zeros(carry_ref.shape, F32)

    ti = ti_ref[...]
    lane = _lane_iota(ti.shape)
    hits = [lane == ti[:, k:k + 1] for k in range(TOP_K)]
    onehot = jnp.zeros(ti.shape, F32)
    for hk in hits:
        onehot = onehot + jnp.where(hk, 1.0, 0.0)
    before = jnp.dot(tri_ref[...], onehot.astype(BF16), preferred_element_type=F32)
    base = before + carry_ref[0:1, :] + ps_ref[...]
    dest = jnp.zeros(ti.shape, F32)
    for k, hk in enumerate(hits):
        dk = jnp.sum(jnp.where(hk, base, 0.0), axis=-1, keepdims=True)
        dest = jnp.where(lane == k, dk, dest)
    dest_ref[...] = dest.astype(I32)
    carry_ref[0:1, :] += jnp.sum(onehot, axis=0, keepdims=True)


def _rank_call(topi, pad_start, tri):
    n_tok = topi.shape[0]
    return pl.pallas_call(
        _rank_kernel,
        grid=(n_tok // TILE,),
        in_specs=[pl.BlockSpec((TILE, LANES), lambda t: (t, 0)),
                  pl.BlockSpec((1, LANES), lambda t: (0, 0)),
                  pl.BlockSpec((TILE, TILE), lambda t: (0, 0))],
        out_specs=pl.BlockSpec((TILE, LANES), lambda t: (t, 0)),
        out_shape=jax.ShapeDtypeStruct((n_tok, LANES), I32),
        scratch_shapes=[pltpu.VMEM((8, LANES), F32)],
        compiler_params=_cparams(("arbitrary",)),
        name="slot_rank",
    )(topi, pad_start, tri)


def _tile_copy(src_ref, src_row, dst_ref, dst_row, sem):
    src = src_ref.at[pl.ds(pl.multiple_of(src_row * ROW_SUB, ROW_SUB), ROW_SUB)]
    dst = dst_ref.at[pl.ds(pl.multiple_of(dst_row * ROW_SUB, ROW_SUB), ROW_SUB)]
    return pltpu.make_async_copy(src, dst, sem)


def _dispatch_kernel(fill_start_ref, fill_on_ref, dest_ref, h2_ref, zero_ref, xs_ref, sem):
    @pl.when(pl.program_id(0) == 0)
    def _():
        def fill(e, wait):
            @pl.when(fill_on_ref[e] > 0)
            def _():
                start = pl.multiple_of(fill_start_ref[e] * ROW_SUB, MOE_BLOCK * ROW_SUB)
                cp = pltpu.make_async_copy(zero_ref, xs_ref.at[pl.ds(start, MOE_BLOCK * ROW_SUB)], sem.at[1])
                if wait:
                    cp.wait()
                else:
                    cp.start()

        lax.fori_loop(0, N_EXPERTS, lambda e, c: (fill(e, False), c)[1], 0)
        lax.fori_loop(0, N_EXPERTS, lambda e, c: (fill(e, True), c)[1], 0)

    def issue(r, c):
        for k in range(TOP_K):
            _tile_copy(h2_ref, r, xs_ref, dest_ref[0, 0, TOP_K * r + k], sem.at[0]).start(priority=k % 2)
        return c

    lax.fori_loop(0, TILE, issue, 0, unroll=DMA_UNROLL)
    for k in range(TOP_K):
        pltpu.make_async_copy(h2_ref, xs_ref.at[pl.ds(0, TILE * ROW_SUB)], sem.at[0]).wait()


def _dispatch_call(fill_start, fill_on, dest3, h2t, n_slots):
    n_tok = h2t.shape[0] // ROW_SUB
    zero_blk = jnp.zeros((MOE_BLOCK * ROW_SUB, LANES), F32)
    return pl.pallas_call(
        _dispatch_kernel,
        grid_spec=pltpu.PrefetchScalarGridSpec(
            num_scalar_prefetch=2,
            grid=(n_tok // TILE,),
            in_specs=[pl.BlockSpec((1, 1, TOP_K * TILE), lambda t, fs, fo: (t, 0, 0), memory_space=pltpu.SMEM),
                      pl.BlockSpec((TILE * ROW_SUB, LANES), lambda t, fs, fo: (t, 0)),
                      pl.BlockSpec((MOE_BLOCK * ROW_SUB, LANES), lambda t, fs, fo: (0, 0))],
            out_specs=pl.BlockSpec(memory_space=pl.ANY),
            scratch_shapes=[pltpu.SemaphoreType.DMA((2,))]),
        out_shape=jax.ShapeDtypeStruct((n_slots * ROW_SUB, LANES), F32),
        compiler_params=_cparams(("arbitrary",)),
        name="moe_dispatch",
    )(fill_start, fill_on, dest3, h2t, zero_blk)


def _expert_kernel(blk_e_ref, nact_ref, xs_ref, wgu_ref, bgu_ref, wd_ref, bd_ref, yb_ref, slab_ref):
    @pl.when(pl.program_id(0) < nact_ref[0])
    def _():
        x = _tiles_to_rows(xs_ref, slab_ref).astype(BF16)
        gu = jnp.dot(x, wgu_ref[...], preferred_element_type=F32) + bgu_ref[...]
        g = jnp.minimum(gu[:, :D_FF], SWIGLU_LIMIT)
        u = jnp.clip(gu[:, D_FF:], -SWIGLU_LIMIT, SWIGLU_LIMIT)
        act = g * _sigmoid(SWIGLU_ALPHA * g) * (u + 1.0)
        y = jnp.dot(act.astype(BF16), wd_ref[...], preferred_element_type=F32) + bd_ref[...]
        _rows_to_tiles(y, yb_ref, slab_ref)


def _expert_call(blk_e, n_active, xs, w_gu, b_gu, w_down, b_down):
    n_slots = xs.shape[0] // ROW_SUB
    n_blocks = n_slots // MOE_BLOCK
    blk = lambda j, be, na: (jnp.minimum(j, na[0] - 1), 0)
    wsel = lambda j, be, na: (be[jnp.minimum(j, na[0] - 1)], 0, 0)
    return pl.pallas_call(
        _expert_kernel,
        grid_spec=pltpu.PrefetchScalarGridSpec(
            num_scalar_prefetch=2,
            grid=(n_blocks,),
            in_specs=[pl.BlockSpec((MOE_BLOCK * ROW_SUB, LANES), blk),
                      pl.BlockSpec((None, D_MODEL, 2 * D_FF), wsel),
                      pl.BlockSpec((None, 1, 2 * D_FF), wsel),
                      pl.BlockSpec((None, D_FF, D_MODEL), wsel),
                      pl.BlockSpec((None, 1, D_MODEL), wsel)],
            out_specs=pl.BlockSpec((MOE_BLOCK * ROW_SUB, LANES), blk),
            scratch_shapes=[_slab_scratch()]),
        out_shape=jax.ShapeDtypeStruct((n_slots * ROW_SUB, LANES), F32),
        compiler_params=_cparams(("arbitrary",)),
        name="moe_experts",
    )(blk_e, n_active, xs, w_gu, b_gu, w_down, b_down)


def _combine_kernel(dest_ref, x_ref, gt_ref, g2_ref, yb_ref, o_ref, buf, slab_ref, sem):
    def issue(r, c):
        for k in range(TOP_K):
            _tile_copy(yb_ref, dest_ref[0, 0, TOP_K * r + k], buf.at[k], r, sem.at[0]).start(priority=k % 2)
        return c

    lax.fori_loop(0, TILE, issue, 0, unroll=DMA_UNROLL)
    for k in range(TOP_K):
        pltpu.make_async_copy(yb_ref.at[pl.ds(0, TILE * ROW_SUB)], buf.at[k], sem.at[0]).wait()
    gt = gt_ref[...]
    y = gt[:, 0:1] * _tiles_to_rows(buf.at[0], slab_ref)
    for k in range(1, TOP_K):
        y = y + gt[:, k:k + 1] * _tiles_to_rows(buf.at[k], slab_ref)
    o_ref[...] = x_ref[...] + g2_ref[...] * y


def _combine_call(dest3, xn, gates, g2, yb, dims, ntq):
    B, NT, NL = dims
    n_tok = xn.shape[0]
    sel = lambda b, i: (jnp.where(i >= NL, B, b), 0, 0)
    crow = lambda b, i: (b * ntq + i, 0)
    return pl.pallas_call(
        _combine_kernel,
        grid=(B, ntq),
        in_specs=[pl.BlockSpec((1, 1, TOP_K * TILE), lambda b, i: (b * ntq + i, 0, 0), memory_space=pltpu.SMEM),
                  pl.BlockSpec((TILE, D_MODEL), crow),
                  pl.BlockSpec((TILE, LANES), crow),
                  pl.BlockSpec((None, 1, D_MODEL), sel),
                  pl.BlockSpec(memory_space=pl.ANY)],
        out_specs=pl.BlockSpec((TILE, D_MODEL), crow),
        out_shape=jax.ShapeDtypeStruct((n_tok, D_MODEL), F32),
        scratch_shapes=[pltpu.VMEM((TOP_K, TILE * ROW_SUB, LANES), F32), _slab_scratch(),
                        pltpu.SemaphoreType.DMA((1,))],
        compiler_params=_cparams(("arbitrary", "arbitrary")),
        name="moe_combine",
    )(dest3, xn, gates, g2, yb)


def _pad_heads_cols(w, n_heads, width):
    w = w.reshape(w.shape[0], n_heads, width)
    return jnp.pad(w, ((0, 0), (0, 0), (0, LANES - width))).reshape(w.shape[0], n_heads * LANES)


def _pad_heads_rows(w, n_heads, width):
    w = w.reshape(n_heads, width, w.shape[1])
    return jnp.pad(w, ((0, 0), (0, LANES - width), (0, 0))).reshape(n_heads * LANES, w.shape[2])


def _prep_w_in(w):
    splits = (256, 128, 128, 256, 256, 256, 256, 256, 256)
    parts, o = [], 0
    for s in splits:
        parts.append(w[:, o:o + s])
        o += s
    aq, ak, av, bu, bg, cu, dq, dk, dv = parts
    return jnp.concatenate([
        _pad_heads_cols(aq, A_HEADS, HEAD_DIM), _pad_heads_cols(ak, A_KV_HEADS, HEAD_DIM),
        _pad_heads_cols(av, A_KV_HEADS, HEAD_DIM), bu, bg, cu,
        _pad_heads_cols(dq, D_HEADS, 2 * D_QKDIM), _pad_heads_cols(dk, D_HEADS, 2 * D_QKDIM),
        _pad_heads_cols(dv, D_HEADS, D_VDIM)], axis=1).astype(BF16)


def _prep_w_out(w):
    return jnp.concatenate([
        _pad_heads_rows(w[0:GROUP_W], A_HEADS, HEAD_DIM), w[GROUP_W:3 * GROUP_W],
        _pad_heads_rows(w[3 * GROUP_W:], D_HEADS, D_VDIM)], axis=0).astype(BF16)


def _pad_lanes(v):
    return jnp.pad(v.astype(F32), (0, LANES - v.shape[0]))


def _rope_tables(L, C):
    t = jnp.arange(L, dtype=jnp.int32)
    rows = (t // GRID_W).astype(F32)[:, None]
    cols = (t % GRID_W).astype(F32)[:, None]
    lane = jnp.arange(LANES)

    def table(group):
        nfreq = group // 2
        j = lane % nfreq
        inv = jnp.power(ROPE_THETA, -(2.0 * j.astype(F32)) / group)[None, :]
        use_cols = ((lane // group) % 2) == 1
        ang = jnp.where(use_cols[None, :], cols, rows) * inv
        sign = jnp.where((lane % group) < nfreq, -1.0, 1.0)[None, :]
        live = (lane < 2 * D_QKDIM)[None, :]
        cos = jnp.where(live, jnp.cos(ang), 1.0)
        sin = jnp.where(live, jnp.sin(ang) * sign, 0.0)
        cos = jnp.concatenate([cos, jnp.ones((C, LANES), F32)], axis=0)
        sin = jnp.concatenate([sin, jnp.zeros((C, LANES), F32)], axis=0)
        return cos, sin

    ca, sa = table(HEAD_DIM // 2)
    cd, sd = table(D_QKDIM // 2)
    return ca, sa, cd, sd


def _dft_mats(n):
    r = 1 << (int(math.log2(n)) // 2)
    assert n % r == 0
    k = jnp.arange(n, dtype=jnp.int32)[None, :]

    def table(rows, period):
        idx = jnp.arange(rows, dtype=jnp.int32)[:, None]
        ang = ((idx * k) % period).astype(F32) * (2.0 * math.pi / period)
        return jnp.cos(ang), jnp.sin(ang)

    c1, s1 = table(n // r, n // r)
    c2, s2 = table(r, n)
    c1, s1, c2, s2 = c1[:, None, :], s1[:, None, :], c2[None, :, :], s2[None, :, :]
    scale = 1.0 / math.sqrt(n)
    cos = ((c1 * c2 - s1 * s2) * scale).reshape(n, n).astype(BF16)
    sin = ((s1 * c2 + c1 * s2) * scale).reshape(n, n).astype(BF16)
    return cos, sin


def _channel_dft_mats():
    w = GROUP_W // FNET_GROUPS
    k = jnp.arange(GROUP_W, dtype=jnp.int32)
    same = (k[:, None] // w) == (k[None, :] // w)
    prod = ((k[:, None] % w) * (k[None, :] % w)) % w
    ang = prod.astype(F32) * (2.0 * math.pi / w)
    scale = 1.0 / math.sqrt(w)
    return (jnp.where(same, jnp.cos(ang) * scale, 0.0).astype(BF16),
            jnp.where(same, jnp.sin(ang) * scale, 0.0).astype(BF16))


def _slot_plan(counts, n_blocks):
    cnt = counts[0, :N_EXPERTS].astype(I32)
    padded = (cnt + MOE_BLOCK - 1) // MOE_BLOCK * MOE_BLOCK
    pad_ends = jnp.cumsum(padded)
    pad_starts = pad_ends - padded
    n_active = (pad_ends[-1] // MOE_BLOCK).reshape(1).astype(I32)
    blk_start = jnp.arange(n_blocks, dtype=I32) * MOE_BLOCK
    blk_e = jnp.minimum(jnp.sum(pad_ends[None, :] <= blk_start[:, None], axis=1), N_EXPERTS - 1).astype(I32)
    fill_start = jnp.maximum(pad_ends - MOE_BLOCK, 0).astype(I32)
    fill_on = (cnt > 0).astype(I32)
    ps_vec = jnp.pad(pad_starts.astype(F32), (0, LANES - N_EXPERTS)).reshape(1, LANES)
    return ps_vec, blk_e, n_active, fill_start, fill_on


def kernel(x, c, ctx, c_ctx, norm1_g, norm2_g, ada_w, ada_b, w_in, a_qnorm_g, a_knorm_g, conv_dw_w, conv_dw_b,
           conv_ln_g, conv_ln_b, d_qnorm_g, d_knorm_g, d_lambda, d_subln_g, w_out, router_w, router_b,
           exp_w_gu, exp_b_gu, exp_w_down, exp_b_down):
    B, L, D = x.shape
    C = ctx.shape[1]
    depth = ada_w.shape[0]
    assert D == D_MODEL and L % TILE == 0 and C == TILE and L % GRID_W == 0
    NL, NT = L // TILE, (L + C) // TILE
    dims = (B, NT, NL)
    S = NT * TILE

    x_all = jnp.concatenate([x, ctx], axis=1).reshape(B * S, D)

    a = jnp.concatenate([c, c_ctx[None, :]], axis=0)
    a = jnp.pad(a * jax.nn.sigmoid(a), ((0, 16 - (B + 1) % 16 if (B + 1) % 16 else 0), (0, 0)))
    mod_all = _ada_call(a, ada_w, ada_b)

    rope = _rope_tables(L, C)
    cmat, smat = _dft_mats(L)
    cmat_c, smat_c = _dft_mats(C)
    cbd, sbd = _channel_dft_mats()
    tri = jnp.tril(jnp.ones((TILE, TILE), F32), -1).astype(BF16)

    x_lat = None
    for li in range(depth):
        last = li == depth - 1
        ntq = NL if last else NT
        mod = mod_all[li, :B + 1].reshape(B + 1, 6, 1, D)
        sh1, sc1, g1, sh2, sc2, g2 = (mod[:, j] for j in range(6))
        a1 = norm1_g[li][None, None, :] * (1.0 + sc1)
        a2 = norm2_g[li][None, None, :] * (1.0 + sc2)

        gains = jnp.stack([
            _pad_lanes(a_qnorm_g[li]) * (HEAD_DIM ** -0.5 * LOG2E), _pad_lanes(a_knorm_g[li]),
            _pad_lanes(jnp.tile(d_qnorm_g[li], 2)) * (D_QKDIM ** -0.5 * LOG2E), _pad_lanes(jnp.tile(d_knorm_g[li], 2)),
        ] + [jnp.zeros((LANES,), F32)] * 4)
        qa, ka, va, z, cu, qd, kd, vd = _proj_call(x_all, a1, sh1, _prep_w_in(w_in[li]), gains, rope, dims)

        lam_init = 0.8 - 0.6 * math.exp(-0.3 * li)
        lq1, lk1, lq2, lk2 = d_lambda[li].astype(F32)
        lam = jnp.exp(jnp.sum(lq1 * lk1)) - jnp.exp(jnp.sum(lq2 * lk2)) + lam_init
        dpar = jnp.stack([jnp.full((LANES,), lam, F32), _pad_lanes(d_subln_g[li]) * (1.0 - lam_init)]
                         + [jnp.zeros((LANES,), F32)] * 6)

        oa = _gqa_call(qa, ka, va, dims, ntq)
        od = _diff_call(qd, kd, vd, dpar, dims, ntq)
        conv_w = jnp.pad(conv_dw_w[li], ((0, 32 - CONV_W), (0, 0)))
        conv_par = jnp.stack([conv_dw_b[li], conv_ln_g[li], conv_ln_b[li]] + [jnp.zeros((GROUP_W,), F32)] * 5)
        ob = _conv_call(z, conv_w, conv_par, dims, with_ctx=not last)
        oc = _dft_call(cu, cmat, smat, cbd, sbd, dims, 0, L)
        if not last:
            oc = _dft_call(cu, cmat_c, smat_c, cbd, sbd, dims, L, C, prev=oc)

        rw = jnp.pad(router_w[li], ((0, 0), (0, LANES - N_EXPERTS))).astype(BF16)
        rb = jnp.concatenate([router_b[li].astype(F32), jnp.full((LANES - N_EXPERTS,), NEG_BIG, F32)]).reshape(1, LANES)
        xn, h2t, topi, gates, counts = _out_call(x_all, oa, ob, oc, od, _prep_w_out(w_out[li]), g1, a2, sh2,
                                                 rw, rb, dims, ntq)

        n_tok = B * ntq * TILE
        n_blocks = -(-(n_tok * TOP_K + N_EXPERTS * (MOE_BLOCK - 1)) // MOE_BLOCK)
        ps_vec, blk_e, n_active, fill_start, fill_on = _slot_plan(counts, n_blocks)
        dest = _rank_call(topi, ps_vec, tri)
        dest3 = dest[:, :TOP_K].reshape(n_tok // TILE, 1, TOP_K * TILE)
        xs = _dispatch_call(fill_start, fill_on, dest3, h2t, n_blocks * MOE_BLOCK)
        yb = _expert_call(blk_e, n_active, xs, exp_w_gu[li].astype(BF16), exp_b_gu[li][:, None, :],
                          exp_w_down[li].astype(BF16), exp_b_down[li][:, None, :])
        x_next = _combine_call(dest3, xn, gates, g2, yb, dims, ntq)
        if last:
            x_lat = x_next
        else:
            x_all = x_next
    return x_lat.reshape(B, L, D)
```

```python
import functools
import math

import jax
import jax.numpy as jnp
from jax import lax
from jax.experimental import pallas as pl
from jax.experimental.pallas import tpu as pltpu

F32 = jnp.float32
BF16 = jnp.bfloat16
I32 = jnp.int32

D_MODEL = 1024
TILE = 256
LANES = 128
GRID_W = 64
HEAD_DIM = 64
A_HEADS = 4
A_KV_HEADS = 2
CONV_W = 31
CONV_HALO = 16
CONV_ROWS = 128
FNET_GROUPS = 4
D_HEADS = 4
D_QKDIM = 32
D_VDIM = 64
GROUP_W = 256
N_EXPERTS = 32
TOP_K = 4
D_FF = D_MODEL
SWIGLU_LIMIT = 7.0
SWIGLU_ALPHA = 1.702
MOE_BLOCK = 256
KEY_CHUNK = 256
DMA_UNROLL = 8
ROPE_THETA = 10000.0
EPS = 1e-6
NEG_BIG = -1e30
LOG2E = math.log2(math.e)
SUBLANES = 8
ROW_SUB = D_MODEL // LANES
assert ROW_SUB == SUBLANES
ROW_TILE = TILE
ROW_PITCH = ROW_TILE + SUBLANES

PROJ_W = (A_HEADS * LANES, A_KV_HEADS * LANES, A_KV_HEADS * LANES, GROUP_W, GROUP_W, GROUP_W,
          D_HEADS * LANES, D_HEADS * LANES, D_HEADS * LANES)
PROJ_OFF = tuple(sum(PROJ_W[:i]) for i in range(len(PROJ_W) + 1))
OUT_K = A_HEADS * LANES + 2 * GROUP_W + D_HEADS * LANES

VMEM_LIMIT = 52 * 1024 * 1024


def _cparams(sem):
    return pltpu.CompilerParams(dimension_semantics=sem, vmem_limit_bytes=VMEM_LIMIT)


def _lane_iota(shape):
    return lax.broadcasted_iota(I32, shape, len(shape) - 1)


def _sigmoid(x):
    return 0.5 * jnp.tanh(0.5 * x) + 0.5


def _rows_to_tiles(h, out_ref, slab_ref):
    for j in range(ROW_SUB):
        slab_ref[j * ROW_PITCH:j * ROW_PITCH + ROW_TILE, :] = h[:, j * LANES:(j + 1) * LANES]
    for t in range(ROW_TILE):
        out_ref[ROW_SUB * t:ROW_SUB * (t + 1), :] = slab_ref[pl.ds(t, ROW_SUB, stride=ROW_PITCH), :]


def _tiles_to_rows(in_ref, slab_ref):
    for t in range(ROW_TILE):
        slab_ref[pl.ds(t, ROW_SUB, stride=ROW_PITCH), :] = in_ref[ROW_SUB * t:ROW_SUB * (t + 1), :]
    return jnp.concatenate([slab_ref[j * ROW_PITCH:j * ROW_PITCH + ROW_TILE, :] for j in range(ROW_SUB)], axis=1)


def _slab_scratch():
    return pltpu.VMEM((ROW_SUB * ROW_PITCH, LANES), F32)


def _ada_kernel(a_ref, w_ref, b_ref, o_ref):
    o_ref[...] = jnp.dot(a_ref[...], w_ref[...], preferred_element_type=F32,
                         precision=lax.Precision.HIGHEST) + b_ref[...]


def _ada_call(a, ada_w, ada_b):
    depth, d, n = ada_w.shape
    rows = a.shape[0]
    nb = n // d
    return pl.pallas_call(
        _ada_kernel,
        grid=(depth, nb),
        in_specs=[pl.BlockSpec((rows, d), lambda l, j: (0, 0)),
                  pl.BlockSpec((None, d, d), lambda l, j: (l, 0, j)),
                  pl.BlockSpec((None, 1, d), lambda l, j: (l, 0, j))],
        out_specs=pl.BlockSpec((None, rows, d), lambda l, j: (l, 0, j)),
        out_shape=jax.ShapeDtypeStruct((depth, rows, n), F32),
        compiler_params=_cparams(("arbitrary", "arbitrary")),
        name="ada_mod",
    )(a, ada_w, ada_b.reshape(depth, 1, n))


def _swap_halves(x, half):
    lane = _lane_iota(x.shape)
    first = (lane & (2 * half - 1)) < half
    return jnp.where(first, pltpu.roll(x, LANES - half, 1), pltpu.roll(x, half, 1))


def _proj_kernel(x_ref, a_ref, sh_ref, w_ref, gains_ref, ca_ref, sa_ref, cd_ref, sd_ref,
                 qa_ref, ka_ref, va_ref, z_ref, cu_ref, qd_ref, kd_ref, vd_ref):
    x = x_ref[...]
    ms = jnp.mean(x * x, axis=-1, keepdims=True)
    h = (x * lax.rsqrt(ms + EPS) * a_ref[...] + sh_ref[...]).astype(BF16)
    p = jnp.dot(h, w_ref[...], preferred_element_type=F32)
    lane = _lane_iota((TILE, LANES))
    ca, sa, cd, sd = ca_ref[...], sa_ref[...], cd_ref[...], sd_ref[...]

    def slab(sec, i):
        o = PROJ_OFF[sec] + i * LANES
        return p[:, o:o + LANES]

    def norm_rope_a(s, gain):
        msq = jnp.sum(s * s, axis=-1, keepdims=True) * (1.0 / HEAD_DIM)
        sn = s * lax.rsqrt(msq + EPS) * gain
        return (sn * ca + _swap_halves(sn, HEAD_DIM // 4) * sa).astype(BF16)

    def norm_rope_d(s, gain):
        s2 = s * s
        m1 = jnp.sum(jnp.where(lane < D_QKDIM, s2, 0.0), axis=-1, keepdims=True)
        m2 = jnp.sum(jnp.where(lane >= D_QKDIM, s2, 0.0), axis=-1, keepdims=True)
        msq = jnp.where(lane < D_QKDIM, m1, m2) * (1.0 / D_QKDIM)
        sn = s * lax.rsqrt(msq + EPS) * gain
        return (sn * cd + _swap_halves(sn, D_QKDIM // 4) * sd).astype(BF16)

    def with_ones(s, width):
        st = s.T
        chan = lax.broadcasted_iota(I32, st.shape, 0)
        return jnp.where(chan == width, 1.0, st).astype(BF16)

    for hh in range(A_HEADS):
        qa_ref[hh] = norm_rope_a(slab(0, hh), gains_ref[0:1, :])
    for hh in range(A_KV_HEADS):
        ka_ref[hh] = norm_rope_a(slab(1, hh), gains_ref[1:2, :])
        va_ref[hh] = with_ones(slab(2, hh), HEAD_DIM)
    bu = p[:, PROJ_OFF[3]:PROJ_OFF[4]]
    bg = p[:, PROJ_OFF[4]:PROJ_OFF[5]]
    z_ref[...] = bu * _sigmoid(bg)
    cu_ref[...] = p[:, PROJ_OFF[5]:PROJ_OFF[6]].astype(BF16)
    for hh in range(D_HEADS):
        qd_ref[hh] = norm_rope_d(slab(6, hh), gains_ref[2:3, :])
        kd_ref[hh] = norm_rope_d(slab(7, hh), gains_ref[3:4, :])
        vd_ref[hh] = with_ones(slab(8, hh), D_VDIM)


def _proj_call(x_all, a1, sh1, w_in_p, gains, rope, dims):
    B, NT, NL = dims
    T = x_all.shape[0]
    S = NT * TILE
    sel = lambda b, i: (jnp.where(i >= NL, B, b), 0, 0)
    row = lambda b, i: (b * NT + i, 0)
    hrow = lambda b, i: (0, b * NT + i, 0)
    tab = lambda b, i: (i, 0)
    head = lambda n: pl.BlockSpec((n, TILE, LANES), hrow)
    hshape = lambda n: jax.ShapeDtypeStruct((n, T, LANES), BF16)
    head_t = lambda n: pl.BlockSpec((n, None, LANES, TILE), lambda b, i: (0, b, 0, i))
    hshape_t = lambda n: jax.ShapeDtypeStruct((n, B, LANES, S), BF16)
    nw = w_in_p.shape[1]
    return pl.pallas_call(
        _proj_kernel,
        grid=(B, NT),
        in_specs=[pl.BlockSpec((TILE, D_MODEL), row),
                  pl.BlockSpec((None, 1, D_MODEL), sel),
                  pl.BlockSpec((None, 1, D_MODEL), sel),
                  pl.BlockSpec((D_MODEL, nw), lambda b, i: (0, 0)),
                  pl.BlockSpec((8, LANES), lambda b, i: (0, 0)),
                  pl.BlockSpec((TILE, LANES), tab), pl.BlockSpec((TILE, LANES), tab),
                  pl.BlockSpec((TILE, LANES), tab), pl.BlockSpec((TILE, LANES), tab)],
        out_specs=[head(A_HEADS), head(A_KV_HEADS), head_t(A_KV_HEADS),
                   pl.BlockSpec((TILE, GROUP_W), row), pl.BlockSpec((TILE, GROUP_W), row),
                   head(D_HEADS), head(D_HEADS), head_t(D_HEADS)],
        out_shape=[hshape(A_HEADS), hshape(A_KV_HEADS), hshape_t(A_KV_HEADS),
                   jax.ShapeDtypeStruct((T, GROUP_W), F32), jax.ShapeDtypeStruct((T, GROUP_W), BF16),
                   hshape(D_HEADS), hshape(D_HEADS), hshape_t(D_HEADS)],
        compiler_params=_cparams(("arbitrary", "arbitrary")),
        name="in_proj",
    )(x_all, a1, sh1, w_in_p, gains, *rope)


def _scores_phase(q, k_ref, k0, nchunks, s_ref, m_ref):
    half = q.shape[0] // 2
    nk = nchunks * KEY_CHUNK
    for h in range(2):
        s_ref[h, 0:nk, :] = lax.dot_general(k_ref[k0:k0 + nk, :], q[h * half:(h + 1) * half],
                                            (((1,), (1,)), ((), ())), preferred_element_type=F32)
    for h in range(2):
        m8 = None
        for c in range(nchunks):
            blk = s_ref[h, c * KEY_CHUNK:(c + 1) * KEY_CHUNK, :]
            part = jnp.max(blk.reshape(KEY_CHUNK // SUBLANES, SUBLANES, half), axis=0)
            m8 = part if m8 is None else jnp.maximum(m8, part)
        m_ref[h] = m8


def _values_phase(vt_ref, k0, nchunks, s_ref, m_ref):
    half = s_ref.shape[2]
    maxes = [jnp.max(m_ref[h], axis=0, keepdims=True) for h in range(2)]
    accs = [jnp.zeros((LANES, half), F32) for _ in range(2)]
    for c in range(nchunks):
        vc = vt_ref[:, k0 + c * KEY_CHUNK:k0 + (c + 1) * KEY_CHUNK]
        for h in range(2):
            pr = jnp.exp2((s_ref[h, c * KEY_CHUNK:(c + 1) * KEY_CHUNK, :] - maxes[h]).astype(BF16))
            accs[h] = accs[h] + jnp.dot(vc, pr, preferred_element_type=F32)
    return jnp.concatenate(accs, axis=1)


def _attend(i, NL, NT, ntq, load_q, k_ref, vt_ref, s_scr, m_scr, a_scr, finish):
    lat = (0, NT * TILE // KEY_CHUNK)
    ctx = (NL * TILE, (NT - NL) * TILE // KEY_CHUNK)

    def case(step, score_keys, value_keys, store):
        w = step % 2
        if store:
            finish(a_scr[w][...])
        if score_keys is not None:
            _scores_phase(load_q(), k_ref, score_keys[0], score_keys[1], s_scr[w], m_scr[w])
        if value_keys is not None:
            a_scr[1 - w][...] = _values_phase(vt_ref, value_keys[0], value_keys[1], s_scr[1 - w], m_scr[1 - w])

    def at(cond, step, score_keys, value_keys, store):
        pl.when(cond)(lambda: case(step, score_keys, value_keys, store))

    at(i == 0, 0, lat, None, False)
    at(i == 1, 1, lat, lat, False)
    for parity in range(2):
        at((i >= 2) & (i < NL) & (lax.rem(i, 2) == parity), parity, lat, lat, True)
    if ntq > NL:
        at(i == NL, NL, ctx, lat, True)
        at(i == NL + 1, NL + 1, None, ctx, True)
        at(i == NL + 2, NL + 2, None, None, True)
    else:
        at(i == NL, NL, None, lat, True)
        at(i == NL + 1, NL + 1, None, None, True)


def _gqa_kernel(q_ref, k_ref, vt_ref, o_ref, s0, s1, m0, m1, a0, a1, *, NL, NT, ntq):
    group = A_HEADS // A_KV_HEADS
    load_q = lambda: q_ref[...].reshape(group * TILE, LANES)

    def finish(acc_t):
        o_t = acc_t * (1.0 / acc_t[HEAD_DIM:HEAD_DIM + 1, :])
        o_ref[...] = o_t.T.reshape(group, TILE, LANES).astype(BF16)

    _attend(pl.program_id(2), NL, NT, ntq, load_q, k_ref, vt_ref, (s0, s1), (m0, m1), (a0, a1), finish)


def _gqa_call(qa, ka, va, dims, ntq):
    B, NT, NL = dims
    T = qa.shape[1]
    S = NT * TILE
    group = A_HEADS // A_KV_HEADS
    qmap = lambda b, g, i: (g, b * NT + jnp.minimum(i, ntq - 1), 0)
    omap = lambda b, g, i: (g, b * NT + jnp.maximum(i - 2, 0), 0)
    kmap = lambda b, g, i: (g, b, 0)
    vmap = lambda b, g, i: (g, b, 0, 0)
    half = group * TILE // 2
    return pl.pallas_call(
        functools.partial(_gqa_kernel, NL=NL, NT=NT, ntq=ntq),
        grid=(B, A_KV_HEADS, ntq + 2),
        in_specs=[pl.BlockSpec((group, TILE, LANES), qmap),
                  pl.BlockSpec((None, S, LANES), kmap),
                  pl.BlockSpec((None, None, LANES, S), vmap)],
        out_specs=pl.BlockSpec((group, TILE, LANES), omap),
        out_shape=jax.ShapeDtypeStruct((A_HEADS, T, LANES), BF16),
        scratch_shapes=([pltpu.VMEM((2, S, half), F32)] * 2 + [pltpu.VMEM((2, SUBLANES, half), F32)] * 2
                        + [pltpu.VMEM((LANES, 2 * half), F32)] * 2),
        compiler_params=_cparams(("arbitrary", "arbitrary", "arbitrary")),
        name="gqa_attn",
    )(qa, ka, va)


def _diff_kernel(q_ref, k_ref, vt_ref, par_ref, o_ref, q2_scr, s0, s1, m0, m1, a0, a1, *, NL, NT, ntq):
    q = q_ref[...]
    lane = _lane_iota(q.shape)
    zero = jnp.zeros_like(q)
    q2_scr[0:TILE, :] = jnp.where(lane < D_QKDIM, q, zero)
    q2_scr[TILE:2 * TILE, :] = jnp.where(lane >= D_QKDIM, q, zero)

    def finish(acc_t):
        o1 = acc_t[:, :TILE] * (1.0 / acc_t[D_VDIM:D_VDIM + 1, :TILE])
        o2 = acc_t[:, TILE:] * (1.0 / acc_t[D_VDIM:D_VDIM + 1, TILE:])
        o = jnp.where(lane < D_VDIM, (o1 - par_ref[0:1, 0:1] * o2).T, 0.0)
        msq = jnp.sum(o * o, axis=-1, keepdims=True) * (1.0 / D_VDIM)
        o_ref[...] = (o * lax.rsqrt(msq + EPS) * par_ref[1:2, :]).astype(BF16)

    _attend(pl.program_id(2), NL, NT, ntq, lambda: q2_scr[...], k_ref, vt_ref, (s0, s1), (m0, m1), (a0, a1), finish)


def _diff_call(qd, kd, vd, dpar, dims, ntq):
    B, NT, NL = dims
    T = qd.shape[1]
    S = NT * TILE
    qmap = lambda b, h, i: (h, b * NT + jnp.minimum(i, ntq - 1), 0)
    omap = lambda b, h, i: (h, b * NT + jnp.maximum(i - 2, 0), 0)
    kmap = lambda b, h, i: (h, b, 0)
    vmap = lambda b, h, i: (h, b, 0, 0)
    return pl.pallas_call(
        functools.partial(_diff_kernel, NL=NL, NT=NT, ntq=ntq),
        grid=(B, D_HEADS, ntq + 2),
        in_specs=[pl.BlockSpec((None, TILE, LANES), qmap),
                  pl.BlockSpec((None, S, LANES), kmap),
                  pl.BlockSpec((None, None, LANES, S), vmap),
                  pl.BlockSpec((8, LANES), lambda b, h, i: (0, 0))],
        out_specs=pl.BlockSpec((None, TILE, LANES), omap),
        out_shape=jax.ShapeDtypeStruct((D_HEADS, T, LANES), BF16),
        scratch_shapes=([pltpu.VMEM((2 * TILE, LANES), BF16)] + [pltpu.VMEM((2, S, TILE), F32)] * 2
                        + [pltpu.VMEM((2, SUBLANES, TILE), F32)] * 2 + [pltpu.VMEM((LANES, 2 * TILE), F32)] * 2),
        compiler_params=_cparams(("arbitrary", "arbitrary", "arbitrary")),
        name="diff_attn",
    )(qd, kd, vd, dpar)


def _conv_kernel(z_ref, w_ref, par_ref, o_ref, zp_ref, sh_ref, *, segs):
    zeros = jnp.zeros((CONV_HALO, GROUP_W), F32)
    pos = 0
    starts = []
    for (r0, n) in segs:
        zp_ref[pos:pos + CONV_HALO, :] = zeros
        zp_ref[pos + CONV_HALO:pos + CONV_HALO + n, :] = z_ref[r0:r0 + n, :]
        starts.append(pos + CONV_HALO)
        pos += CONV_HALO + n
    zp_ref[pos:pos + CONV_HALO, :] = zeros
    bias, ln_g, ln_b = par_ref[0:1, :], par_ref[1:2, :], par_ref[2:3, :]

    for (r0, n), p0 in zip(segs, starts):
        def chunk(j, carry, r0=r0, p0=p0):
            base = pl.multiple_of(j * CONV_ROWS, CONV_ROWS)
            win = zp_ref[pl.ds(base + (p0 - CONV_HALO), CONV_ROWS + 2 * CONV_HALO), :]
            acc = jnp.zeros((CONV_ROWS, GROUP_W), F32)
            first = CONV_HALO - CONV_W // 2
            for res in range(SUBLANES):
                taps = [k for k in range(CONV_W) if (first + k) % SUBLANES == res]
                if not taps:
                    continue
                span = (first + taps[-1]) - res + CONV_ROWS
                sh_ref[res, 0:span, :] = win[res:res + span, :]
                for k in taps:
                    a = first + k - res
                    acc = acc + sh_ref[res, a:a + CONV_ROWS, :] * w_ref[k:k + 1, :]
            u = acc + bias
            mu = jnp.mean(u, axis=-1, keepdims=True)
            uc = u - mu
            var = jnp.mean(uc * uc, axis=-1, keepdims=True)
            y = uc * lax.rsqrt(var + EPS) * ln_g + ln_b
            o_ref[pl.ds(pl.multiple_of(base + r0, CONV_ROWS), CONV_ROWS), :] = (
                y * _sigmoid(y)).astype(BF16)
            return carry

        lax.fori_loop(0, n // CONV_ROWS, chunk, 0)


def _conv_call(z, conv_w, conv_par, dims, with_ctx):
    B, NT, NL = dims
    T = z.shape[0]
    S = NT * TILE
    L = NL * TILE
    segs = ((0, L), (L, S - L)) if with_ctx else ((0, L),)
    pad_rows = sum(n for _, n in segs) + CONV_HALO * (len(segs) + 1)
    return pl.pallas_call(
        functools.partial(_conv_kernel, segs=segs),
        grid=(B,),
        in_specs=[pl.BlockSpec((S, GROUP_W), lambda b: (b, 0)),
                  pl.BlockSpec((32, GROUP_W), lambda b: (0, 0)),
                  pl.BlockSpec((8, GROUP_W), lambda b: (0, 0))],
        out_specs=pl.BlockSpec((S, GROUP_W), lambda b: (b, 0)),
        out_shape=jax.ShapeDtypeStruct((T, GROUP_W), BF16),
        scratch_shapes=[pltpu.VMEM((pad_rows, GROUP_W), F32),
                        pltpu.VMEM((SUBLANES, CONV_ROWS + 2 * CONV_HALO, GROUP_W), F32)],
        compiler_params=_cparams(("arbitrary",)),
        name="conformer_conv",
    )(z, conv_w, conv_par)


def _dft_kernel(c_ref, s_ref, x_ref, cb_ref, sb_ref, o_ref, *, r0, n):
    xs = x_ref[r0:r0 + n, :]
    u = jnp.dot(c_ref[...], xs, preferred_element_type=F32).astype(BF16)
    w = jnp.dot(s_ref[...], xs, preferred_element_type=F32).astype(BF16)
    y = (jnp.dot(u, cb_ref[...], preferred_element_type=F32)
         - jnp.dot(w, sb_ref[...], preferred_element_type=F32))
    o_ref[...] = y.astype(BF16)


def _dft_call(cu, cmat, smat, cbd, sbd, dims, r0, n, prev=None):
    B, NT, NL = dims
    T = cu.shape[0]
    S = NT * TILE
    nr = n // TILE
    t0 = r0 // TILE
    args = [cmat, smat, cu, cbd, sbd]
    in_specs = [pl.BlockSpec((TILE, n), lambda r, b: (r, 0)),
                pl.BlockSpec((TILE, n), lambda r, b: (r, 0)),
                pl.BlockSpec((S, GROUP_W), lambda r, b: (b, 0)),
                pl.BlockSpec((GROUP_W, GROUP_W), lambda r, b: (0, 0)),
                pl.BlockSpec((GROUP_W, GROUP_W), lambda r, b: (0, 0))]
    kern = functools.partial(_dft_kernel, r0=r0, n=n)
    aliases = {}
    if prev is not None:
        args.append(prev)
        in_specs.append(pl.BlockSpec(memory_space=pl.ANY))
        aliases = {5: 0}
        kern = lambda c, s, x, cb, sb, _prev, o: _dft_kernel(c, s, x, cb, sb, o, r0=r0, n=n)
    return pl.pallas_call(
        kern,
        grid=(nr, B),
        in_specs=in_specs,
        out_specs=pl.BlockSpec((TILE, GROUP_W), lambda r, b: (b * NT + t0 + r, 0)),
        out_shape=jax.ShapeDtypeStruct((T, GROUP_W), BF16),
        input_output_aliases=aliases,
        compiler_params=_cparams(("arbitrary", "arbitrary")),
        name="fourier_mix",
    )(*args)


def _out_kernel(x_ref, oa_ref, ob_ref, oc_ref, od_ref, w_ref, g1_ref, a2_ref, sh2_ref, rw_ref, rb_ref,
                xn_ref, h2_ref, ti_ref, gt_ref, cnt_ref, slab_ref):
    ocat = jnp.concatenate([oa_ref[hh] for hh in range(A_HEADS)] + [ob_ref[...], oc_ref[...]]
                           + [od_ref[hh] for hh in range(D_HEADS)], axis=1)
    r = jnp.dot(ocat, w_ref[...], preferred_element_type=F32)
    xn = x_ref[...] + g1_ref[...] * r
    xn_ref[...] = xn
    ms = jnp.mean(xn * xn, axis=-1, keepdims=True)
    h2 = xn * lax.rsqrt(ms + EPS) * a2_ref[...] + sh2_ref[...]
    _rows_to_tiles(h2, h2_ref, slab_ref)
    logits = jnp.dot(h2.astype(BF16), rw_ref[...], preferred_element_type=F32) + rb_ref[...]
    lane = _lane_iota(logits.shape)
    lanef = lane.astype(F32)
    vals, idxs = [], []
    cur = logits
    for _ in range(TOP_K):
        mx = jnp.max(cur, axis=-1, keepdims=True)
        idx = jnp.min(jnp.where(cur == mx, lanef, float(LANES)), axis=-1, keepdims=True)
        vals.append(mx)
        idxs.append(idx)
        cur = jnp.where(lanef == idx, NEG_BIG * 2, cur)
    exps = [jnp.exp(v - vals[0]) for v in vals]
    inv = 1.0 / (exps[0] + exps[1] + exps[2] + exps[3])
    ti = jnp.zeros(logits.shape, F32)
    gt = jnp.zeros(logits.shape, F32)
    onehot = jnp.zeros(logits.shape, F32)
    for k in range(TOP_K):
        ti = jnp.where(lane == k, idxs[k], ti)
        gt = jnp.where(lane == k, exps[k] * inv, gt)
        onehot = onehot + jnp.where(lanef == idxs[k], 1.0, 0.0)
    ti_ref[...] = ti.astype(I32)
    gt_ref[...] = gt

    @pl.when((pl.program_id(0) == 0) & (pl.program_id(1) == 0))
    def _():
        cnt_ref[...] = jnp.zeros(cnt_ref.shape, F32)

    cnt_ref[0:1, :] += jnp.sum(onehot, axis=0, keepdims=True)


def _out_call(x_all, oa, ob, oc, od, w_out_p, g1, a2, sh2, rw, rb, dims, ntq):
    B, NT, NL = dims
    n_tok = B * ntq * TILE
    sel = lambda b, i: (jnp.where(i >= NL, B, b), 0, 0)
    row = lambda b, i: (b * NT + i, 0)
    hrow = lambda b, i: (0, b * NT + i, 0)
    crow = lambda b, i: (b * ntq + i, 0)
    const = lambda b, i: (0, 0)
    return pl.pallas_call(
        _out_kernel,
        grid=(B, ntq),
        in_specs=[pl.BlockSpec((TILE, D_MODEL), row),
                  pl.BlockSpec((A_HEADS, TILE, LANES), hrow),
                  pl.BlockSpec((TILE, GROUP_W), row),
                  pl.BlockSpec((TILE, GROUP_W), row),
                  pl.BlockSpec((D_HEADS, TILE, LANES), hrow),
                  pl.BlockSpec((OUT_K, D_MODEL), const),
                  pl.BlockSpec((None, 1, D_MODEL), sel),
                  pl.BlockSpec((None, 1, D_MODEL), sel),
                  pl.BlockSpec((None, 1, D_MODEL), sel),
                  pl.BlockSpec((D_MODEL, LANES), const),
                  pl.BlockSpec((1, LANES), const)],
        out_specs=[pl.BlockSpec((TILE, D_MODEL), crow),
                   pl.BlockSpec((TILE * ROW_SUB, LANES), crow),
                   pl.BlockSpec((TILE, LANES), crow),
                   pl.BlockSpec((TILE, LANES), crow),
                   pl.BlockSpec((8, LANES), const)],
        out_shape=[jax.ShapeDtypeStruct((n_tok, D_MODEL), F32),
                   jax.ShapeDtypeStruct((n_tok * ROW_SUB, LANES), F32),
                   jax.ShapeDtypeStruct((n_tok, LANES), I32),
                   jax.ShapeDtypeStruct((n_tok, LANES), F32),
                   jax.ShapeDtypeStruct((8, LANES), F32)],
        scratch_shapes=[_slab_scratch()],
        compiler_params=_cparams(("arbitrary", "arbitrary")),
        name="out_proj_router",
    )(x_all, oa, ob, oc, od, w_out_p, g1, a2, sh2, rw, rb)


def _rank_kernel(ti_ref, ps_ref, tri_ref, dest_ref, carry_ref):
    @pl.when(pl.program_id(0) == 0)
    def _():
        carry_ref[...] = jnp.zeros(carry_ref.shape, F32)

    ti = ti_ref[...]
    lane = _lane_iota(ti.shape)
    hits = [lane == ti[:, k:k + 1] for k in range(TOP_K)]
    onehot = jnp.zeros(ti.shape, F32)
    for hk in hits:
        onehot = onehot + jnp.where(hk, 1.0, 0.0)
    before = jnp.dot(tri_ref[...], onehot.astype(BF16), preferred_element_type=F32)
    base = before + carry_ref[0:1, :] + ps_ref[...]
    dest = jnp.zeros(ti.shape, F32)
    for k, hk in enumerate(hits):
        dk = jnp.sum(jnp.where(hk, base, 0.0), axis=-1, keepdims=True)
        dest = jnp.where(lane == k, dk, dest)
    dest_ref[...] = dest.astype(I32)
    carry_ref[0:1, :] += jnp.sum(onehot, axis=0, keepdims=True)


def _rank_call(topi, pad_start, tri):
    n_tok = topi.shape[0]
    return pl.pallas_call(
        _rank_kernel,
        grid=(n_tok // TILE,),
        in_specs=[pl.BlockSpec((TILE, LANES), lambda t: (t, 0)),
                  pl.BlockSpec((1, LANES), lambda t: (0, 0)),
                  pl.BlockSpec((TILE, TILE), lambda t: (0, 0))],
        out_specs=pl.BlockSpec((TILE, LANES), lambda t: (t, 0)),
        out_shape=jax.ShapeDtypeStruct((n_tok, LANES), I32),
        scratch_shapes=[pltpu.VMEM((8, LANES), F32)],
        compiler_params=_cparams(("arbitrary",)),
        name="slot_rank",
    )(topi, pad_start, tri)


def _tile_copy(src_ref, src_row, dst_ref, dst_row, sem):
    src = src_ref.at[pl.ds(pl.multiple_of(src_row * ROW_SUB, ROW_SUB), ROW_SUB)]
    dst = dst_ref.at[pl.ds(pl.multiple_of(dst_row * ROW_SUB, ROW_SUB), ROW_SUB)]
    return pltpu.make_async_copy(src, dst, sem)


def _dispatch_kernel(fill_start_ref, fill_on_ref, dest_ref, h2_ref, zero_ref, xs_ref, sem):
    @pl.when(pl.program_id(0) == 0)
    def _():
        def fill(e, wait):
            @pl.when(fill_on_ref[e] > 0)
            def _():
                start = pl.multiple_of(fill_start_ref[e] * ROW_SUB, MOE_BLOCK * ROW_SUB)
                cp = pltpu.make_async_copy(zero_ref, xs_ref.at[pl.ds(start, MOE_BLOCK * ROW_SUB)], sem.at[1])
                if wait:
                    cp.wait()
                else:
                    cp.start()

        lax.fori_loop(0, N_EXPERTS, lambda e, c: (fill(e, False), c)[1], 0)
        lax.fori_loop(0, N_EXPERTS, lambda e, c: (fill(e, True), c)[1], 0)

    step = pl.program_id(0)
    row0 = step * TILE

    def issue(r, c):
        for k in range(TOP_K):
            _tile_copy(h2_ref, row0 + r, xs_ref, dest_ref[0, 0, TOP_K * r + k], sem.at[0]).start(priority=k % 2)
        return c

    lax.fori_loop(0, TILE, issue, 0, unroll=DMA_UNROLL)

    def drain():
        for k in range(TOP_K):
            span = pl.ds(0, TILE * ROW_SUB)
            pltpu.make_async_copy(h2_ref.at[span], xs_ref.at[span], sem.at[0]).wait()

    pl.when(step > 0)(drain)
    pl.when(step == pl.num_programs(0) - 1)(drain)


def _dispatch_call(fill_start, fill_on, dest3, h2t, n_slots):
    n_tok = h2t.shape[0] // ROW_SUB
    zero_blk = jnp.zeros((MOE_BLOCK * ROW_SUB, LANES), F32)
    return pl.pallas_call(
        _dispatch_kernel,
        grid_spec=pltpu.PrefetchScalarGridSpec(
            num_scalar_prefetch=2,
            grid=(n_tok // TILE,),
            in_specs=[pl.BlockSpec((1, 1, TOP_K * TILE), lambda t, fs, fo: (t, 0, 0), memory_space=pltpu.SMEM),
                      pl.BlockSpec(memory_space=pl.ANY),
                      pl.BlockSpec((MOE_BLOCK * ROW_SUB, LANES), lambda t, fs, fo: (0, 0))],
            out_specs=pl.BlockSpec(memory_space=pl.ANY),
            scratch_shapes=[pltpu.SemaphoreType.DMA((2,))]),
        out_shape=jax.ShapeDtypeStruct((n_slots * ROW_SUB, LANES), F32),
        compiler_params=_cparams(("arbitrary",)),
        name="moe_dispatch",
    )(fill_start, fill_on, dest3, h2t, zero_blk)


def _expert_kernel(blk_e_ref, nact_ref, xs_ref, wgu_ref, bgu_ref, wd_ref, bd_ref, yb_ref,
                   slab_ref, wgu_bf, wd_bf):
    j = pl.program_id(0)
    active = j < nact_ref[0]

    @pl.when(active & ((j == 0) | (blk_e_ref[j] != blk_e_ref[jnp.maximum(j - 1, 0)])))
    def _():
        wgu_bf[...] = wgu_ref[...].astype(BF16)
        wd_bf[...] = wd_ref[...].astype(BF16)

    @pl.when(active)
    def _():
        x = _tiles_to_rows(xs_ref, slab_ref).astype(BF16)
        gu = jnp.dot(x, wgu_bf[...], preferred_element_type=F32) + bgu_ref[...]
        g = jnp.minimum(gu[:, :D_FF], SWIGLU_LIMIT)
        u = jnp.clip(gu[:, D_FF:], -SWIGLU_LIMIT, SWIGLU_LIMIT)
        act = g * _sigmoid(SWIGLU_ALPHA * g) * (u + 1.0)
        y = jnp.dot(act.astype(BF16), wd_bf[...], preferred_element_type=F32) + bd_ref[...]
        _rows_to_tiles(y, yb_ref, slab_ref)


def _expert_call(blk_e, n_active, xs, w_gu, b_gu, w_down, b_down, li):
    n_slots = xs.shape[0] // ROW_SUB
    n_blocks = n_slots // MOE_BLOCK
    blk = lambda j, be, na: (jnp.minimum(j, na[0] - 1), 0)
    wsel = lambda j, be, na: (be[jnp.minimum(j, na[0] - 1)], 0, 0)
    wsel4 = lambda j, be, na: (li, be[jnp.minimum(j, na[0] - 1)], 0, 0)
    return pl.pallas_call(
        _expert_kernel,
        grid_spec=pltpu.PrefetchScalarGridSpec(
            num_scalar_prefetch=2,
            grid=(n_blocks,),
            in_specs=[pl.BlockSpec((MOE_BLOCK * ROW_SUB, LANES), blk),
                      pl.BlockSpec((None, None, D_MODEL, 2 * D_FF), wsel4),
                      pl.BlockSpec((None, 1, 2 * D_FF), wsel),
                      pl.BlockSpec((None, None, D_FF, D_MODEL), wsel4),
                      pl.BlockSpec((None, 1, D_MODEL), wsel)],
            out_specs=pl.BlockSpec((MOE_BLOCK * ROW_SUB, LANES), blk),
            scratch_shapes=[_slab_scratch(), pltpu.VMEM((D_MODEL, 2 * D_FF), BF16),
                            pltpu.VMEM((D_FF, D_MODEL), BF16)]),
        out_shape=jax.ShapeDtypeStruct((n_slots * ROW_SUB, LANES), F32),
        compiler_params=_cparams(("arbitrary",)),
        name="moe_experts",
    )(blk_e, n_active, xs, w_gu, b_gu, w_down, b_down)


def _combine_kernel(dest_ref, dest_next_ref, x_ref, gt_ref, g2_ref, yb_ref, o_ref, buf0, buf1, slab_ref, sem):
    step = pl.program_id(0) * pl.num_programs(1) + pl.program_id(1)
    last = pl.num_programs(0) * pl.num_programs(1) - 1

    def gather(d_ref, buf, s):
        def issue(r, c):
            for k in range(TOP_K):
                _tile_copy(yb_ref, d_ref[0, 0, TOP_K * r + k], buf.at[k], r, s).start(priority=k % 2)
            return c

        lax.fori_loop(0, TILE, issue, 0, unroll=DMA_UNROLL)

    pl.when(step == 0)(lambda: gather(dest_ref, buf0, sem.at[0]))

    def tile(parity):
        cur, nxt = (buf0, buf1) if parity == 0 else (buf1, buf0)
        pl.when(step < last)(lambda: gather(dest_next_ref, nxt, sem.at[1 - parity]))
        for k in range(TOP_K):
            pltpu.make_async_copy(yb_ref.at[pl.ds(0, TILE * ROW_SUB)], cur.at[k], sem.at[parity]).wait()
        gt = gt_ref[...]
        y = gt[:, 0:1] * _tiles_to_rows(cur.at[0], slab_ref)
        for k in range(1, TOP_K):
            y = y + gt[:, k:k + 1] * _tiles_to_rows(cur.at[k], slab_ref)
        o_ref[...] = x_ref[...] + g2_ref[...] * y

    for parity in range(2):
        pl.when(lax.rem(step, 2) == parity)(functools.partial(tile, parity))


def _combine_call(dest3, xn, gates, g2, yb, dims, ntq):
    B, NT, NL = dims
    n_tok = xn.shape[0]
    sel = lambda b, i: (jnp.where(i >= NL, B, b), 0, 0)
    crow = lambda b, i: (b * ntq + i, 0)
    nxt = lambda b, i: (jnp.minimum(b * ntq + i + 1, B * ntq - 1), 0, 0)
    return pl.pallas_call(
        _combine_kernel,
        grid=(B, ntq),
        in_specs=[pl.BlockSpec((1, 1, TOP_K * TILE), lambda b, i: (b * ntq + i, 0, 0), memory_space=pltpu.SMEM),
                  pl.BlockSpec((1, 1, TOP_K * TILE), nxt, memory_space=pltpu.SMEM),
                  pl.BlockSpec((TILE, D_MODEL), crow),
                  pl.BlockSpec((TILE, LANES), crow),
                  pl.BlockSpec((None, 1, D_MODEL), sel),
                  pl.BlockSpec(memory_space=pl.ANY)],
        out_specs=pl.BlockSpec((TILE, D_MODEL), crow),
        out_shape=jax.ShapeDtypeStruct((n_tok, D_MODEL), F32),
        scratch_shapes=[pltpu.VMEM((TOP_K, TILE * ROW_SUB, LANES), F32),
                        pltpu.VMEM((TOP_K, TILE * ROW_SUB, LANES), F32), _slab_scratch(),
                        pltpu.SemaphoreType.DMA((2,))],
        compiler_params=_cparams(("arbitrary", "arbitrary")),
        name="moe_combine",
    )(dest3, dest3, xn, gates, g2, yb)


def _pad_heads_cols(w, n_heads, width):
    w = w.reshape(w.shape[0], n_heads, width)
    return jnp.pad(w, ((0, 0), (0, 0), (0, LANES - width))).reshape(w.shape[0], n_heads * LANES)


def _pad_heads_rows(w, n_heads, width):
    w = w.reshape(n_heads, width, w.shape[1])
    return jnp.pad(w, ((0, 0), (0, LANES - width), (0, 0))).reshape(n_heads * LANES, w.shape[2])


def _prep_w_in(w):
    splits = (256, 128, 128, 256, 256, 256, 256, 256, 256)
    parts, o = [], 0
    for s in splits:
        parts.append(w[:, o:o + s])
        o += s
    aq, ak, av, bu, bg, cu, dq, dk, dv = parts
    return jnp.concatenate([
        _pad_heads_cols(aq, A_HEADS, HEAD_DIM), _pad_heads_cols(ak, A_KV_HEADS, HEAD_DIM),
        _pad_heads_cols(av, A_KV_HEADS, HEAD_DIM), bu, bg, cu,
        _pad_heads_cols(dq, D_HEADS, 2 * D_QKDIM), _pad_heads_cols(dk, D_HEADS, 2 * D_QKDIM),
        _pad_heads_cols(dv, D_HEADS, D_VDIM)], axis=1).astype(BF16)


def _prep_w_out(w):
    return jnp.concatenate([
        _pad_heads_rows(w[0:GROUP_W], A_HEADS, HEAD_DIM), w[GROUP_W:3 * GROUP_W],
        _pad_heads_rows(w[3 * GROUP_W:], D_HEADS, D_VDIM)], axis=0).astype(BF16)


def _pad_lanes(v):
    return jnp.pad(v.astype(F32), (0, LANES - v.shape[0]))


def _rope_tables(L, C):
    t = jnp.arange(L, dtype=jnp.int32)
    rows = (t // GRID_W).astype(F32)[:, None]
    cols = (t % GRID_W).astype(F32)[:, None]
    lane = jnp.arange(LANES)

    def table(group):
        nfreq = group // 2
        j = lane % nfreq
        inv = jnp.power(ROPE_THETA, -(2.0 * j.astype(F32)) / group)[None, :]
        use_cols = ((lane // group) % 2) == 1
        ang = jnp.where(use_cols[None, :], cols, rows) * inv
        sign = jnp.where((lane % group) < nfreq, -1.0, 1.0)[None, :]
        live = (lane < 2 * D_QKDIM)[None, :]
        cos = jnp.where(live, jnp.cos(ang), 1.0)
        sin = jnp.where(live, jnp.sin(ang) * sign, 0.0)
        cos = jnp.concatenate([cos, jnp.ones((C, LANES), F32)], axis=0)
        sin = jnp.concatenate([sin, jnp.zeros((C, LANES), F32)], axis=0)
        return cos, sin

    ca, sa = table(HEAD_DIM // 2)
    cd, sd = table(D_QKDIM // 2)
    return ca, sa, cd, sd


def _dft_mats(n):
    r = 1 << (int(math.log2(n)) // 2)
    assert n % r == 0
    k = jnp.arange(n, dtype=jnp.int32)[None, :]

    def table(rows, period):
        idx = jnp.arange(rows, dtype=jnp.int32)[:, None]
        ang = ((idx * k) % period).astype(F32) * (2.0 * math.pi / period)
        return jnp.cos(ang), jnp.sin(ang)

    c1, s1 = table(n // r, n // r)
    c2, s2 = table(r, n)
    c1, s1, c2, s2 = c1[:, None, :], s1[:, None, :], c2[None, :, :], s2[None, :, :]
    scale = 1.0 / math.sqrt(n)
    cos = ((c1 * c2 - s1 * s2) * scale).reshape(n, n).astype(BF16)
    sin = ((s1 * c2 + c1 * s2) * scale).reshape(n, n).astype(BF16)
    return cos, sin


def _channel_dft_mats():
    w = GROUP_W // FNET_GROUPS
    k = jnp.arange(GROUP_W, dtype=jnp.int32)
    same = (k[:, None] // w) == (k[None, :] // w)
    prod = ((k[:, None] % w) * (k[None, :] % w)) % w
    ang = prod.astype(F32) * (2.0 * math.pi / w)
    scale = 1.0 / math.sqrt(w)
    return (jnp.where(same, jnp.cos(ang) * scale, 0.0).astype(BF16),
            jnp.where(same, jnp.sin(ang) * scale, 0.0).astype(BF16))


def _slot_plan(counts, n_blocks):
    cnt = counts[0, :N_EXPERTS].astype(I32)
    padded = (cnt + MOE_BLOCK - 1) // MOE_BLOCK * MOE_BLOCK
    pad_ends = jnp.cumsum(padded)
    pad_starts = pad_ends - padded
    n_active = (pad_ends[-1] // MOE_BLOCK).reshape(1).astype(I32)
    blk_start = jnp.arange(n_blocks, dtype=I32) * MOE_BLOCK
    blk_e = jnp.minimum(jnp.sum(pad_ends[None, :] <= blk_start[:, None], axis=1), N_EXPERTS - 1).astype(I32)
    fill_start = jnp.maximum(pad_ends - MOE_BLOCK, 0).astype(I32)
    fill_on = (cnt > 0).astype(I32)
    ps_vec = jnp.pad(pad_starts.astype(F32), (0, LANES - N_EXPERTS)).reshape(1, LANES)
    return ps_vec, blk_e, n_active, fill_start, fill_on


def kernel(x, c, ctx, c_ctx, norm1_g, norm2_g, ada_w, ada_b, w_in, a_qnorm_g, a_knorm_g, conv_dw_w, conv_dw_b,
           conv_ln_g, conv_ln_b, d_qnorm_g, d_knorm_g, d_lambda, d_subln_g, w_out, router_w, router_b,
           exp_w_gu, exp_b_gu, exp_w_down, exp_b_down):
    B, L, D = x.shape
    C = ctx.shape[1]
    depth = ada_w.shape[0]
    assert D == D_MODEL and L % TILE == 0 and C == TILE and L % GRID_W == 0
    NL, NT = L // TILE, (L + C) // TILE
    dims = (B, NT, NL)
    S = NT * TILE

    x_all = jnp.concatenate([x, ctx], axis=1).reshape(B * S, D)

    a = jnp.concatenate([c, c_ctx[None, :]], axis=0)
    a = jnp.pad(a * jax.nn.sigmoid(a), ((0, 16 - (B + 1) % 16 if (B + 1) % 16 else 0), (0, 0)))
    mod_all = _ada_call(a, ada_w, ada_b)

    rope = _rope_tables(L, C)
    cmat, smat = _dft_mats(L)
    cmat_c, smat_c = _dft_mats(C)
    cbd, sbd = _channel_dft_mats()
    tri = jnp.tril(jnp.ones((TILE, TILE), F32), -1).astype(BF16)

    x_lat = None
    for li in range(depth):
        last = li == depth - 1
        ntq = NL if last else NT
        mod = mod_all[li, :B + 1].reshape(B + 1, 6, 1, D)
        sh1, sc1, g1, sh2, sc2, g2 = (mod[:, j] for j in range(6))
        a1 = norm1_g[li][None, None, :] * (1.0 + sc1)
        a2 = norm2_g[li][None, None, :] * (1.0 + sc2)

        gains = jnp.stack([
            _pad_lanes(a_qnorm_g[li]) * (HEAD_DIM ** -0.5 * LOG2E), _pad_lanes(a_knorm_g[li]),
            _pad_lanes(jnp.tile(d_qnorm_g[li], 2)) * (D_QKDIM ** -0.5 * LOG2E), _pad_lanes(jnp.tile(d_knorm_g[li], 2)),
        ] + [jnp.zeros((LANES,), F32)] * 4)
        qa, ka, va, z, cu, qd, kd, vd = _proj_call(x_all, a1, sh1, _prep_w_in(w_in[li]), gains, rope, dims)

        lam_init = 0.8 - 0.6 * math.exp(-0.3 * li)
        lq1, lk1, lq2, lk2 = d_lambda[li].astype(F32)
        lam = jnp.exp(jnp.sum(lq1 * lk1)) - jnp.exp(jnp.sum(lq2 * lk2)) + lam_init
        dpar = jnp.stack([jnp.full((LANES,), lam, F32), _pad_lanes(d_subln_g[li]) * (1.0 - lam_init)]
                         + [jnp.zeros((LANES,), F32)] * 6)

        oa = _gqa_call(qa, ka, va, dims, ntq)
        od = _diff_call(qd, kd, vd, dpar, dims, ntq)
        conv_w = jnp.pad(conv_dw_w[li], ((0, 32 - CONV_W), (0, 0)))
        conv_par = jnp.stack([conv_dw_b[li], conv_ln_g[li], conv_ln_b[li]] + [jnp.zeros((GROUP_W,), F32)] * 5)
        ob = _conv_call(z, conv_w, conv_par, dims, with_ctx=not last)
        oc = _dft_call(cu, cmat, smat, cbd, sbd, dims, 0, L)
        if not last:
            oc = _dft_call(cu, cmat_c, smat_c, cbd, sbd, dims, L, C, prev=oc)

        rw = jnp.pad(router_w[li], ((0, 0), (0, LANES - N_EXPERTS))).astype(BF16)
        rb = jnp.concatenate([router_b[li].astype(F32), jnp.full((LANES - N_EXPERTS,), NEG_BIG, F32)]).reshape(1, LANES)
        xn, h2t, topi, gates, counts = _out_call(x_all, oa, ob, oc, od, _prep_w_out(w_out[li]), g1, a2, sh2,
                                                 rw, rb, dims, ntq)

        n_tok = B * ntq * TILE
        n_blocks = -(-(n_tok * TOP_K + N_EXPERTS * (MOE_BLOCK - 1)) // MOE_BLOCK)
        ps_vec, blk_e, n_active, fill_start, fill_on = _slot_plan(counts, n_blocks)
        dest = _rank_call(topi, ps_vec, tri)
        dest3 = dest[:, :TOP_K].reshape(n_tok // TILE, 1, TOP_K * TILE)
        xs = _dispatch_call(fill_start, fill_on, dest3, h2t, n_blocks * MOE_BLOCK)
        yb = _expert_call(blk_e, n_active, xs, exp_w_gu, exp_b_gu[li][:, None, :],
                          exp_w_down, exp_b_down[li][:, None, :], li)
        x_next = _combine_call(dest3, xn, gates, g2, yb, dims, ntq)
        if last:
            x_lat = x_next
        else:
            x_all = x_next
    return x_lat.reshape(B, L, D)
```

```python
import functools
import math

import jax
import jax.numpy as jnp
from jax import lax
from jax.experimental import pallas as pl
from jax.experimental.pallas import tpu as pltpu

F32 = jnp.float32
BF16 = jnp.bfloat16
I32 = jnp.int32

D_MODEL = 1024
TILE = 256
LANES = 128
GRID_W = 64
HEAD_DIM = 64
A_HEADS = 4
A_KV_HEADS = 2
CONV_W = 31
CONV_HALO = 16
CONV_ROWS = 128
FNET_GROUPS = 4
D_HEADS = 4
D_QKDIM = 32
D_VDIM = 64
GROUP_W = 256
N_EXPERTS = 32
TOP_K = 4
D_FF = D_MODEL
SWIGLU_LIMIT = 7.0
SWIGLU_ALPHA = 1.702
MOE_BLOCK = 256
KEY_CHUNK = 256
DMA_UNROLL = 8
ROPE_THETA = 10000.0
EPS = 1e-6
NEG_BIG = -1e30
LOG2E = math.log2(math.e)
SUBLANES = 8
ROW_SUB = D_MODEL // LANES
assert ROW_SUB == SUBLANES
ROW_TILE = TILE
ROW_PITCH = ROW_TILE + SUBLANES

PROJ_W = (A_HEADS * LANES, A_KV_HEADS * LANES, A_KV_HEADS * LANES, GROUP_W, GROUP_W, GROUP_W,
          D_HEADS * LANES, D_HEADS * LANES, D_HEADS * LANES)
PROJ_OFF = tuple(sum(PROJ_W[:i]) for i in range(len(PROJ_W) + 1))
OUT_K = A_HEADS * LANES + 2 * GROUP_W + D_HEADS * LANES

VMEM_LIMIT = 52 * 1024 * 1024


def _cparams(sem):
    return pltpu.CompilerParams(dimension_semantics=sem, vmem_limit_bytes=VMEM_LIMIT)


def _lane_iota(shape):
    return lax.broadcasted_iota(I32, shape, len(shape) - 1)


def _sigmoid(x):
    return 0.5 * jnp.tanh(0.5 * x) + 0.5


def _rows_to_tiles(h, out_ref, slab_ref):
    for j in range(ROW_SUB):
        slab_ref[j * ROW_PITCH:j * ROW_PITCH + ROW_TILE, :] = h[:, j * LANES:(j + 1) * LANES]
    for t in range(ROW_TILE):
        out_ref[ROW_SUB * t:ROW_SUB * (t + 1), :] = slab_ref[pl.ds(t, ROW_SUB, stride=ROW_PITCH), :]


def _tiles_to_rows(in_ref, slab_ref):
    for t in range(ROW_TILE):
        slab_ref[pl.ds(t, ROW_SUB, stride=ROW_PITCH), :] = in_ref[ROW_SUB * t:ROW_SUB * (t + 1), :]
    return jnp.concatenate([slab_ref[j * ROW_PITCH:j * ROW_PITCH + ROW_TILE, :] for j in range(ROW_SUB)], axis=1)


def _slab_scratch():
    return pltpu.VMEM((ROW_SUB * ROW_PITCH, LANES), F32)


def _ada_kernel(a_ref, w_ref, b_ref, o_ref):
    o_ref[...] = jnp.dot(a_ref[...], w_ref[...], preferred_element_type=F32,
                         precision=lax.Precision.HIGHEST) + b_ref[...]


def _ada_call(a, ada_w, ada_b):
    depth, d, n = ada_w.shape
    rows = a.shape[0]
    nb = n // d
    return pl.pallas_call(
        _ada_kernel,
        grid=(depth, nb),
        in_specs=[pl.BlockSpec((rows, d), lambda l, j: (0, 0)),
                  pl.BlockSpec((None, d, d), lambda l, j: (l, 0, j)),
                  pl.BlockSpec((None, 1, d), lambda l, j: (l, 0, j))],
        out_specs=pl.BlockSpec((None, rows, d), lambda l, j: (l, 0, j)),
        out_shape=jax.ShapeDtypeStruct((depth, rows, n), F32),
        compiler_params=_cparams(("arbitrary", "arbitrary")),
        name="ada_mod",
    )(a, ada_w, ada_b.reshape(depth, 1, n))


def _swap_halves(x, half):
    lane = _lane_iota(x.shape)
    first = (lane & (2 * half - 1)) < half
    return jnp.where(first, pltpu.roll(x, LANES - half, 1), pltpu.roll(x, half, 1))


def _proj_kernel(x_ref, a_ref, sh_ref, w_ref, gains_ref, ca_ref, sa_ref, cd_ref, sd_ref,
                 qa_ref, ka_ref, va_ref, z_ref, cu_ref, qd_ref, kd_ref, vd_ref):
    x = x_ref[...]
    ms = jnp.mean(x * x, axis=-1, keepdims=True)
    h = (x * lax.rsqrt(ms + EPS) * a_ref[...] + sh_ref[...]).astype(BF16)
    p = jnp.dot(h, w_ref[...], preferred_element_type=F32)
    lane = _lane_iota((TILE, LANES))
    ca, sa, cd, sd = ca_ref[...], sa_ref[...], cd_ref[...], sd_ref[...]

    def slab(sec, i):
        o = PROJ_OFF[sec] + i * LANES
        return p[:, o:o + LANES]

    def norm_rope_a(s, gain):
        msq = jnp.sum(s * s, axis=-1, keepdims=True) * (1.0 / HEAD_DIM)
        sn = s * lax.rsqrt(msq + EPS) * gain
        return (sn * ca + _swap_halves(sn, HEAD_DIM // 4) * sa).astype(BF16)

    def norm_rope_d(s, gain):
        s2 = s * s
        m1 = jnp.sum(jnp.where(lane < D_QKDIM, s2, 0.0), axis=-1, keepdims=True)
        m2 = jnp.sum(jnp.where(lane >= D_QKDIM, s2, 0.0), axis=-1, keepdims=True)
        msq = jnp.where(lane < D_QKDIM, m1, m2) * (1.0 / D_QKDIM)
        sn = s * lax.rsqrt(msq + EPS) * gain
        return (sn * cd + _swap_halves(sn, D_QKDIM // 4) * sd).astype(BF16)

    def with_ones(s, width):
        st = s.T
        chan = lax.broadcasted_iota(I32, st.shape, 0)
        return jnp.where(chan == width, 1.0, st).astype(BF16)

    for hh in range(A_HEADS):
        qa_ref[hh] = norm_rope_a(slab(0, hh), gains_ref[0:1, :])
    for hh in range(A_KV_HEADS):
        ka_ref[hh] = norm_rope_a(slab(1, hh), gains_ref[1:2, :])
        va_ref[hh] = with_ones(slab(2, hh), HEAD_DIM)
    bu = p[:, PROJ_OFF[3]:PROJ_OFF[4]]
    bg = p[:, PROJ_OFF[4]:PROJ_OFF[5]]
    z_ref[...] = bu * _sigmoid(bg)
    cu_ref[...] = p[:, PROJ_OFF[5]:PROJ_OFF[6]].astype(BF16)
    for hh in range(D_HEADS):
        qd_ref[hh] = norm_rope_d(slab(6, hh), gains_ref[2:3, :])
        kd_ref[hh] = norm_rope_d(slab(7, hh), gains_ref[3:4, :])
        vd_ref[hh] = with_ones(slab(8, hh), D_VDIM)


def _proj_call(x_all, a1, sh1, w_in_p, gains, rope, dims):
    B, NT, NL = dims
    T = x_all.shape[0]
    S = NT * TILE
    sel = lambda b, i: (jnp.where(i >= NL, B, b), 0, 0)
    row = lambda b, i: (b * NT + i, 0)
    hrow = lambda b, i: (0, b * NT + i, 0)
    tab = lambda b, i: (i, 0)
    head = lambda n: pl.BlockSpec((n, TILE, LANES), hrow)
    hshape = lambda n: jax.ShapeDtypeStruct((n, T, LANES), BF16)
    head_t = lambda n: pl.BlockSpec((n, None, LANES, TILE), lambda b, i: (0, b, 0, i))
    hshape_t = lambda n: jax.ShapeDtypeStruct((n, B, LANES, S), BF16)
    nw = w_in_p.shape[1]
    return pl.pallas_call(
        _proj_kernel,
        grid=(B, NT),
        in_specs=[pl.BlockSpec((TILE, D_MODEL), row),
                  pl.BlockSpec((None, 1, D_MODEL), sel),
                  pl.BlockSpec((None, 1, D_MODEL), sel),
                  pl.BlockSpec((D_MODEL, nw), lambda b, i: (0, 0)),
                  pl.BlockSpec((8, LANES), lambda b, i: (0, 0)),
                  pl.BlockSpec((TILE, LANES), tab), pl.BlockSpec((TILE, LANES), tab),
                  pl.BlockSpec((TILE, LANES), tab), pl.BlockSpec((TILE, LANES), tab)],
        out_specs=[head(A_HEADS), head(A_KV_HEADS), head_t(A_KV_HEADS),
                   pl.BlockSpec((TILE, GROUP_W), row), pl.BlockSpec((TILE, GROUP_W), row),
                   head(D_HEADS), head(D_HEADS), head_t(D_HEADS)],
        out_shape=[hshape(A_HEADS), hshape(A_KV_HEADS), hshape_t(A_KV_HEADS),
                   jax.ShapeDtypeStruct((T, GROUP_W), F32), jax.ShapeDtypeStruct((T, GROUP_W), BF16),
                   hshape(D_HEADS), hshape(D_HEADS), hshape_t(D_HEADS)],
        compiler_params=_cparams(("arbitrary", "arbitrary")),
        name="in_proj",
    )(x_all, a1, sh1, w_in_p, gains, *rope)


def _scores_phase(q, k_ref, k0, nchunks, s_ref, m_ref):
    half = q.shape[0] // 2
    nk = nchunks * KEY_CHUNK
    for h in range(2):
        s_ref[h, 0:nk, :] = lax.dot_general(k_ref[k0:k0 + nk, :], q[h * half:(h + 1) * half],
                                            (((1,), (1,)), ((), ())), preferred_element_type=F32)
    for h in range(2):
        m8 = None
        for c in range(nchunks):
            blk = s_ref[h, c * KEY_CHUNK:(c + 1) * KEY_CHUNK, :]
            part = jnp.max(blk.reshape(KEY_CHUNK // SUBLANES, SUBLANES, half), axis=0)
            m8 = part if m8 is None else jnp.maximum(m8, part)
        m_ref[h] = m8


def _values_phase(vt_ref, k0, nchunks, s_ref, m_ref):
    half = s_ref.shape[2]
    maxes = [jnp.max(m_ref[h], axis=0, keepdims=True) for h in range(2)]
    accs = [jnp.zeros((LANES, half), F32) for _ in range(2)]
    for c in range(nchunks):
        vc = vt_ref[:, k0 + c * KEY_CHUNK:k0 + (c + 1) * KEY_CHUNK]
        for h in range(2):
            pr = jnp.exp2((s_ref[h, c * KEY_CHUNK:(c + 1) * KEY_CHUNK, :] - maxes[h]).astype(BF16))
            accs[h] = accs[h] + jnp.dot(vc, pr, preferred_element_type=F32)
    return jnp.concatenate(accs, axis=1)


def _attend(i, NL, NT, ntq, load_q, k_ref, vt_ref, s_scr, m_scr, a_scr, finish):
    lat = (0, NT * TILE // KEY_CHUNK)
    ctx = (NL * TILE, (NT - NL) * TILE // KEY_CHUNK)

    def case(step, score_keys, value_keys, store):
        w = step % 2
        if store:
            finish(a_scr[w][...])
        if score_keys is not None:
            _scores_phase(load_q(), k_ref, score_keys[0], score_keys[1], s_scr[w], m_scr[w])
        if value_keys is not None:
            a_scr[1 - w][...] = _values_phase(vt_ref, value_keys[0], value_keys[1], s_scr[1 - w], m_scr[1 - w])

    def at(cond, step, score_keys, value_keys, store):
        pl.when(cond)(lambda: case(step, score_keys, value_keys, store))

    at(i == 0, 0, lat, None, False)
    at(i == 1, 1, lat, lat, False)
    for parity in range(2):
        at((i >= 2) & (i < NL) & (lax.rem(i, 2) == parity), parity, lat, lat, True)
    if ntq > NL:
        at(i == NL, NL, ctx, lat, True)
        at(i == NL + 1, NL + 1, None, ctx, True)
        at(i == NL + 2, NL + 2, None, None, True)
    else:
        at(i == NL, NL, None, lat, True)
        at(i == NL + 1, NL + 1, None, None, True)


def _gqa_kernel(q_ref, k_ref, vt_ref, o_ref, s0, s1, m0, m1, a0, a1, *, NL, NT, ntq):
    group = A_HEADS // A_KV_HEADS
    load_q = lambda: q_ref[...].reshape(group * TILE, LANES)

    def finish(acc_t):
        o_t = acc_t * (1.0 / acc_t[HEAD_DIM:HEAD_DIM + 1, :])
        o_ref[...] = o_t.T.reshape(group, TILE, LANES).astype(BF16)

    _attend(pl.program_id(2), NL, NT, ntq, load_q, k_ref, vt_ref, (s0, s1), (m0, m1), (a0, a1), finish)


def _gqa_call(qa, ka, va, dims, ntq):
    B, NT, NL = dims
    T = qa.shape[1]
    S = NT * TILE
    group = A_HEADS // A_KV_HEADS
    qmap = lambda b, g, i: (g, b * NT + jnp.minimum(i, ntq - 1), 0)
    omap = lambda b, g, i: (g, b * NT + jnp.maximum(i - 2, 0), 0)
    kmap = lambda b, g, i: (g, b, 0)
    vmap = lambda b, g, i: (g, b, 0, 0)
    half = group * TILE // 2
    return pl.pallas_call(
        functools.partial(_gqa_kernel, NL=NL, NT=NT, ntq=ntq),
        grid=(B, A_KV_HEADS, ntq + 2),
        in_specs=[pl.BlockSpec((group, TILE, LANES), qmap),
                  pl.BlockSpec((None, S, LANES), kmap),
                  pl.BlockSpec((None, None, LANES, S), vmap)],
        out_specs=pl.BlockSpec((group, TILE, LANES), omap),
        out_shape=jax.ShapeDtypeStruct((A_HEADS, T, LANES), BF16),
        scratch_shapes=([pltpu.VMEM((2, S, half), F32)] * 2 + [pltpu.VMEM((2, SUBLANES, half), F32)] * 2
                        + [pltpu.VMEM((LANES, 2 * half), F32)] * 2),
        compiler_params=_cparams(("arbitrary", "arbitrary", "arbitrary")),
        name="gqa_attn",
    )(qa, ka, va)


def _diff_kernel(q_ref, k_ref, vt_ref, par_ref, o_ref, q2_scr, s0, s1, m0, m1, a0, a1, *, NL, NT, ntq):
    q = q_ref[...]
    lane = _lane_iota(q.shape)
    zero = jnp.zeros_like(q)
    q2_scr[0:TILE, :] = jnp.where(lane < D_QKDIM, q, zero)
    q2_scr[TILE:2 * TILE, :] = jnp.where(lane >= D_QKDIM, q, zero)

    def finish(acc_t):
        o1 = acc_t[:, :TILE] * (1.0 / acc_t[D_VDIM:D_VDIM + 1, :TILE])
        o2 = acc_t[:, TILE:] * (1.0 / acc_t[D_VDIM:D_VDIM + 1, TILE:])
        o = jnp.where(lane < D_VDIM, (o1 - par_ref[0:1, 0:1] * o2).T, 0.0)
        msq = jnp.sum(o * o, axis=-1, keepdims=True) * (1.0 / D_VDIM)
        o_ref[...] = (o * lax.rsqrt(msq + EPS) * par_ref[1:2, :]).astype(BF16)

    _attend(pl.program_id(2), NL, NT, ntq, lambda: q2_scr[...], k_ref, vt_ref, (s0, s1), (m0, m1), (a0, a1), finish)


def _diff_call(qd, kd, vd, dpar, dims, ntq):
    B, NT, NL = dims
    T = qd.shape[1]
    S = NT * TILE
    qmap = lambda b, h, i: (h, b * NT + jnp.minimum(i, ntq - 1), 0)
    omap = lambda b, h, i: (h, b * NT + jnp.maximum(i - 2, 0), 0)
    kmap = lambda b, h, i: (h, b, 0)
    vmap = lambda b, h, i: (h, b, 0, 0)
    return pl.pallas_call(
        functools.partial(_diff_kernel, NL=NL, NT=NT, ntq=ntq),
        grid=(B, D_HEADS, ntq + 2),
        in_specs=[pl.BlockSpec((None, TILE, LANES), qmap),
                  pl.BlockSpec((None, S, LANES), kmap),
                  pl.BlockSpec((None, None, LANES, S), vmap),
                  pl.BlockSpec((8, LANES), lambda b, h, i: (0, 0))],
        out_specs=pl.BlockSpec((None, TILE, LANES), omap),
        out_shape=jax.ShapeDtypeStruct((D_HEADS, T, LANES), BF16),
        scratch_shapes=([pltpu.VMEM((2 * TILE, LANES), BF16)] + [pltpu.VMEM((2, S, TILE), F32)] * 2
                        + [pltpu.VMEM((2, SUBLANES, TILE), F32)] * 2 + [pltpu.VMEM((LANES, 2 * TILE), F32)] * 2),
        compiler_params=_cparams(("arbitrary", "arbitrary", "arbitrary")),
        name="diff_attn",
    )(qd, kd, vd, dpar)


def _conv_kernel(z_ref, w_ref, par_ref, o_ref, zp_ref, sh_ref, *, segs):
    zeros = jnp.zeros((CONV_HALO, GROUP_W), F32)
    pos = 0
    starts = []
    for (r0, n) in segs:
        zp_ref[pos:pos + CONV_HALO, :] = zeros
        zp_ref[pos + CONV_HALO:pos + CONV_HALO + n, :] = z_ref[r0:r0 + n, :]
        starts.append(pos + CONV_HALO)
        pos += CONV_HALO + n
    zp_ref[pos:pos + CONV_HALO, :] = zeros
    bias, ln_g, ln_b = par_ref[0:1, :], par_ref[1:2, :], par_ref[2:3, :]

    for (r0, n), p0 in zip(segs, starts):
        def chunk(j, carry, r0=r0, p0=p0):
            base = pl.multiple_of(j * CONV_ROWS, CONV_ROWS)
            win = zp_ref[pl.ds(base + (p0 - CONV_HALO), CONV_ROWS + 2 * CONV_HALO), :]
            acc = jnp.zeros((CONV_ROWS, GROUP_W), F32)
            first = CONV_HALO - CONV_W // 2
            for res in range(SUBLANES):
                taps = [k for k in range(CONV_W) if (first + k) % SUBLANES == res]
                if not taps:
                    continue
                span = (first + taps[-1]) - res + CONV_ROWS
                sh_ref[res, 0:span, :] = win[res:res + span, :]
                for k in taps:
                    a = first + k - res
                    acc = acc + sh_ref[res, a:a + CONV_ROWS, :] * w_ref[k:k + 1, :]
            u = acc + bias
            mu = jnp.mean(u, axis=-1, keepdims=True)
            uc = u - mu
            var = jnp.mean(uc * uc, axis=-1, keepdims=True)
            y = uc * lax.rsqrt(var + EPS) * ln_g + ln_b
            o_ref[pl.ds(pl.multiple_of(base + r0, CONV_ROWS), CONV_ROWS), :] = (
                y * _sigmoid(y)).astype(BF16)
            return carry

        lax.fori_loop(0, n // CONV_ROWS, chunk, 0)


def _conv_call(z, conv_w, conv_par, dims, with_ctx):
    B, NT, NL = dims
    T = z.shape[0]
    S = NT * TILE
    L = NL * TILE
    segs = ((0, L), (L, S - L)) if with_ctx else ((0, L),)
    pad_rows = sum(n for _, n in segs) + CONV_HALO * (len(segs) + 1)
    return pl.pallas_call(
        functools.partial(_conv_kernel, segs=segs),
        grid=(B,),
        in_specs=[pl.BlockSpec((S, GROUP_W), lambda b: (b, 0)),
                  pl.BlockSpec((32, GROUP_W), lambda b: (0, 0)),
                  pl.BlockSpec((8, GROUP_W), lambda b: (0, 0))],
        out_specs=pl.BlockSpec((S, GROUP_W), lambda b: (b, 0)),
        out_shape=jax.ShapeDtypeStruct((T, GROUP_W), BF16),
        scratch_shapes=[pltpu.VMEM((pad_rows, GROUP_W), F32),
                        pltpu.VMEM((SUBLANES, CONV_ROWS + 2 * CONV_HALO, GROUP_W), F32)],
        compiler_params=_cparams(("arbitrary",)),
        name="conformer_conv",
    )(z, conv_w, conv_par)


def _dft_kernel(c_ref, s_ref, x_ref, cb_ref, sb_ref, o_ref, *, r0, n):
    xs = x_ref[r0:r0 + n, :]
    u = jnp.dot(c_ref[...], xs, preferred_element_type=F32).astype(BF16)
    w = jnp.dot(s_ref[...], xs, preferred_element_type=F32).astype(BF16)
    y = (jnp.dot(u, cb_ref[...], preferred_element_type=F32)
         - jnp.dot(w, sb_ref[...], preferred_element_type=F32))
    o_ref[...] = y.astype(BF16)


def _dft_call(cu, cmat, smat, cbd, sbd, dims, r0, n, prev=None):
    B, NT, NL = dims
    T = cu.shape[0]
    S = NT * TILE
    nr = n // TILE
    t0 = r0 // TILE
    args = [cmat, smat, cu, cbd, sbd]
    in_specs = [pl.BlockSpec((TILE, n), lambda r, b: (r, 0)),
                pl.BlockSpec((TILE, n), lambda r, b: (r, 0)),
                pl.BlockSpec((S, GROUP_W), lambda r, b: (b, 0)),
                pl.BlockSpec((GROUP_W, GROUP_W), lambda r, b: (0, 0)),
                pl.BlockSpec((GROUP_W, GROUP_W), lambda r, b: (0, 0))]
    kern = functools.partial(_dft_kernel, r0=r0, n=n)
    aliases = {}
    if prev is not None:
        args.append(prev)
        in_specs.append(pl.BlockSpec(memory_space=pl.ANY))
        aliases = {5: 0}
        kern = lambda c, s, x, cb, sb, _prev, o: _dft_kernel(c, s, x, cb, sb, o, r0=r0, n=n)
    return pl.pallas_call(
        kern,
        grid=(nr, B),
        in_specs=in_specs,
        out_specs=pl.BlockSpec((TILE, GROUP_W), lambda r, b: (b * NT + t0 + r, 0)),
        out_shape=jax.ShapeDtypeStruct((T, GROUP_W), BF16),
        input_output_aliases=aliases,
        compiler_params=_cparams(("arbitrary", "arbitrary")),
        name="fourier_mix",
    )(*args)


def _out_kernel(x_ref, oa_ref, ob_ref, oc_ref, od_ref, w_ref, g1_ref, a2_ref, sh2_ref, rw_ref, rb_ref,
                xn_ref, h2_ref, ti_ref, gt_ref, cnt_ref, slab_ref):
    ocat = jnp.concatenate([oa_ref[hh] for hh in range(A_HEADS)] + [ob_ref[...], oc_ref[...]]
                           + [od_ref[hh] for hh in range(D_HEADS)], axis=1)
    r = jnp.dot(ocat, w_ref[...], preferred_element_type=F32)
    xn = x_ref[...] + g1_ref[...] * r
    xn_ref[...] = xn
    ms = jnp.mean(xn * xn, axis=-1, keepdims=True)
    h2 = xn * lax.rsqrt(ms + EPS) * a2_ref[...] + sh2_ref[...]
    _rows_to_tiles(h2, h2_ref, slab_ref)
    logits = jnp.dot(h2.astype(BF16), rw_ref[...], preferred_element_type=F32) + rb_ref[...]
    lane = _lane_iota(logits.shape)
    lanef = lane.astype(F32)
    vals, idxs = [], []
    cur = logits
    for _ in range(TOP_K):
        mx = jnp.max(cur, axis=-1, keepdims=True)
        idx = jnp.min(jnp.where(cur == mx, lanef, float(LANES)), axis=-1, keepdims=True)
        vals.append(mx)
        idxs.append(idx)
        cur = jnp.where(lanef == idx, NEG_BIG * 2, cur)
    exps = [jnp.exp(v - vals[0]) for v in vals]
    inv = 1.0 / (exps[0] + exps[1] + exps[2] + exps[3])
    ti = jnp.zeros(logits.shape, F32)
    gt = jnp.zeros(logits.shape, F32)
    onehot = jnp.zeros(logits.shape, F32)
    for k in range(TOP_K):
        ti = jnp.where(lane == k, idxs[k], ti)
        gt = jnp.where(lane == k, exps[k] * inv, gt)
        onehot = onehot + jnp.where(lanef == idxs[k], 1.0, 0.0)
    ti_ref[...] = ti.astype(I32)
    gt_ref[...] = gt

    @pl.when((pl.program_id(0) == 0) & (pl.program_id(1) == 0))
    def _():
        cnt_ref[...] = jnp.zeros(cnt_ref.shape, F32)

    cnt_ref[0:1, :] += jnp.sum(onehot, axis=0, keepdims=True)


def _out_call(x_all, oa, ob, oc, od, w_out_p, g1, a2, sh2, rw, rb, dims, ntq):
    B, NT, NL = dims
    n_tok = B * ntq * TILE
    sel = lambda b, i: (jnp.where(i >= NL, B, b), 0, 0)
    row = lambda b, i: (b * NT + i, 0)
    hrow = lambda b, i: (0, b * NT + i, 0)
    crow = lambda b, i: (b * ntq + i, 0)
    const = lambda b, i: (0, 0)
    return pl.pallas_call(
        _out_kernel,
        grid=(B, ntq),
        in_specs=[pl.BlockSpec((TILE, D_MODEL), row),
                  pl.BlockSpec((A_HEADS, TILE, LANES), hrow),
                  pl.BlockSpec((TILE, GROUP_W), row),
                  pl.BlockSpec((TILE, GROUP_W), row),
                  pl.BlockSpec((D_HEADS, TILE, LANES), hrow),
                  pl.BlockSpec((OUT_K, D_MODEL), const),
                  pl.BlockSpec((None, 1, D_MODEL), sel),
                  pl.BlockSpec((None, 1, D_MODEL), sel),
                  pl.BlockSpec((None, 1, D_MODEL), sel),
                  pl.BlockSpec((D_MODEL, LANES), const),
                  pl.BlockSpec((1, LANES), const)],
        out_specs=[pl.BlockSpec((TILE, D_MODEL), crow),
                   pl.BlockSpec((TILE * ROW_SUB, LANES), crow),
                   pl.BlockSpec((TILE, LANES), crow),
                   pl.BlockSpec((TILE, LANES), crow),
                   pl.BlockSpec((8, LANES), const)],
        out_shape=[jax.ShapeDtypeStruct((n_tok, D_MODEL), F32),
                   jax.ShapeDtypeStruct((n_tok * ROW_SUB, LANES), F32),
                   jax.ShapeDtypeStruct((n_tok, LANES), I32),
                   jax.ShapeDtypeStruct((n_tok, LANES), F32),
                   jax.ShapeDtypeStruct((8, LANES), F32)],
        scratch_shapes=[_slab_scratch()],
        compiler_params=_cparams(("arbitrary", "arbitrary")),
        name="out_proj_router",
    )(x_all, oa, ob, oc, od, w_out_p, g1, a2, sh2, rw, rb)


def _rank_kernel(ti_ref, ps_ref, tri_ref, dest_ref, carry_ref):
    @pl.when(pl.program_id(0) == 0)
    def _():
        carry_ref[...] = jnp.zeros(carry_ref.shape, F32)

    ti = ti_ref[...]
    lane = _lane_iota(ti.shape)
    hits = [lane == ti[:, k:k + 1] for k in range(TOP_K)]
    onehot = jnp.zeros(ti.shape, F32)
    for hk in hits:
        onehot = onehot + jnp.where(hk, 1.0, 0.0)
    before = jnp.dot(tri_ref[...], onehot.astype(BF16), preferred_element_type=F32)
    base = before + carry_ref[0:1, :] + ps_ref[...]
    dest = jnp.zeros(ti.shape, F32)
    for k, hk in enumerate(hits):
        dk = jnp.sum(jnp.where(hk, base, 0.0), axis=-1, keepdims=True)
        dest = jnp.where(lane == k, dk, dest)
    dest_ref[...] = dest.astype(I32)
    carry_ref[0:1, :] += jnp.sum(onehot, axis=0, keepdims=True)


def _rank_call(topi, pad_start, tri):
    n_tok = topi.shape[0]
    return pl.pallas_call(
        _rank_kernel,
        grid=(n_tok // TILE,),
        in_specs=[pl.BlockSpec((TILE, LANES), lambda t: (t, 0)),
                  pl.BlockSpec((1, LANES), lambda t: (0, 0)),
                  pl.BlockSpec((TILE, TILE), lambda t: (0, 0))],
        out_specs=pl.BlockSpec((TILE, LANES), lambda t: (t, 0)),
        out_shape=jax.ShapeDtypeStruct((n_tok, LANES), I32),
        scratch_shapes=[pltpu.VMEM((8, LANES), F32)],
        compiler_params=_cparams(("arbitrary",)),
        name="slot_rank",
    )(topi, pad_start, tri)


def _tile_copy(src_ref, src_row, dst_ref, dst_row, sem):
    src = src_ref.at[pl.ds(pl.multiple_of(src_row * ROW_SUB, ROW_SUB), ROW_SUB)]
    dst = dst_ref.at[pl.ds(pl.multiple_of(dst_row * ROW_SUB, ROW_SUB), ROW_SUB)]
    return pltpu.make_async_copy(src, dst, sem)


def _dispatch_kernel(fill_start_ref, fill_on_ref, dest_ref, h2_ref, zero_ref, xs_ref, sem):
    @pl.when(pl.program_id(0) == 0)
    def _():
        def fill(e, wait):
            @pl.when(fill_on_ref[e] > 0)
            def _():
                start = pl.multiple_of(fill_start_ref[e] * ROW_SUB, MOE_BLOCK * ROW_SUB)
                cp = pltpu.make_async_copy(zero_ref, xs_ref.at[pl.ds(start, MOE_BLOCK * ROW_SUB)], sem.at[1])
                if wait:
                    cp.wait()
                else:
                    cp.start()

        lax.fori_loop(0, N_EXPERTS, lambda e, c: (fill(e, False), c)[1], 0)
        lax.fori_loop(0, N_EXPERTS, lambda e, c: (fill(e, True), c)[1], 0)

    def issue(r, c):
        for k in range(TOP_K):
            _tile_copy(h2_ref, r, xs_ref, dest_ref[0, 0, TOP_K * r + k], sem.at[0]).start(priority=k % 2)
        return c

    lax.fori_loop(0, TILE, issue, 0, unroll=DMA_UNROLL)
    for k in range(TOP_K):
        pltpu.make_async_copy(h2_ref, xs_ref.at[pl.ds(0, TILE * ROW_SUB)], sem.at[0]).wait()


def _dispatch_call(fill_start, fill_on, dest3, h2t, n_slots):
    n_tok = h2t.shape[0] // ROW_SUB
    zero_blk = jnp.zeros((MOE_BLOCK * ROW_SUB, LANES), F32)
    return pl.pallas_call(
        _dispatch_kernel,
        grid_spec=pltpu.PrefetchScalarGridSpec(
            num_scalar_prefetch=2,
            grid=(n_tok // TILE,),
            in_specs=[pl.BlockSpec((1, 1, TOP_K * TILE), lambda t, fs, fo: (t, 0, 0), memory_space=pltpu.SMEM),
                      pl.BlockSpec((TILE * ROW_SUB, LANES), lambda t, fs, fo: (t, 0)),
                      pl.BlockSpec((MOE_BLOCK * ROW_SUB, LANES), lambda t, fs, fo: (0, 0))],
            out_specs=pl.BlockSpec(memory_space=pl.ANY),
            scratch_shapes=[pltpu.SemaphoreType.DMA((2,))]),
        out_shape=jax.ShapeDtypeStruct((n_slots * ROW_SUB, LANES), F32),
        compiler_params=_cparams(("arbitrary",)),
        name="moe_dispatch",
    )(fill_start, fill_on, dest3, h2t, zero_blk)


def _expert_kernel(blk_e_ref, nact_ref, xs_ref, wgu_ref, bgu_ref, wd_ref, bd_ref, yb_ref,
                   slab_ref, wgu_bf, wd_bf):
    j = pl.program_id(0)
    active = j < nact_ref[0]

    @pl.when(active & ((j == 0) | (blk_e_ref[j] != blk_e_ref[jnp.maximum(j - 1, 0)])))
    def _():
        wgu_bf[...] = wgu_ref[...].astype(BF16)
        wd_bf[...] = wd_ref[...].astype(BF16)

    @pl.when(active)
    def _():
        x = _tiles_to_rows(xs_ref, slab_ref).astype(BF16)
        gu = jnp.dot(x, wgu_bf[...], preferred_element_type=F32) + bgu_ref[...]
        g = jnp.minimum(gu[:, :D_FF], SWIGLU_LIMIT)
        u = jnp.clip(gu[:, D_FF:], -SWIGLU_LIMIT, SWIGLU_LIMIT)
        act = g * _sigmoid(SWIGLU_ALPHA * g) * (u + 1.0)
        y = jnp.dot(act.astype(BF16), wd_bf[...], preferred_element_type=F32) + bd_ref[...]
        _rows_to_tiles(y, yb_ref, slab_ref)


def _expert_call(blk_e, n_active, xs, w_gu, b_gu, w_down, b_down, li):
    n_slots = xs.shape[0] // ROW_SUB
    n_blocks = n_slots // MOE_BLOCK
    blk = lambda j, be, na: (jnp.minimum(j, na[0] - 1), 0)
    wsel = lambda j, be, na: (be[jnp.minimum(j, na[0] - 1)], 0, 0)
    wsel4 = lambda j, be, na: (li, be[jnp.minimum(j, na[0] - 1)], 0, 0)
    return pl.pallas_call(
        _expert_kernel,
        grid_spec=pltpu.PrefetchScalarGridSpec(
            num_scalar_prefetch=2,
            grid=(n_blocks,),
            in_specs=[pl.BlockSpec((MOE_BLOCK * ROW_SUB, LANES), blk),
                      pl.BlockSpec((None, None, D_MODEL, 2 * D_FF), wsel4),
                      pl.BlockSpec((None, 1, 2 * D_FF), wsel),
                      pl.BlockSpec((None, None, D_FF, D_MODEL), wsel4),
                      pl.BlockSpec((None, 1, D_MODEL), wsel)],
            out_specs=pl.BlockSpec((MOE_BLOCK * ROW_SUB, LANES), blk),
            scratch_shapes=[_slab_scratch(), pltpu.VMEM((D_MODEL, 2 * D_FF), BF16),
                            pltpu.VMEM((D_FF, D_MODEL), BF16)]),
        out_shape=jax.ShapeDtypeStruct((n_slots * ROW_SUB, LANES), F32),
        compiler_params=_cparams(("arbitrary",)),
        name="moe_experts",
    )(blk_e, n_active, xs, w_gu, b_gu, w_down, b_down)


def _combine_kernel(dest_ref, dest_next_ref, x_ref, gt_ref, g2_ref, yb_ref, o_ref, buf0, buf1, slab_ref, sem):
    step = pl.program_id(0) * pl.num_programs(1) + pl.program_id(1)
    last = pl.num_programs(0) * pl.num_programs(1) - 1

    def gather(d_ref, buf, s):
        def issue(r, c):
            for k in range(TOP_K):
                _tile_copy(yb_ref, d_ref[0, 0, TOP_K * r + k], buf.at[k], r, s).start(priority=k % 2)
            return c

        lax.fori_loop(0, TILE, issue, 0, unroll=DMA_UNROLL)

    pl.when(step == 0)(lambda: gather(dest_ref, buf0, sem.at[0]))

    def tile(parity):
        cur, nxt = (buf0, buf1) if parity == 0 else (buf1, buf0)
        pl.when(step < last)(lambda: gather(dest_next_ref, nxt, sem.at[1 - parity]))
        for k in range(TOP_K):
            pltpu.make_async_copy(yb_ref.at[pl.ds(0, TILE * ROW_SUB)], cur.at[k], sem.at[parity]).wait()
        gt = gt_ref[...]
        y = gt[:, 0:1] * _tiles_to_rows(cur.at[0], slab_ref)
        for k in range(1, TOP_K):
            y = y + gt[:, k:k + 1] * _tiles_to_rows(cur.at[k], slab_ref)
        o_ref[...] = x_ref[...] + g2_ref[...] * y

    for parity in range(2):
        pl.when(lax.rem(step, 2) == parity)(functools.partial(tile, parity))


def _combine_call(dest3, xn, gates, g2, yb, dims, ntq):
    B, NT, NL = dims
    n_tok = xn.shape[0]
    sel = lambda b, i: (jnp.where(i >= NL, B, b), 0, 0)
    crow = lambda b, i: (b * ntq + i, 0)
    nxt = lambda b, i: (jnp.minimum(b * ntq + i + 1, B * ntq - 1), 0, 0)
    return pl.pallas_call(
        _combine_kernel,
        grid=(B, ntq),
        in_specs=[pl.BlockSpec((1, 1, TOP_K * TILE), lambda b, i: (b * ntq + i, 0, 0), memory_space=pltpu.SMEM),
                  pl.BlockSpec((1, 1, TOP_K * TILE), nxt, memory_space=pltpu.SMEM),
                  pl.BlockSpec((TILE, D_MODEL), crow),
                  pl.BlockSpec((TILE, LANES), crow),
                  pl.BlockSpec((None, 1, D_MODEL), sel),
                  pl.BlockSpec(memory_space=pl.ANY)],
        out_specs=pl.BlockSpec((TILE, D_MODEL), crow),
        out_shape=jax.ShapeDtypeStruct((n_tok, D_MODEL), F32),
        scratch_shapes=[pltpu.VMEM((TOP_K, TILE * ROW_SUB, LANES), F32),
                        pltpu.VMEM((TOP_K, TILE * ROW_SUB, LANES), F32), _slab_scratch(),
                        pltpu.SemaphoreType.DMA((2,))],
        compiler_params=_cparams(("arbitrary", "arbitrary")),
        name="moe_combine",
    )(dest3, dest3, xn, gates, g2, yb)


def _pad_heads_cols(w, n_heads, width):
    w = w.reshape(w.shape[0], n_heads, width)
    return jnp.pad(w, ((0, 0), (0, 0), (0, LANES - width))).reshape(w.shape[0], n_heads * LANES)


def _pad_heads_rows(w, n_heads, width):
    w = w.reshape(n_heads, width, w.shape[1])
    return jnp.pad(w, ((0, 0), (0, LANES - width), (0, 0))).reshape(n_heads * LANES, w.shape[2])


def _prep_w_in(w):
    splits = (256, 128, 128, 256, 256, 256, 256, 256, 256)
    parts, o = [], 0
    for s in splits:
        parts.append(w[:, o:o + s])
        o += s
    aq, ak, av, bu, bg, cu, dq, dk, dv = parts
    return jnp.concatenate([
        _pad_heads_cols(aq, A_HEADS, HEAD_DIM), _pad_heads_cols(ak, A_KV_HEADS, HEAD_DIM),
        _pad_heads_cols(av, A_KV_HEADS, HEAD_DIM), bu, bg, cu,
        _pad_heads_cols(dq, D_HEADS, 2 * D_QKDIM), _pad_heads_cols(dk, D_HEADS, 2 * D_QKDIM),
        _pad_heads_cols(dv, D_HEADS, D_VDIM)], axis=1).astype(BF16)


def _prep_w_out(w):
    return jnp.concatenate([
        _pad_heads_rows(w[0:GROUP_W], A_HEADS, HEAD_DIM), w[GROUP_W:3 * GROUP_W],
        _pad_heads_rows(w[3 * GROUP_W:], D_HEADS, D_VDIM)], axis=0).astype(BF16)


def _pad_lanes(v):
    return jnp.pad(v.astype(F32), (0, LANES - v.shape[0]))


def _rope_tables(L, C):
    t = jnp.arange(L, dtype=jnp.int32)
    rows = (t // GRID_W).astype(F32)[:, None]
    cols = (t % GRID_W).astype(F32)[:, None]
    lane = jnp.arange(LANES)

    def table(group):
        nfreq = group // 2
        j = lane % nfreq
        inv = jnp.power(ROPE_THETA, -(2.0 * j.astype(F32)) / group)[None, :]
        use_cols = ((lane // group) % 2) == 1
        ang = jnp.where(use_cols[None, :], cols, rows) * inv
        sign = jnp.where((lane % group) < nfreq, -1.0, 1.0)[None, :]
        live = (lane < 2 * D_QKDIM)[None, :]
        cos = jnp.where(live, jnp.cos(ang), 1.0)
        sin = jnp.where(live, jnp.sin(ang) * sign, 0.0)
        cos = jnp.concatenate([cos, jnp.ones((C, LANES), F32)], axis=0)
        sin = jnp.concatenate([sin, jnp.zeros((C, LANES), F32)], axis=0)
        return cos, sin

    ca, sa = table(HEAD_DIM // 2)
    cd, sd = table(D_QKDIM // 2)
    return ca, sa, cd, sd


def _dft_mats(n):
    r = 1 << (int(math.log2(n)) // 2)
    assert n % r == 0
    k = jnp.arange(n, dtype=jnp.int32)[None, :]

    def table(rows, period):
        idx = jnp.arange(rows, dtype=jnp.int32)[:, None]
        ang = ((idx * k) % period).astype(F32) * (2.0 * math.pi / period)
        return jnp.cos(ang), jnp.sin(ang)

    c1, s1 = table(n // r, n // r)
    c2, s2 = table(r, n)
    c1, s1, c2, s2 = c1[:, None, :], s1[:, None, :], c2[None, :, :], s2[None, :, :]
    scale = 1.0 / math.sqrt(n)
    cos = ((c1 * c2 - s1 * s2) * scale).reshape(n, n).astype(BF16)
    sin = ((s1 * c2 + c1 * s2) * scale).reshape(n, n).astype(BF16)
    return cos, sin


def _channel_dft_mats():
    w = GROUP_W // FNET_GROUPS
    k = jnp.arange(GROUP_W, dtype=jnp.int32)
    same = (k[:, None] // w) == (k[None, :] // w)
    prod = ((k[:, None] % w) * (k[None, :] % w)) % w
    ang = prod.astype(F32) * (2.0 * math.pi / w)
    scale = 1.0 / math.sqrt(w)
    return (jnp.where(same, jnp.cos(ang) * scale, 0.0).astype(BF16),
            jnp.where(same, jnp.sin(ang) * scale, 0.0).astype(BF16))


def _slot_plan(counts, n_blocks):
    cnt = counts[0, :N_EXPERTS].astype(I32)
    padded = (cnt + MOE_BLOCK - 1) // MOE_BLOCK * MOE_BLOCK
    pad_ends = jnp.cumsum(padded)
    pad_starts = pad_ends - padded
    n_active = (pad_ends[-1] // MOE_BLOCK).reshape(1).astype(I32)
    blk_start = jnp.arange(n_blocks, dtype=I32) * MOE_BLOCK
    blk_e = jnp.minimum(jnp.sum(pad_ends[None, :] <= blk_start[:, None], axis=1), N_EXPERTS - 1).astype(I32)
    fill_start = jnp.maximum(pad_ends - MOE_BLOCK, 0).astype(I32)
    fill_on = (cnt > 0).astype(I32)
    ps_vec = jnp.pad(pad_starts.astype(F32), (0, LANES - N_EXPERTS)).reshape(1, LANES)
    return ps_vec, blk_e, n_active, fill_start, fill_on


def kernel(x, c, ctx, c_ctx, norm1_g, norm2_g, ada_w, ada_b, w_in, a_qnorm_g, a_knorm_g, conv_dw_w, conv_dw_b,
           conv_ln_g, conv_ln_b, d_qnorm_g, d_knorm_g, d_lambda, d_subln_g, w_out, router_w, router_b,
           exp_w_gu, exp_b_gu, exp_w_down, exp_b_down):
    B, L, D = x.shape
    C = ctx.shape[1]
    depth = ada_w.shape[0]
    assert D == D_MODEL and L % TILE == 0 and C == TILE and L % GRID_W == 0
    NL, NT = L // TILE, (L + C) // TILE
    dims = (B, NT, NL)
    S = NT * TILE

    x_all = jnp.concatenate([x, ctx], axis=1).reshape(B * S, D)

    a = jnp.concatenate([c, c_ctx[None, :]], axis=0)
    a = jnp.pad(a * jax.nn.sigmoid(a), ((0, 16 - (B + 1) % 16 if (B + 1) % 16 else 0), (0, 0)))
    mod_all = _ada_call(a, ada_w, ada_b)

    rope = _rope_tables(L, C)
    cmat, smat = _dft_mats(L)
    cmat_c, smat_c = _dft_mats(C)
    cbd, sbd = _channel_dft_mats()
    tri = jnp.tril(jnp.ones((TILE, TILE), F32), -1).astype(BF16)

    x_lat = None
    for li in range(depth):
        last = li == depth - 1
        ntq = NL if last else NT
        mod = mod_all[li, :B + 1].reshape(B + 1, 6, 1, D)
        sh1, sc1, g1, sh2, sc2, g2 = (mod[:, j] for j in range(6))
        a1 = norm1_g[li][None, None, :] * (1.0 + sc1)
        a2 = norm2_g[li][None, None, :] * (1.0 + sc2)

        gains = jnp.stack([
            _pad_lanes(a_qnorm_g[li]) * (HEAD_DIM ** -0.5 * LOG2E), _pad_lanes(a_knorm_g[li]),
            _pad_lanes(jnp.tile(d_qnorm_g[li], 2)) * (D_QKDIM ** -0.5 * LOG2E), _pad_lanes(jnp.tile(d_knorm_g[li], 2)),
        ] + [jnp.zeros((LANES,), F32)] * 4)
        qa, ka, va, z, cu, qd, kd, vd = _proj_call(x_all, a1, sh1, _prep_w_in(w_in[li]), gains, rope, dims)

        lam_init = 0.8 - 0.6 * math.exp(-0.3 * li)
        lq1, lk1, lq2, lk2 = d_lambda[li].astype(F32)
        lam = jnp.exp(jnp.sum(lq1 * lk1)) - jnp.exp(jnp.sum(lq2 * lk2)) + lam_init
        dpar = jnp.stack([jnp.full((LANES,), lam, F32), _pad_lanes(d_subln_g[li]) * (1.0 - lam_init)]
                         + [jnp.zeros((LANES,), F32)] * 6)

        oa = _gqa_call(qa, ka, va, dims, ntq)
        od = _diff_call(qd, kd, vd, dpar, dims, ntq)
        conv_w = jnp.pad(conv_dw_w[li], ((0, 32 - CONV_W), (0, 0)))
        conv_par = jnp.stack([conv_dw_b[li], conv_ln_g[li], conv_ln_b[li]] + [jnp.zeros((GROUP_W,), F32)] * 5)
        ob = _conv_call(z, conv_w, conv_par, dims, with_ctx=not last)
        oc = _dft_call(cu, cmat, smat, cbd, sbd, dims, 0, L)
        if not last:
            oc = _dft_call(cu, cmat_c, smat_c, cbd, sbd, dims, L, C, prev=oc)

        rw = jnp.pad(router_w[li], ((0, 0), (0, LANES - N_EXPERTS))).astype(BF16)
        rb = jnp.concatenate([router_b[li].astype(F32), jnp.full((LANES - N_EXPERTS,), NEG_BIG, F32)]).reshape(1, LANES)
        xn, h2t, topi, gates, counts = _out_call(x_all, oa, ob, oc, od, _prep_w_out(w_out[li]), g1, a2, sh2,
                                                 rw, rb, dims, ntq)

        n_tok = B * ntq * TILE
        n_blocks = -(-(n_tok * TOP_K + N_EXPERTS * (MOE_BLOCK - 1)) // MOE_BLOCK)
        ps_vec, blk_e, n_active, fill_start, fill_on = _slot_plan(counts, n_blocks)
        dest = _rank_call(topi, ps_vec, tri)
        dest3 = dest[:, :TOP_K].reshape(n_tok // TILE, 1, TOP_K * TILE)
        xs = _dispatch_call(fill_start, fill_on, dest3, h2t, n_blocks * MOE_BLOCK)
        yb = _expert_call(blk_e, n_active, xs, exp_w_gu, exp_b_gu[li][:, None, :],
                          exp_w_down, exp_b_down[li][:, None, :], li)
        x_next = _combine_call(dest3, xn, gates, g2, yb, dims, ntq)
        if last:
            x_lat = x_next
        else:
            x_all = x_next
    return x_lat.reshape(B, L, D)
```

```python
import functools
import math

import jax
import jax.numpy as jnp
from jax import lax
from jax.experimental import pallas as pl
from jax.experimental.pallas import tpu as pltpu

F32 = jnp.float32
BF16 = jnp.bfloat16
I32 = jnp.int32

D_MODEL = 1024
TILE = 256
LANES = 128
GRID_W = 64
HEAD_DIM = 64
A_HEADS = 4
A_KV_HEADS = 2
CONV_W = 31
CONV_HALO = 16
CONV_ROWS = 128
FNET_GROUPS = 4
D_HEADS = 4
D_QKDIM = 32
D_VDIM = 64
GROUP_W = 256
N_EXPERTS = 32
TOP_K = 4
D_FF = D_MODEL
SWIGLU_LIMIT = 7.0
SWIGLU_ALPHA = 1.702
MOE_BLOCK = 256
KEY_CHUNK = 256
DMA_UNROLL = 8
ROPE_THETA = 10000.0
EPS = 1e-6
NEG_BIG = -1e30
LOG2E = math.log2(math.e)
SUBLANES = 8
ROW_SUB = D_MODEL // LANES
assert ROW_SUB == SUBLANES
ROW_TILE = TILE
ROW_PITCH = ROW_TILE + SUBLANES

PROJ_W = (A_HEADS * LANES, A_KV_HEADS * LANES, A_KV_HEADS * LANES, GROUP_W, GROUP_W, GROUP_W,
          D_HEADS * LANES, D_HEADS * LANES, D_HEADS * LANES)
PROJ_OFF = tuple(sum(PROJ_W[:i]) for i in range(len(PROJ_W) + 1))
OUT_K = A_HEADS * LANES + 2 * GROUP_W + D_HEADS * LANES

VMEM_LIMIT = 52 * 1024 * 1024


def _cparams(sem):
    return pltpu.CompilerParams(dimension_semantics=sem, vmem_limit_bytes=VMEM_LIMIT)


def _lane_iota(shape):
    return lax.broadcasted_iota(I32, shape, len(shape) - 1)


def _sigmoid(x):
    return 0.5 * jnp.tanh(0.5 * x) + 0.5


def _rows_to_tiles(h, out_ref, slab_ref):
    for j in range(ROW_SUB):
        slab_ref[j * ROW_PITCH:j * ROW_PITCH + ROW_TILE, :] = h[:, j * LANES:(j + 1) * LANES]
    for t in range(ROW_TILE):
        out_ref[ROW_SUB * t:ROW_SUB * (t + 1), :] = slab_ref[pl.ds(t, ROW_SUB, stride=ROW_PITCH), :]


def _tiles_to_rows(in_ref, slab_ref):
    for t in range(ROW_TILE):
        slab_ref[pl.ds(t, ROW_SUB, stride=ROW_PITCH), :] = in_ref[ROW_SUB * t:ROW_SUB * (t + 1), :]
    return jnp.concatenate([slab_ref[j * ROW_PITCH:j * ROW_PITCH + ROW_TILE, :] for j in range(ROW_SUB)], axis=1)


def _slab_scratch():
    return pltpu.VMEM((ROW_SUB * ROW_PITCH, LANES), F32)


def _ada_kernel(a_ref, w_ref, b_ref, o_ref):
    o_ref[...] = jnp.dot(a_ref[...], w_ref[...], preferred_element_type=F32,
                         precision=lax.Precision.HIGHEST) + b_ref[...]


def _ada_call(a, ada_w, ada_b):
    depth, d, n = ada_w.shape
    rows = a.shape[0]
    nb = n // d
    return pl.pallas_call(
        _ada_kernel,
        grid=(depth, nb),
        in_specs=[pl.BlockSpec((rows, d), lambda l, j: (0, 0)),
                  pl.BlockSpec((None, d, d), lambda l, j: (l, 0, j)),
                  pl.BlockSpec((None, 1, d), lambda l, j: (l, 0, j))],
        out_specs=pl.BlockSpec((None, rows, d), lambda l, j: (l, 0, j)),
        out_shape=jax.ShapeDtypeStruct((depth, rows, n), F32),
        compiler_params=_cparams(("arbitrary", "arbitrary")),
        name="ada_mod",
    )(a, ada_w, ada_b.reshape(depth, 1, n))


def _swap_halves(x, half):
    lane = _lane_iota(x.shape)
    first = (lane & (2 * half - 1)) < half
    return jnp.where(first, pltpu.roll(x, LANES - half, 1), pltpu.roll(x, half, 1))


def _proj_kernel(x_ref, a_ref, sh_ref, w_ref, gains_ref, ca_ref, sa_ref, cd_ref, sd_ref,
                 qa_ref, ka_ref, va_ref, z_ref, cu_ref, qd_ref, kd_ref, vd_ref):
    x = x_ref[...]
    ms = jnp.mean(x * x, axis=-1, keepdims=True)
    h = (x * lax.rsqrt(ms + EPS) * a_ref[...] + sh_ref[...]).astype(BF16)
    p = jnp.dot(h, w_ref[...], preferred_element_type=F32)
    lane = _lane_iota((TILE, LANES))
    ca, sa, cd, sd = ca_ref[...], sa_ref[...], cd_ref[...], sd_ref[...]

    def slab(sec, i):
        o = PROJ_OFF[sec] + i * LANES
        return p[:, o:o + LANES]

    def norm_rope_a(s, gain):
        msq = jnp.sum(s * s, axis=-1, keepdims=True) * (1.0 / HEAD_DIM)
        sn = s * lax.rsqrt(msq + EPS) * gain
        return (sn * ca + _swap_halves(sn, HEAD_DIM // 4) * sa).astype(BF16)

    def norm_rope_d(s, gain):
        s2 = s * s
        m1 = jnp.sum(jnp.where(lane < D_QKDIM, s2, 0.0), axis=-1, keepdims=True)
        m2 = jnp.sum(jnp.where(lane >= D_QKDIM, s2, 0.0), axis=-1, keepdims=True)
        msq = jnp.where(lane < D_QKDIM, m1, m2) * (1.0 / D_QKDIM)
        sn = s * lax.rsqrt(msq + EPS) * gain
        return (sn * cd + _swap_halves(sn, D_QKDIM // 4) * sd).astype(BF16)

    def with_ones(s, width):
        st = s.T
        chan = lax.broadcasted_iota(I32, st.shape, 0)
        return jnp.where(chan == width, 1.0, st).astype(BF16)

    for hh in range(A_HEADS):
        qa_ref[hh] = norm_rope_a(slab(0, hh), gains_ref[0:1, :])
    for hh in range(A_KV_HEADS):
        ka_ref[hh] = norm_rope_a(slab(1, hh), gains_ref[1:2, :])
        va_ref[hh] = with_ones(slab(2, hh), HEAD_DIM)
    bu = p[:, PROJ_OFF[3]:PROJ_OFF[4]]
    bg = p[:, PROJ_OFF[4]:PROJ_OFF[5]]
    z_ref[...] = bu * _sigmoid(bg)
    cu_ref[...] = p[:, PROJ_OFF[5]:PROJ_OFF[6]].astype(BF16)
    for hh in range(D_HEADS):
        qd_ref[hh] = norm_rope_d(slab(6, hh), gains_ref[2:3, :])
        kd_ref[hh] = norm_rope_d(slab(7, hh), gains_ref[3:4, :])
        vd_ref[hh] = with_ones(slab(8, hh), D_VDIM)


def _proj_call(x_all, a1, sh1, w_in_p, gains, rope, dims):
    B, NT, NL = dims
    T = x_all.shape[0]
    S = NT * TILE
    sel = lambda b, i: (jnp.where(i >= NL, B, b), 0, 0)
    row = lambda b, i: (b * NT + i, 0)
    hrow = lambda b, i: (0, b * NT + i, 0)
    tab = lambda b, i: (i, 0)
    head = lambda n: pl.BlockSpec((n, TILE, LANES), hrow)
    hshape = lambda n: jax.ShapeDtypeStruct((n, T, LANES), BF16)
    head_t = lambda n: pl.BlockSpec((n, None, LANES, TILE), lambda b, i: (0, b, 0, i))
    hshape_t = lambda n: jax.ShapeDtypeStruct((n, B, LANES, S), BF16)
    nw = w_in_p.shape[1]
    return pl.pallas_call(
        _proj_kernel,
        grid=(B, NT),
        in_specs=[pl.BlockSpec((TILE, D_MODEL), row),
                  pl.BlockSpec((None, 1, D_MODEL), sel),
                  pl.BlockSpec((None, 1, D_MODEL), sel),
                  pl.BlockSpec((D_MODEL, nw), lambda b, i: (0, 0)),
                  pl.BlockSpec((8, LANES), lambda b, i: (0, 0)),
                  pl.BlockSpec((TILE, LANES), tab), pl.BlockSpec((TILE, LANES), tab),
                  pl.BlockSpec((TILE, LANES), tab), pl.BlockSpec((TILE, LANES), tab)],
        out_specs=[head(A_HEADS), head(A_KV_HEADS), head_t(A_KV_HEADS),
                   pl.BlockSpec((TILE, GROUP_W), row), pl.BlockSpec((TILE, GROUP_W), row),
                   head(D_HEADS), head(D_HEADS), head_t(D_HEADS)],
        out_shape=[hshape(A_HEADS), hshape(A_KV_HEADS), hshape_t(A_KV_HEADS),
                   jax.ShapeDtypeStruct((T, GROUP_W), F32), jax.ShapeDtypeStruct((T, GROUP_W), BF16),
                   hshape(D_HEADS), hshape(D_HEADS), hshape_t(D_HEADS)],
        compiler_params=_cparams(("arbitrary", "arbitrary")),
        name="in_proj",
    )(x_all, a1, sh1, w_in_p, gains, *rope)


def _scores_phase(q, k_ref, k0, nchunks, s_ref, m_ref):
    half = q.shape[0] // 2
    nk = nchunks * KEY_CHUNK
    for h in range(2):
        s_ref[h, 0:nk, :] = lax.dot_general(k_ref[k0:k0 + nk, :], q[h * half:(h + 1) * half],
                                            (((1,), (1,)), ((), ())), preferred_element_type=F32)
    for h in range(2):
        m8 = None
        for c in range(nchunks):
            blk = s_ref[h, c * KEY_CHUNK:(c + 1) * KEY_CHUNK, :]
            part = jnp.max(blk.reshape(KEY_CHUNK // SUBLANES, SUBLANES, half), axis=0)
            m8 = part if m8 is None else jnp.maximum(m8, part)
        m_ref[h] = m8


def _values_phase(vt_ref, k0, nchunks, s_ref, m_ref):
    half = s_ref.shape[2]
    maxes = [jnp.max(m_ref[h], axis=0, keepdims=True) for h in range(2)]
    accs = [jnp.zeros((LANES, half), F32) for _ in range(2)]
    for c in range(nchunks):
        vc = vt_ref[:, k0 + c * KEY_CHUNK:k0 + (c + 1) * KEY_CHUNK]
        for h in range(2):
            pr = jnp.exp2((s_ref[h, c * KEY_CHUNK:(c + 1) * KEY_CHUNK, :] - maxes[h]).astype(BF16))
            accs[h] = accs[h] + jnp.dot(vc, pr, preferred_element_type=F32)
    return jnp.concatenate(accs, axis=1)


def _attend(i, NL, NT, ntq, load_q, k_ref, vt_ref, s_scr, m_scr, a_scr, finish):
    lat = (0, NT * TILE // KEY_CHUNK)
    ctx = (NL * TILE, (NT - NL) * TILE // KEY_CHUNK)

    def case(step, score_keys, value_keys, store):
        w = step % 2
        if store:
            finish(a_scr[w][...])
        if score_keys is not None:
            _scores_phase(load_q(), k_ref, score_keys[0], score_keys[1], s_scr[w], m_scr[w])
        if value_keys is not None:
            a_scr[1 - w][...] = _values_phase(vt_ref, value_keys[0], value_keys[1], s_scr[1 - w], m_scr[1 - w])

    def at(cond, step, score_keys, value_keys, store):
        pl.when(cond)(lambda: case(step, score_keys, value_keys, store))

    at(i == 0, 0, lat, None, False)
    at(i == 1, 1, lat, lat, False)
    for parity in range(2):
        at((i >= 2) & (i < NL) & (lax.rem(i, 2) == parity), parity, lat, lat, True)
    if ntq > NL:
        at(i == NL, NL, ctx, lat, True)
        at(i == NL + 1, NL + 1, None, ctx, True)
        at(i == NL + 2, NL + 2, None, None, True)
    else:
        at(i == NL, NL, None, lat, True)
        at(i == NL + 1, NL + 1, None, None, True)


def _gqa_kernel(q_ref, k_ref, vt_ref, o_ref, s0, s1, m0, m1, a0, a1, *, NL, NT, ntq):
    group = A_HEADS // A_KV_HEADS
    load_q = lambda: q_ref[...].reshape(group * TILE, LANES)

    def finish(acc_t):
        o_t = acc_t * (1.0 / acc_t[HEAD_DIM:HEAD_DIM + 1, :])
        o_ref[...] = o_t.T.reshape(group, TILE, LANES).astype(BF16)

    _attend(pl.program_id(2), NL, NT, ntq, load_q, k_ref, vt_ref, (s0, s1), (m0, m1), (a0, a1), finish)


def _gqa_call(qa, ka, va, dims, ntq):
    B, NT, NL = dims
    T = qa.shape[1]
    S = NT * TILE
    group = A_HEADS // A_KV_HEADS
    qmap = lambda b, g, i: (g, b * NT + jnp.minimum(i, ntq - 1), 0)
    omap = lambda b, g, i: (g, b * NT + jnp.maximum(i - 2, 0), 0)
    kmap = lambda b, g, i: (g, b, 0)
    vmap = lambda b, g, i: (g, b, 0, 0)
    half = group * TILE // 2
    return pl.pallas_call(
        functools.partial(_gqa_kernel, NL=NL, NT=NT, ntq=ntq),
        grid=(B, A_KV_HEADS, ntq + 2),
        in_specs=[pl.BlockSpec((group, TILE, LANES), qmap),
                  pl.BlockSpec((None, S, LANES), kmap),
                  pl.BlockSpec((None, None, LANES, S), vmap)],
        out_specs=pl.BlockSpec((group, TILE, LANES), omap),
        out_shape=jax.ShapeDtypeStruct((A_HEADS, T, LANES), BF16),
        scratch_shapes=([pltpu.VMEM((2, S, half), F32)] * 2 + [pltpu.VMEM((2, SUBLANES, half), F32)] * 2
                        + [pltpu.VMEM((LANES, 2 * half), F32)] * 2),
        compiler_params=_cparams(("arbitrary", "arbitrary", "arbitrary")),
        name="gqa_attn",
    )(qa, ka, va)


def _diff_kernel(q_ref, k_ref, vt_ref, par_ref, o_ref, q2_scr, s0, s1, m0, m1, a0, a1, *, NL, NT, ntq):
    q = q_ref[...]
    lane = _lane_iota(q.shape)
    zero = jnp.zeros_like(q)
    q2_scr[0:TILE, :] = jnp.where(lane < D_QKDIM, q, zero)
    q2_scr[TILE:2 * TILE, :] = jnp.where(lane >= D_QKDIM, q, zero)

    def finish(acc_t):
        o1 = acc_t[:, :TILE] * (1.0 / acc_t[D_VDIM:D_VDIM + 1, :TILE])
        o2 = acc_t[:, TILE:] * (1.0 / acc_t[D_VDIM:D_VDIM + 1, TILE:])
        o = jnp.where(lane < D_VDIM, (o1 - par_ref[0:1, 0:1] * o2).T, 0.0)
        msq = jnp.sum(o * o, axis=-1, keepdims=True) * (1.0 / D_VDIM)
        o_ref[...] = (o * lax.rsqrt(msq + EPS) * par_ref[1:2, :]).astype(BF16)

    _attend(pl.program_id(2), NL, NT, ntq, lambda: q2_scr[...], k_ref, vt_ref, (s0, s1), (m0, m1), (a0, a1), finish)


def _diff_call(qd, kd, vd, dpar, dims, ntq):
    B, NT, NL = dims
    T = qd.shape[1]
    S = NT * TILE
    qmap = lambda b, h, i: (h, b * NT + jnp.minimum(i, ntq - 1), 0)
    omap = lambda b, h, i: (h, b * NT + jnp.maximum(i - 2, 0), 0)
    kmap = lambda b, h, i: (h, b, 0)
    vmap = lambda b, h, i: (h, b, 0, 0)
    return pl.pallas_call(
        functools.partial(_diff_kernel, NL=NL, NT=NT, ntq=ntq),
        grid=(B, D_HEADS, ntq + 2),
        in_specs=[pl.BlockSpec((None, TILE, LANES), qmap),
                  pl.BlockSpec((None, S, LANES), kmap),
                  pl.BlockSpec((None, None, LANES, S), vmap),
                  pl.BlockSpec((8, LANES), lambda b, h, i: (0, 0))],
        out_specs=pl.BlockSpec((None, TILE, LANES), omap),
        out_shape=jax.ShapeDtypeStruct((D_HEADS, T, LANES), BF16),
        scratch_shapes=([pltpu.VMEM((2 * TILE, LANES), BF16)] + [pltpu.VMEM((2, S, TILE), F32)] * 2
                        + [pltpu.VMEM((2, SUBLANES, TILE), F32)] * 2 + [pltpu.VMEM((LANES, 2 * TILE), F32)] * 2),
        compiler_params=_cparams(("arbitrary", "arbitrary", "arbitrary")),
        name="diff_attn",
    )(qd, kd, vd, dpar)


def _conv_kernel(z_ref, w_ref, par_ref, o_ref, zp_ref, sh_ref, *, segs):
    zeros = jnp.zeros((CONV_HALO, GROUP_W), F32)
    pos = 0
    starts = []
    for (r0, n) in segs:
        zp_ref[pos:pos + CONV_HALO, :] = zeros
        zp_ref[pos + CONV_HALO:pos + CONV_HALO + n, :] = z_ref[r0:r0 + n, :]
        starts.append(pos + CONV_HALO)
        pos += CONV_HALO + n
    zp_ref[pos:pos + CONV_HALO, :] = zeros
    bias, ln_g, ln_b = par_ref[0:1, :], par_ref[1:2, :], par_ref[2:3, :]

    for (r0, n), p0 in zip(segs, starts):
        def chunk(j, carry, r0=r0, p0=p0):
            base = pl.multiple_of(j * CONV_ROWS, CONV_ROWS)
            win = zp_ref[pl.ds(base + (p0 - CONV_HALO), CONV_ROWS + 2 * CONV_HALO), :]
            acc = jnp.zeros((CONV_ROWS, GROUP_W), F32)
            first = CONV_HALO - CONV_W // 2
            for res in range(SUBLANES):
                taps = [k for k in range(CONV_W) if (first + k) % SUBLANES == res]
                if not taps:
                    continue
                span = (first + taps[-1]) - res + CONV_ROWS
                sh_ref[res, 0:span, :] = win[res:res + span, :]
                for k in taps:
                    a = first + k - res
                    acc = acc + sh_ref[res, a:a + CONV_ROWS, :] * w_ref[k:k + 1, :]
            u = acc + bias
            mu = jnp.mean(u, axis=-1, keepdims=True)
            uc = u - mu
            var = jnp.mean(uc * uc, axis=-1, keepdims=True)
            y = uc * lax.rsqrt(var + EPS) * ln_g + ln_b
            o_ref[pl.ds(pl.multiple_of(base + r0, CONV_ROWS), CONV_ROWS), :] = (
                y * _sigmoid(y)).astype(BF16)
            return carry

        lax.fori_loop(0, n // CONV_ROWS, chunk, 0)


def _conv_call(z, conv_w, conv_par, dims, with_ctx):
    B, NT, NL = dims
    T = z.shape[0]
    S = NT * TILE
    L = NL * TILE
    segs = ((0, L), (L, S - L)) if with_ctx else ((0, L),)
    pad_rows = sum(n for _, n in segs) + CONV_HALO * (len(segs) + 1)
    return pl.pallas_call(
        functools.partial(_conv_kernel, segs=segs),
        grid=(B,),
        in_specs=[pl.BlockSpec((S, GROUP_W), lambda b: (b, 0)),
                  pl.BlockSpec((32, GROUP_W), lambda b: (0, 0)),
                  pl.BlockSpec((8, GROUP_W), lambda b: (0, 0))],
        out_specs=pl.BlockSpec((S, GROUP_W), lambda b: (b, 0)),
        out_shape=jax.ShapeDtypeStruct((T, GROUP_W), BF16),
        scratch_shapes=[pltpu.VMEM((pad_rows, GROUP_W), F32),
                        pltpu.VMEM((SUBLANES, CONV_ROWS + 2 * CONV_HALO, GROUP_W), F32)],
        compiler_params=_cparams(("arbitrary",)),
        name="conformer_conv",
    )(z, conv_w, conv_par)


def _dft_kernel(c_ref, s_ref, x_ref, cb_ref, sb_ref, o_ref, *, r0, n):
    xs = x_ref[r0:r0 + n, :]
    u = jnp.dot(c_ref[...], xs, preferred_element_type=F32).astype(BF16)
    w = jnp.dot(s_ref[...], xs, preferred_element_type=F32).astype(BF16)
    y = (jnp.dot(u, cb_ref[...], preferred_element_type=F32)
         - jnp.dot(w, sb_ref[...], preferred_element_type=F32))
    o_ref[...] = y.astype(BF16)


def _dft_call(cu, cmat, smat, cbd, sbd, dims, r0, n, prev=None):
    B, NT, NL = dims
    T = cu.shape[0]
    S = NT * TILE
    nr = n // TILE
    t0 = r0 // TILE
    args = [cmat, smat, cu, cbd, sbd]
    in_specs = [pl.BlockSpec((TILE, n), lambda r, b: (r, 0)),
                pl.BlockSpec((TILE, n), lambda r, b: (r, 0)),
                pl.BlockSpec((S, GROUP_W), lambda r, b: (b, 0)),
                pl.BlockSpec((GROUP_W, GROUP_W), lambda r, b: (0, 0)),
                pl.BlockSpec((GROUP_W, GROUP_W), lambda r, b: (0, 0))]
    kern = functools.partial(_dft_kernel, r0=r0, n=n)
    aliases = {}
    if prev is not None:
        args.append(prev)
        in_specs.append(pl.BlockSpec(memory_space=pl.ANY))
        aliases = {5: 0}
        kern = lambda c, s, x, cb, sb, _prev, o: _dft_kernel(c, s, x, cb, sb, o, r0=r0, n=n)
    return pl.pallas_call(
        kern,
        grid=(nr, B),
        in_specs=in_specs,
        out_specs=pl.BlockSpec((TILE, GROUP_W), lambda r, b: (b * NT + t0 + r, 0)),
        out_shape=jax.ShapeDtypeStruct((T, GROUP_W), BF16),
        input_output_aliases=aliases,
        compiler_params=_cparams(("arbitrary", "arbitrary")),
        name="fourier_mix",
    )(*args)


def _out_kernel(x_ref, oa_ref, ob_ref, oc_ref, od_ref, w_ref, g1_ref, a2_ref, sh2_ref, rw_ref, rb_ref,
                xn_ref, h2_ref, ti_ref, gt_ref, cnt_ref, slab_ref, logit_ref, *, n_tiles):
    t = pl.program_id(0)

    @pl.when(t == 0)
    def _():
        cnt_ref[...] = jnp.zeros(cnt_ref.shape, F32)
        _out_project(x_ref, oa_ref, ob_ref, oc_ref, od_ref, w_ref, g1_ref, a2_ref, sh2_ref, rw_ref, rb_ref,
                     xn_ref, h2_ref, slab_ref, logit_ref)

    @pl.when((t >= 1) & (t < n_tiles))
    def _():
        logits = logit_ref[...]
        _out_project(x_ref, oa_ref, ob_ref, oc_ref, od_ref, w_ref, g1_ref, a2_ref, sh2_ref, rw_ref, rb_ref,
                     xn_ref, h2_ref, slab_ref, logit_ref)
        _out_route(logits, ti_ref, gt_ref, cnt_ref)

    @pl.when(t == n_tiles)
    def _():
        _out_route(logit_ref[...], ti_ref, gt_ref, cnt_ref)


def _out_project(x_ref, oa_ref, ob_ref, oc_ref, od_ref, w_ref, g1_ref, a2_ref, sh2_ref, rw_ref, rb_ref,
                 xn_ref, h2_ref, slab_ref, logit_ref):
    ocat = jnp.concatenate([oa_ref[hh] for hh in range(A_HEADS)] + [ob_ref[...], oc_ref[...]]
                           + [od_ref[hh] for hh in range(D_HEADS)], axis=1)
    r = jnp.dot(ocat, w_ref[...], preferred_element_type=F32)
    xn = x_ref[...] + g1_ref[...] * r
    xn_ref[...] = xn
    ms = jnp.mean(xn * xn, axis=-1, keepdims=True)
    h2 = xn * lax.rsqrt(ms + EPS) * a2_ref[...] + sh2_ref[...]
    _rows_to_tiles(h2, h2_ref, slab_ref)
    logit_ref[...] = jnp.dot(h2.astype(BF16), rw_ref[...], preferred_element_type=F32) + rb_ref[...]


def _out_route(logits, ti_ref, gt_ref, cnt_ref):
    lane = _lane_iota(logits.shape)
    lanef = lane.astype(F32)
    vals, idxs = [], []
    cur = logits
    for _ in range(TOP_K):
        mx = jnp.max(cur, axis=-1, keepdims=True)
        idx = jnp.min(jnp.where(cur == mx, lanef, float(LANES)), axis=-1, keepdims=True)
        vals.append(mx)
        idxs.append(idx)
        cur = jnp.where(lanef == idx, NEG_BIG * 2, cur)
    exps = [jnp.exp(v - vals[0]) for v in vals]
    inv = 1.0 / (exps[0] + exps[1] + exps[2] + exps[3])
    ti = jnp.zeros(logits.shape, F32)
    gt = jnp.zeros(logits.shape, F32)
    onehot = jnp.zeros(logits.shape, F32)
    for k in range(TOP_K):
        ti = jnp.where(lane == k, idxs[k], ti)
        gt = jnp.where(lane == k, exps[k] * inv, gt)
        onehot = onehot + jnp.where(lanef == idxs[k], 1.0, 0.0)
    ti_ref[...] = ti.astype(I32)
    gt_ref[...] = gt
    cnt_ref[0:1, :] += jnp.sum(onehot, axis=0, keepdims=True)


def _out_call(x_all, oa, ob, oc, od, w_out_p, g1, a2, sh2, rw, rb, dims, ntq):
    B, NT, NL = dims
    n_tiles = B * ntq
    n_tok = n_tiles * TILE
    cur = lambda t: jnp.minimum(t, n_tiles - 1)
    bat = lambda t: cur(t) // ntq
    pos = lambda t: cur(t) % ntq
    sel = lambda t: (jnp.where(pos(t) >= NL, B, bat(t)), 0, 0)
    row = lambda t: (bat(t) * NT + pos(t), 0)
    hrow = lambda t: (0, bat(t) * NT + pos(t), 0)
    crow = lambda t: (cur(t), 0)
    prow = lambda t: (jnp.maximum(t - 1, 0), 0)
    const = lambda t: (0, 0)
    return pl.pallas_call(
        functools.partial(_out_kernel, n_tiles=n_tiles),
        grid=(n_tiles + 1,),
        in_specs=[pl.BlockSpec((TILE, D_MODEL), row),
                  pl.BlockSpec((A_HEADS, TILE, LANES), hrow),
                  pl.BlockSpec((TILE, GROUP_W), row),
                  pl.BlockSpec((TILE, GROUP_W), row),
                  pl.BlockSpec((D_HEADS, TILE, LANES), hrow),
                  pl.BlockSpec((OUT_K, D_MODEL), const),
                  pl.BlockSpec((None, 1, D_MODEL), sel),
                  pl.BlockSpec((None, 1, D_MODEL), sel),
                  pl.BlockSpec((None, 1, D_MODEL), sel),
                  pl.BlockSpec((D_MODEL, LANES), const),
                  pl.BlockSpec((1, LANES), const)],
        out_specs=[pl.BlockSpec((TILE, D_MODEL), crow),
                   pl.BlockSpec((TILE * ROW_SUB, LANES), crow),
                   pl.BlockSpec((TILE, LANES), prow),
                   pl.BlockSpec((TILE, LANES), prow),
                   pl.BlockSpec((8, LANES), const)],
        out_shape=[jax.ShapeDtypeStruct((n_tok, D_MODEL), F32),
                   jax.ShapeDtypeStruct((n_tok * ROW_SUB, LANES), F32),
                   jax.ShapeDtypeStruct((n_tok, LANES), I32),
                   jax.ShapeDtypeStruct((n_tok, LANES), F32),
                   jax.ShapeDtypeStruct((8, LANES), F32)],
        scratch_shapes=[_slab_scratch(), pltpu.VMEM((TILE, LANES), F32)],
        compiler_params=_cparams(("arbitrary",)),
        name="out_proj_router",
    )(x_all, oa, ob, oc, od, w_out_p, g1, a2, sh2, rw, rb)


def _rank_kernel(ti_ref, ps_ref, tri_ref, dest_ref, carry_ref):
    @pl.when(pl.program_id(0) == 0)
    def _():
        carry_ref[...] = jnp.zeros(carry_ref.shape, F32)

    ti = ti_ref[...]
    lane = _lane_iota(ti.shape)
    hits = [lane == ti[:, k:k + 1] for k in range(TOP_K)]
    onehot = jnp.zeros(ti.shape, F32)
    for hk in hits:
        onehot = onehot + jnp.where(hk, 1.0, 0.0)
    before = jnp.dot(tri_ref[...], onehot.astype(BF16), preferred_element_type=F32)
    base = before + carry_ref[0:1, :] + ps_ref[...]
    dest = jnp.zeros(ti.shape, F32)
    for k, hk in enumerate(hits):
        dk = jnp.sum(jnp.where(hk, base, 0.0), axis=-1, keepdims=True)
        dest = jnp.where(lane == k, dk, dest)
    dest_ref[...] = dest.astype(I32)
    carry_ref[0:1, :] += jnp.sum(onehot, axis=0, keepdims=True)


def _rank_call(topi, pad_start, tri):
    n_tok = topi.shape[0]
    return pl.pallas_call(
        _rank_kernel,
        grid=(n_tok // TILE,),
        in_specs=[pl.BlockSpec((TILE, LANES), lambda t: (t, 0)),
                  pl.BlockSpec((1, LANES), lambda t: (0, 0)),
                  pl.BlockSpec((TILE, TILE), lambda t: (0, 0))],
        out_specs=pl.BlockSpec((TILE, LANES), lambda t: (t, 0)),
        out_shape=jax.ShapeDtypeStruct((n_tok, LANES), I32),
        scratch_shapes=[pltpu.VMEM((8, LANES), F32)],
        compiler_params=_cparams(("arbitrary",)),
        name="slot_rank",
    )(topi, pad_start, tri)


def _tile_copy(src_ref, src_row, dst_ref, dst_row, sem):
    src = src_ref.at[pl.ds(pl.multiple_of(src_row * ROW_SUB, ROW_SUB), ROW_SUB)]
    dst = dst_ref.at[pl.ds(pl.multiple_of(dst_row * ROW_SUB, ROW_SUB), ROW_SUB)]
    return pltpu.make_async_copy(src, dst, sem)


def _dispatch_kernel(fill_start_ref, fill_on_ref, dest_ref, h2_ref, zero_ref, xs_ref, sem):
    @pl.when(pl.program_id(0) == 0)
    def _():
        def fill(e, wait):
            @pl.when(fill_on_ref[e] > 0)
            def _():
                start = pl.multiple_of(fill_start_ref[e] * ROW_SUB, MOE_BLOCK * ROW_SUB)
                cp = pltpu.make_async_copy(zero_ref, xs_ref.at[pl.ds(start, MOE_BLOCK * ROW_SUB)], sem.at[1])
                if wait:
                    cp.wait()
                else:
                    cp.start()

        lax.fori_loop(0, N_EXPERTS, lambda e, c: (fill(e, False), c)[1], 0)
        lax.fori_loop(0, N_EXPERTS, lambda e, c: (fill(e, True), c)[1], 0)

    def issue(r, c):
        for k in range(TOP_K):
            _tile_copy(h2_ref, r, xs_ref, dest_ref[0, 0, TOP_K * r + k], sem.at[0]).start(priority=k % 2)
        return c

    lax.fori_loop(0, TILE, issue, 0, unroll=DMA_UNROLL)
    for k in range(TOP_K):
        pltpu.make_async_copy(h2_ref, xs_ref.at[pl.ds(0, TILE * ROW_SUB)], sem.at[0]).wait()


def _dispatch_call(fill_start, fill_on, dest3, h2t, n_slots):
    n_tok = h2t.shape[0] // ROW_SUB
    zero_blk = jnp.zeros((MOE_BLOCK * ROW_SUB, LANES), F32)
    return pl.pallas_call(
        _dispatch_kernel,
        grid_spec=pltpu.PrefetchScalarGridSpec(
            num_scalar_prefetch=2,
            grid=(n_tok // TILE,),
            in_specs=[pl.BlockSpec((1, 1, TOP_K * TILE), lambda t, fs, fo: (t, 0, 0), memory_space=pltpu.SMEM),
                      pl.BlockSpec((TILE * ROW_SUB, LANES), lambda t, fs, fo: (t, 0)),
                      pl.BlockSpec((MOE_BLOCK * ROW_SUB, LANES), lambda t, fs, fo: (0, 0))],
            out_specs=pl.BlockSpec(memory_space=pl.ANY),
            scratch_shapes=[pltpu.SemaphoreType.DMA((2,))]),
        out_shape=jax.ShapeDtypeStruct((n_slots * ROW_SUB, LANES), F32),
        compiler_params=_cparams(("arbitrary",)),
        name="moe_dispatch",
    )(fill_start, fill_on, dest3, h2t, zero_blk)


def _expert_kernel(blk_e_ref, nact_ref, xs_ref, wgu_ref, bgu_ref, wd_ref, bd_ref, yb_ref,
                   slab_ref, wgu_bf, wd_bf):
    j = pl.program_id(0)
    active = j < nact_ref[0]

    @pl.when(active & ((j == 0) | (blk_e_ref[j] != blk_e_ref[jnp.maximum(j - 1, 0)])))
    def _():
        wgu_bf[...] = wgu_ref[...].astype(BF16)
        wd_bf[...] = wd_ref[...].astype(BF16)

    @pl.when(active)
    def _():
        x = _tiles_to_rows(xs_ref, slab_ref).astype(BF16)
        gu = jnp.dot(x, wgu_bf[...], preferred_element_type=F32) + bgu_ref[...]
        g = jnp.minimum(gu[:, :D_FF], SWIGLU_LIMIT)
        u = jnp.clip(gu[:, D_FF:], -SWIGLU_LIMIT, SWIGLU_LIMIT)
        act = g * _sigmoid(SWIGLU_ALPHA * g) * (u + 1.0)
        y = jnp.dot(act.astype(BF16), wd_bf[...], preferred_element_type=F32) + bd_ref[...]
        _rows_to_tiles(y, yb_ref, slab_ref)


def _expert_call(blk_e, n_active, xs, w_gu, b_gu, w_down, b_down, li):
    n_slots = xs.shape[0] // ROW_SUB
    n_blocks = n_slots // MOE_BLOCK
    blk = lambda j, be, na: (jnp.minimum(j, na[0] - 1), 0)
    wsel = lambda j, be, na: (be[jnp.minimum(j, na[0] - 1)], 0, 0)
    wsel4 = lambda j, be, na: (li, be[jnp.minimum(j, na[0] - 1)], 0, 0)
    return pl.pallas_call(
        _expert_kernel,
        grid_spec=pltpu.PrefetchScalarGridSpec(
            num_scalar_prefetch=2,
            grid=(n_blocks,),
            in_specs=[pl.BlockSpec((MOE_BLOCK * ROW_SUB, LANES), blk),
                      pl.BlockSpec((None, None, D_MODEL, 2 * D_FF), wsel4),
                      pl.BlockSpec((None, 1, 2 * D_FF), wsel),
                      pl.BlockSpec((None, None, D_FF, D_MODEL), wsel4),
                      pl.BlockSpec((None, 1, D_MODEL), wsel)],
            out_specs=pl.BlockSpec((MOE_BLOCK * ROW_SUB, LANES), blk),
            scratch_shapes=[_slab_scratch(), pltpu.VMEM((D_MODEL, 2 * D_FF), BF16),
                            pltpu.VMEM((D_FF, D_MODEL), BF16)]),
        out_shape=jax.ShapeDtypeStruct((n_slots * ROW_SUB, LANES), F32),
        compiler_params=_cparams(("arbitrary",)),
        name="moe_experts",
    )(blk_e, n_active, xs, w_gu, b_gu, w_down, b_down)


def _combine_kernel(dest_ref, dest_next_ref, x_ref, gt_ref, g2_ref, yb_ref, o_ref, buf0, buf1, slab_ref, sem):
    step = pl.program_id(0) * pl.num_programs(1) + pl.program_id(1)
    last = pl.num_programs(0) * pl.num_programs(1) - 1

    def gather(d_ref, buf, s):
        def issue(r, c):
            for k in range(TOP_K):
                _tile_copy(yb_ref, d_ref[0, 0, TOP_K * r + k], buf.at[k], r, s).start(priority=k % 2)
            return c

        lax.fori_loop(0, TILE, issue, 0, unroll=DMA_UNROLL)

    pl.when(step == 0)(lambda: gather(dest_ref, buf0, sem.at[0]))

    def tile(parity):
        cur, nxt = (buf0, buf1) if parity == 0 else (buf1, buf0)
        pl.when(step < last)(lambda: gather(dest_next_ref, nxt, sem.at[1 - parity]))
        for k in range(TOP_K):
            pltpu.make_async_copy(yb_ref.at[pl.ds(0, TILE * ROW_SUB)], cur.at[k], sem.at[parity]).wait()
        gt = gt_ref[...]
        y = gt[:, 0:1] * _tiles_to_rows(cur.at[0], slab_ref)
        for k in range(1, TOP_K):
            y = y + gt[:, k:k + 1] * _tiles_to_rows(cur.at[k], slab_ref)
        o_ref[...] = x_ref[...] + g2_ref[...] * y

    for parity in range(2):
        pl.when(lax.rem(step, 2) == parity)(functools.partial(tile, parity))


def _combine_call(dest3, xn, gates, g2, yb, dims, ntq):
    B, NT, NL = dims
    n_tok = xn.shape[0]
    sel = lambda b, i: (jnp.where(i >= NL, B, b), 0, 0)
    crow = lambda b, i: (b * ntq + i, 0)
    nxt = lambda b, i: (jnp.minimum(b * ntq + i + 1, B * ntq - 1), 0, 0)
    return pl.pallas_call(
        _combine_kernel,
        grid=(B, ntq),
        in_specs=[pl.BlockSpec((1, 1, TOP_K * TILE), lambda b, i: (b * ntq + i, 0, 0), memory_space=pltpu.SMEM),
                  pl.BlockSpec((1, 1, TOP_K * TILE), nxt, memory_space=pltpu.SMEM),
                  pl.BlockSpec((TILE, D_MODEL), crow),
                  pl.BlockSpec((TILE, LANES), crow),
                  pl.BlockSpec((None, 1, D_MODEL), sel),
                  pl.BlockSpec(memory_space=pl.ANY)],
        out_specs=pl.BlockSpec((TILE, D_MODEL), crow),
        out_shape=jax.ShapeDtypeStruct((n_tok, D_MODEL), F32),
        scratch_shapes=[pltpu.VMEM((TOP_K, TILE * ROW_SUB, LANES), F32),
                        pltpu.VMEM((TOP_K, TILE * ROW_SUB, LANES), F32), _slab_scratch(),
                        pltpu.SemaphoreType.DMA((2,))],
        compiler_params=_cparams(("arbitrary", "arbitrary")),
        name="moe_combine",
    )(dest3, dest3, xn, gates, g2, yb)


def _pad_heads_cols(w, n_heads, width):
    w = w.reshape(w.shape[0], n_heads, width)
    return jnp.pad(w, ((0, 0), (0, 0), (0, LANES - width))).reshape(w.shape[0], n_heads * LANES)


def _pad_heads_rows(w, n_heads, width):
    w = w.reshape(n_heads, width, w.shape[1])
    return jnp.pad(w, ((0, 0), (0, LANES - width), (0, 0))).reshape(n_heads * LANES, w.shape[2])


def _prep_w_in(w):
    splits = (256, 128, 128, 256, 256, 256, 256, 256, 256)
    parts, o = [], 0
    for s in splits:
        parts.append(w[:, o:o + s])
        o += s
    aq, ak, av, bu, bg, cu, dq, dk, dv = parts
    return jnp.concatenate([
        _pad_heads_cols(aq, A_HEADS, HEAD_DIM), _pad_heads_cols(ak, A_KV_HEADS, HEAD_DIM),
        _pad_heads_cols(av, A_KV_HEADS, HEAD_DIM), bu, bg, cu,
        _pad_heads_cols(dq, D_HEADS, 2 * D_QKDIM), _pad_heads_cols(dk, D_HEADS, 2 * D_QKDIM),
        _pad_heads_cols(dv, D_HEADS, D_VDIM)], axis=1).astype(BF16)


def _prep_w_out(w):
    return jnp.concatenate([
        _pad_heads_rows(w[0:GROUP_W], A_HEADS, HEAD_DIM), w[GROUP_W:3 * GROUP_W],
        _pad_heads_rows(w[3 * GROUP_W:], D_HEADS, D_VDIM)], axis=0).astype(BF16)


def _pad_lanes(v):
    return jnp.pad(v.astype(F32), (0, LANES - v.shape[0]))


def _rope_tables(L, C):
    t = jnp.arange(L, dtype=jnp.int32)
    rows = (t // GRID_W).astype(F32)[:, None]
    cols = (t % GRID_W).astype(F32)[:, None]
    lane = jnp.arange(LANES)

    def table(group):
        nfreq = group // 2
        j = lane % nfreq
        inv = jnp.power(ROPE_THETA, -(2.0 * j.astype(F32)) / group)[None, :]
        use_cols = ((lane // group) % 2) == 1
        ang = jnp.where(use_cols[None, :], cols, rows) * inv
        sign = jnp.where((lane % group) < nfreq, -1.0, 1.0)[None, :]
        live = (lane < 2 * D_QKDIM)[None, :]
        cos = jnp.where(live, jnp.cos(ang), 1.0)
        sin = jnp.where(live, jnp.sin(ang) * sign, 0.0)
        cos = jnp.concatenate([cos, jnp.ones((C, LANES), F32)], axis=0)
        sin = jnp.concatenate([sin, jnp.zeros((C, LANES), F32)], axis=0)
        return cos, sin

    ca, sa = table(HEAD_DIM // 2)
    cd, sd = table(D_QKDIM // 2)
    return ca, sa, cd, sd


def _dft_mats(n):
    r = 1 << (int(math.log2(n)) // 2)
    assert n % r == 0
    k = jnp.arange(n, dtype=jnp.int32)[None, :]

    def table(rows, period):
        idx = jnp.arange(rows, dtype=jnp.int32)[:, None]
        ang = ((idx * k) % period).astype(F32) * (2.0 * math.pi / period)
        return jnp.cos(ang), jnp.sin(ang)

    c1, s1 = table(n // r, n // r)
    c2, s2 = table(r, n)
    c1, s1, c2, s2 = c1[:, None, :], s1[:, None, :], c2[None, :, :], s2[None, :, :]
    scale = 1.0 / math.sqrt(n)
    cos = ((c1 * c2 - s1 * s2) * scale).reshape(n, n).astype(BF16)
    sin = ((s1 * c2 + c1 * s2) * scale).reshape(n, n).astype(BF16)
    return cos, sin


def _channel_dft_mats():
    w = GROUP_W // FNET_GROUPS
    k = jnp.arange(GROUP_W, dtype=jnp.int32)
    same = (k[:, None] // w) == (k[None, :] // w)
    prod = ((k[:, None] % w) * (k[None, :] % w)) % w
    ang = prod.astype(F32) * (2.0 * math.pi / w)
    scale = 1.0 / math.sqrt(w)
    return (jnp.where(same, jnp.cos(ang) * scale, 0.0).astype(BF16),
            jnp.where(same, jnp.sin(ang) * scale, 0.0).astype(BF16))


def _slot_plan(counts, n_blocks):
    cnt = counts[0, :N_EXPERTS].astype(I32)
    padded = (cnt + MOE_BLOCK - 1) // MOE_BLOCK * MOE_BLOCK
    pad_ends = jnp.cumsum(padded)
    pad_starts = pad_ends - padded
    n_active = (pad_ends[-1] // MOE_BLOCK).reshape(1).astype(I32)
    blk_start = jnp.arange(n_blocks, dtype=I32) * MOE_BLOCK
    blk_e = jnp.minimum(jnp.sum(pad_ends[None, :] <= blk_start[:, None], axis=1), N_EXPERTS - 1).astype(I32)
    fill_start = jnp.maximum(pad_ends - MOE_BLOCK, 0).astype(I32)
    fill_on = (cnt > 0).astype(I32)
    ps_vec = jnp.pad(pad_starts.astype(F32), (0, LANES - N_EXPERTS)).reshape(1, LANES)
    return ps_vec, blk_e, n_active, fill_start, fill_on


def kernel(x, c, ctx, c_ctx, norm1_g, norm2_g, ada_w, ada_b, w_in, a_qnorm_g, a_knorm_g, conv_dw_w, conv_dw_b,
           conv_ln_g, conv_ln_b, d_qnorm_g, d_knorm_g, d_lambda, d_subln_g, w_out, router_w, router_b,
           exp_w_gu, exp_b_gu, exp_w_down, exp_b_down):
    B, L, D = x.shape
    C = ctx.shape[1]
    depth = ada_w.shape[0]
    assert D == D_MODEL and L % TILE == 0 and C == TILE and L % GRID_W == 0
    NL, NT = L // TILE, (L + C) // TILE
    dims = (B, NT, NL)
    S = NT * TILE

    x_all = jnp.concatenate([x, ctx], axis=1).reshape(B * S, D)

    a = jnp.concatenate([c, c_ctx[None, :]], axis=0)
    a = jnp.pad(a * jax.nn.sigmoid(a), ((0, 16 - (B + 1) % 16 if (B + 1) % 16 else 0), (0, 0)))
    mod_all = _ada_call(a, ada_w, ada_b)

    rope = _rope_tables(L, C)
    cmat, smat = _dft_mats(L)
    cmat_c, smat_c = _dft_mats(C)
    cbd, sbd = _channel_dft_mats()
    tri = jnp.tril(jnp.ones((TILE, TILE), F32), -1).astype(BF16)

    x_lat = None
    for li in range(depth):
        last = li == depth - 1
        ntq = NL if last else NT
        mod = mod_all[li, :B + 1].reshape(B + 1, 6, 1, D)
        sh1, sc1, g1, sh2, sc2, g2 = (mod[:, j] for j in range(6))
        a1 = norm1_g[li][None, None, :] * (1.0 + sc1)
        a2 = norm2_g[li][None, None, :] * (1.0 + sc2)

        gains = jnp.stack([
            _pad_lanes(a_qnorm_g[li]) * (HEAD_DIM ** -0.5 * LOG2E), _pad_lanes(a_knorm_g[li]),
            _pad_lanes(jnp.tile(d_qnorm_g[li], 2)) * (D_QKDIM ** -0.5 * LOG2E), _pad_lanes(jnp.tile(d_knorm_g[li], 2)),
        ] + [jnp.zeros((LANES,), F32)] * 4)
        qa, ka, va, z, cu, qd, kd, vd = _proj_call(x_all, a1, sh1, _prep_w_in(w_in[li]), gains, rope, dims)

        lam_init = 0.8 - 0.6 * math.exp(-0.3 * li)
        lq1, lk1, lq2, lk2 = d_lambda[li].astype(F32)
        lam = jnp.exp(jnp.sum(lq1 * lk1)) - jnp.exp(jnp.sum(lq2 * lk2)) + lam_init
        dpar = jnp.stack([jnp.full((LANES,), lam, F32), _pad_lanes(d_subln_g[li]) * (1.0 - lam_init)]
                         + [jnp.zeros((LANES,), F32)] * 6)

        oa = _gqa_call(qa, ka, va, dims, ntq)
        od = _diff_call(qd, kd, vd, dpar, dims, ntq)
        conv_w = jnp.pad(conv_dw_w[li], ((0, 32 - CONV_W), (0, 0)))
        conv_par = jnp.stack([conv_dw_b[li], conv_ln_g[li], conv_ln_b[li]] + [jnp.zeros((GROUP_W,), F32)] * 5)
        ob = _conv_call(z, conv_w, conv_par, dims, with_ctx=not last)
        oc = _dft_call(cu, cmat, smat, cbd, sbd, dims, 0, L)
        if not last:
            oc = _dft_call(cu, cmat_c, smat_c, cbd, sbd, dims, L, C, prev=oc)

        rw = jnp.pad(router_w[li], ((0, 0), (0, LANES - N_EXPERTS))).astype(BF16)
        rb = jnp.concatenate([router_b[li].astype(F32), jnp.full((LANES - N_EXPERTS,), NEG_BIG, F32)]).reshape(1, LANES)
        xn, h2t, topi, gates, counts = _out_call(x_all, oa, ob, oc, od, _prep_w_out(w_out[li]), g1, a2, sh2,
                                                 rw, rb, dims, ntq)

        n_tok = B * ntq * TILE
        n_blocks = -(-(n_tok * TOP_K + N_EXPERTS * (MOE_BLOCK - 1)) // MOE_BLOCK)
        ps_vec, blk_e, n_active, fill_start, fill_on = _slot_plan(counts, n_blocks)
        dest = _rank_call(topi, ps_vec, tri)
        dest3 = dest[:, :TOP_K].reshape(n_tok // TILE, 1, TOP_K * TILE)
        xs = _dispatch_call(fill_start, fill_on, dest3, h2t, n_blocks * MOE_BLOCK)
        yb = _expert_call(blk_e, n_active, xs, exp_w_gu, exp_b_gu[li][:, None, :],
                          exp_w_down, exp_b_down[li][:, None, :], li)
        x_next = _combine_call(dest3, xn, gates, g2, yb, dims, ntq)
        if last:
            x_lat = x_next
        else:
            x_all = x_next
    return x_lat.reshape(B, L, D)
```

```python
import functools
import math

import jax
import jax.numpy as jnp
from jax import lax
from jax.experimental import pallas as pl
from jax.experimental.pallas import tpu as pltpu

F32 = jnp.float32
BF16 = jnp.bfloat16
I32 = jnp.int32

D_MODEL = 1024
TILE = 256
LANES = 128
GRID_W = 64
HEAD_DIM = 64
A_HEADS = 4
A_KV_HEADS = 2
CONV_W = 31
CONV_HALO = 16
CONV_ROWS = 128
FNET_GROUPS = 4
D_HEADS = 4
D_QKDIM = 32
D_VDIM = 64
GROUP_W = 256
N_EXPERTS = 32
TOP_K = 4
D_FF = D_MODEL
SWIGLU_LIMIT = 7.0
SWIGLU_ALPHA = 1.702
MOE_BLOCK = 256
KEY_CHUNK = 256
DMA_UNROLL = 8
ROPE_THETA = 10000.0
EPS = 1e-6
NEG_BIG = -1e30
LOG2E = math.log2(math.e)
SUBLANES = 8
ROW_SUB = D_MODEL // LANES
assert ROW_SUB == SUBLANES
ROW_TILE = TILE
ROW_PITCH = ROW_TILE + SUBLANES

PROJ_W = (A_HEADS * LANES, A_KV_HEADS * LANES, A_KV_HEADS * LANES, GROUP_W, GROUP_W, GROUP_W,
          D_HEADS * LANES, D_HEADS * LANES, D_HEADS * LANES)
PROJ_OFF = tuple(sum(PROJ_W[:i]) for i in range(len(PROJ_W) + 1))
OUT_K = A_HEADS * LANES + 2 * GROUP_W + D_HEADS * LANES

VMEM_LIMIT = 52 * 1024 * 1024


def _cparams(sem):
    return pltpu.CompilerParams(dimension_semantics=sem, vmem_limit_bytes=VMEM_LIMIT)


def _lane_iota(shape):
    return lax.broadcasted_iota(I32, shape, len(shape) - 1)


def _sigmoid(x):
    return 0.5 * jnp.tanh(0.5 * x) + 0.5


def _rows_to_tiles(h, out_ref, slab_ref):
    for j in range(ROW_SUB):
        slab_ref[j * ROW_PITCH:j * ROW_PITCH + ROW_TILE, :] = h[:, j * LANES:(j + 1) * LANES]
    for t in range(ROW_TILE):
        out_ref[ROW_SUB * t:ROW_SUB * (t + 1), :] = slab_ref[pl.ds(t, ROW_SUB, stride=ROW_PITCH), :]


def _tiles_to_rows(in_ref, slab_ref):
    for t in range(ROW_TILE):
        slab_ref[pl.ds(t, ROW_SUB, stride=ROW_PITCH), :] = in_ref[ROW_SUB * t:ROW_SUB * (t + 1), :]
    return jnp.concatenate([slab_ref[j * ROW_PITCH:j * ROW_PITCH + ROW_TILE, :] for j in range(ROW_SUB)], axis=1)


def _slab_scratch():
    return pltpu.VMEM((ROW_SUB * ROW_PITCH, LANES), F32)


def _ada_kernel(a_ref, w_ref, b_ref, o_ref):
    o_ref[...] = jnp.dot(a_ref[...], w_ref[...], preferred_element_type=F32,
                         precision=lax.Precision.HIGHEST) + b_ref[...]


def _ada_call(a, ada_w, ada_b):
    depth, d, n = ada_w.shape
    rows = a.shape[0]
    nb = n // d
    return pl.pallas_call(
        _ada_kernel,
        grid=(depth, nb),
        in_specs=[pl.BlockSpec((rows, d), lambda l, j: (0, 0)),
                  pl.BlockSpec((None, d, d), lambda l, j: (l, 0, j)),
                  pl.BlockSpec((None, 1, d), lambda l, j: (l, 0, j))],
        out_specs=pl.BlockSpec((None, rows, d), lambda l, j: (l, 0, j)),
        out_shape=jax.ShapeDtypeStruct((depth, rows, n), F32),
        compiler_params=_cparams(("arbitrary", "arbitrary")),
        name="ada_mod",
    )(a, ada_w, ada_b.reshape(depth, 1, n))


def _swap_halves(x, half):
    lane = _lane_iota(x.shape)
    first = (lane & (2 * half - 1)) < half
    return jnp.where(first, pltpu.roll(x, LANES - half, 1), pltpu.roll(x, half, 1))


def _proj_kernel(x_ref, a_ref, sh_ref, w_ref, gains_ref, ca_ref, sa_ref, cd_ref, sd_ref,
                 qa_ref, ka_ref, va_ref, z_ref, cu_ref, qd_ref, kd_ref, vd_ref):
    x = x_ref[...]
    ms = jnp.mean(x * x, axis=-1, keepdims=True)
    h = (x * lax.rsqrt(ms + EPS) * a_ref[...] + sh_ref[...]).astype(BF16)
    p = jnp.dot(h, w_ref[...], preferred_element_type=F32)
    lane = _lane_iota((TILE, LANES))
    ca, sa, cd, sd = ca_ref[...], sa_ref[...], cd_ref[...], sd_ref[...]

    def slab(sec, i):
        o = PROJ_OFF[sec] + i * LANES
        return p[:, o:o + LANES]

    def norm_rope_a(s, gain):
        msq = jnp.sum(s * s, axis=-1, keepdims=True) * (1.0 / HEAD_DIM)
        sn = s * lax.rsqrt(msq + EPS) * gain
        return (sn * ca + _swap_halves(sn, HEAD_DIM // 4) * sa).astype(BF16)

    def norm_rope_d(s, gain):
        s2 = s * s
        m1 = jnp.sum(jnp.where(lane < D_QKDIM, s2, 0.0), axis=-1, keepdims=True)
        m2 = jnp.sum(jnp.where(lane >= D_QKDIM, s2, 0.0), axis=-1, keepdims=True)
        msq = jnp.where(lane < D_QKDIM, m1, m2) * (1.0 / D_QKDIM)
        sn = s * lax.rsqrt(msq + EPS) * gain
        return (sn * cd + _swap_halves(sn, D_QKDIM // 4) * sd).astype(BF16)

    def with_ones(s, width):
        st = s.T
        chan = lax.broadcasted_iota(I32, st.shape, 0)
        return jnp.where(chan == width, 1.0, st).astype(BF16)

    for hh in range(A_HEADS):
        qa_ref[hh] = norm_rope_a(slab(0, hh), gains_ref[0:1, :])
    for hh in range(A_KV_HEADS):
        ka_ref[hh] = norm_rope_a(slab(1, hh), gains_ref[1:2, :])
        va_ref[hh] = with_ones(slab(2, hh), HEAD_DIM)
    bu = p[:, PROJ_OFF[3]:PROJ_OFF[4]]
    bg = p[:, PROJ_OFF[4]:PROJ_OFF[5]]
    z_ref[...] = bu * _sigmoid(bg)
    cu_ref[...] = p[:, PROJ_OFF[5]:PROJ_OFF[6]].astype(BF16)
    for hh in range(D_HEADS):
        qd_ref[hh] = norm_rope_d(slab(6, hh), gains_ref[2:3, :])
        kd_ref[hh] = norm_rope_d(slab(7, hh), gains_ref[3:4, :])
        vd_ref[hh] = with_ones(slab(8, hh), D_VDIM)


def _proj_call(x_all, a1, sh1, w_in_p, gains, rope, dims):
    B, NT, NL = dims
    T = x_all.shape[0]
    S = NT * TILE
    sel = lambda b, i: (jnp.where(i >= NL, B, b), 0, 0)
    row = lambda b, i: (b * NT + i, 0)
    hrow = lambda b, i: (0, b * NT + i, 0)
    tab = lambda b, i: (i, 0)
    head = lambda n: pl.BlockSpec((n, TILE, LANES), hrow)
    hshape = lambda n: jax.ShapeDtypeStruct((n, T, LANES), BF16)
    head_t = lambda n: pl.BlockSpec((n, None, LANES, TILE), lambda b, i: (0, b, 0, i))
    hshape_t = lambda n: jax.ShapeDtypeStruct((n, B, LANES, S), BF16)
    nw = w_in_p.shape[1]
    return pl.pallas_call(
        _proj_kernel,
        grid=(B, NT),
        in_specs=[pl.BlockSpec((TILE, D_MODEL), row),
                  pl.BlockSpec((None, 1, D_MODEL), sel),
                  pl.BlockSpec((None, 1, D_MODEL), sel),
                  pl.BlockSpec((D_MODEL, nw), lambda b, i: (0, 0)),
                  pl.BlockSpec((8, LANES), lambda b, i: (0, 0)),
                  pl.BlockSpec((TILE, LANES), tab), pl.BlockSpec((TILE, LANES), tab),
                  pl.BlockSpec((TILE, LANES), tab), pl.BlockSpec((TILE, LANES), tab)],
        out_specs=[head(A_HEADS), head(A_KV_HEADS), head_t(A_KV_HEADS),
                   pl.BlockSpec((TILE, GROUP_W), row), pl.BlockSpec((TILE, GROUP_W), row),
                   head(D_HEADS), head(D_HEADS), head_t(D_HEADS)],
        out_shape=[hshape(A_HEADS), hshape(A_KV_HEADS), hshape_t(A_KV_HEADS),
                   jax.ShapeDtypeStruct((T, GROUP_W), F32), jax.ShapeDtypeStruct((T, GROUP_W), BF16),
                   hshape(D_HEADS), hshape(D_HEADS), hshape_t(D_HEADS)],
        compiler_params=_cparams(("arbitrary", "arbitrary")),
        name="in_proj",
    )(x_all, a1, sh1, w_in_p, gains, *rope)


def _scores_phase(q, k_ref, k0, nchunks, s_ref, m_ref):
    half = q.shape[0] // 2
    nk = nchunks * KEY_CHUNK
    for h in range(2):
        s_ref[h, 0:nk, :] = lax.dot_general(k_ref[k0:k0 + nk, :], q[h * half:(h + 1) * half],
                                            (((1,), (1,)), ((), ())), preferred_element_type=F32)
    for h in range(2):
        m8 = None
        for c in range(nchunks):
            blk = s_ref[h, c * KEY_CHUNK:(c + 1) * KEY_CHUNK, :]
            part = jnp.max(blk.reshape(KEY_CHUNK // SUBLANES, SUBLANES, half), axis=0)
            m8 = part if m8 is None else jnp.maximum(m8, part)
        m_ref[h] = m8


def _values_phase(vt_ref, k0, nchunks, s_ref, m_ref):
    half = s_ref.shape[2]
    maxes = [jnp.max(m_ref[h], axis=0, keepdims=True) for h in range(2)]
    accs = [jnp.zeros((LANES, half), F32) for _ in range(2)]
    for c in range(nchunks):
        vc = vt_ref[:, k0 + c * KEY_CHUNK:k0 + (c + 1) * KEY_CHUNK]
        for h in range(2):
            pr = jnp.exp2((s_ref[h, c * KEY_CHUNK:(c + 1) * KEY_CHUNK, :] - maxes[h]).astype(BF16))
            accs[h] = accs[h] + jnp.dot(vc, pr, preferred_element_type=F32)
    return jnp.concatenate(accs, axis=1)


def _attend(i, NL, NT, ntq, load_q, k_ref, vt_ref, s_scr, m_scr, a_scr, finish):
    lat = (0, NT * TILE // KEY_CHUNK)
    ctx = (NL * TILE, (NT - NL) * TILE // KEY_CHUNK)

    def case(step, score_keys, value_keys, store):
        w = step % 2
        if store:
            finish(a_scr[w][...])
        if score_keys is not None:
            _scores_phase(load_q(), k_ref, score_keys[0], score_keys[1], s_scr[w], m_scr[w])
        if value_keys is not None:
            a_scr[1 - w][...] = _values_phase(vt_ref, value_keys[0], value_keys[1], s_scr[1 - w], m_scr[1 - w])

    def at(cond, step, score_keys, value_keys, store):
        pl.when(cond)(lambda: case(step, score_keys, value_keys, store))

    at(i == 0, 0, lat, None, False)
    at(i == 1, 1, lat, lat, False)
    for parity in range(2):
        at((i >= 2) & (i < NL) & (lax.rem(i, 2) == parity), parity, lat, lat, True)
    if ntq > NL:
        at(i == NL, NL, ctx, lat, True)
        at(i == NL + 1, NL + 1, None, ctx, True)
        at(i == NL + 2, NL + 2, None, None, True)
    else:
        at(i == NL, NL, None, lat, True)
        at(i == NL + 1, NL + 1, None, None, True)


def _gqa_kernel(q_ref, k_ref, vt_ref, o_ref, s0, s1, m0, m1, a0, a1, *, NL, NT, ntq):
    group = A_HEADS // A_KV_HEADS
    load_q = lambda: q_ref[...].reshape(group * TILE, LANES)

    def finish(acc_t):
        o_t = acc_t * (1.0 / acc_t[HEAD_DIM:HEAD_DIM + 1, :])
        o_ref[...] = o_t.T.reshape(group, TILE, LANES).astype(BF16)

    _attend(pl.program_id(2), NL, NT, ntq, load_q, k_ref, vt_ref, (s0, s1), (m0, m1), (a0, a1), finish)


def _gqa_call(qa, ka, va, dims, ntq):
    B, NT, NL = dims
    T = qa.shape[1]
    S = NT * TILE
    group = A_HEADS // A_KV_HEADS
    qmap = lambda b, g, i: (g, b * NT + jnp.minimum(i, ntq - 1), 0)
    omap = lambda b, g, i: (g, b * NT + jnp.maximum(i - 2, 0), 0)
    kmap = lambda b, g, i: (g, b, 0)
    vmap = lambda b, g, i: (g, b, 0, 0)
    half = group * TILE // 2
    return pl.pallas_call(
        functools.partial(_gqa_kernel, NL=NL, NT=NT, ntq=ntq),
        grid=(B, A_KV_HEADS, ntq + 2),
        in_specs=[pl.BlockSpec((group, TILE, LANES), qmap),
                  pl.BlockSpec((None, S, LANES), kmap),
                  pl.BlockSpec((None, None, LANES, S), vmap)],
        out_specs=pl.BlockSpec((group, TILE, LANES), omap),
        out_shape=jax.ShapeDtypeStruct((A_HEADS, T, LANES), BF16),
        scratch_shapes=([pltpu.VMEM((2, S, half), F32)] * 2 + [pltpu.VMEM((2, SUBLANES, half), F32)] * 2
                        + [pltpu.VMEM((LANES, 2 * half), F32)] * 2),
        compiler_params=_cparams(("arbitrary", "arbitrary", "arbitrary")),
        name="gqa_attn",
    )(qa, ka, va)


def _diff_kernel(q_ref, k_ref, vt_ref, par_ref, o_ref, q2_scr, s0, s1, m0, m1, a0, a1, *, NL, NT, ntq):
    q = q_ref[...]
    lane = _lane_iota(q.shape)
    zero = jnp.zeros_like(q)
    q2_scr[0:TILE, :] = jnp.where(lane < D_QKDIM, q, zero)
    q2_scr[TILE:2 * TILE, :] = jnp.where(lane >= D_QKDIM, q, zero)

    def finish(acc_t):
        o1 = acc_t[:, :TILE] * (1.0 / acc_t[D_VDIM:D_VDIM + 1, :TILE])
        o2 = acc_t[:, TILE:] * (1.0 / acc_t[D_VDIM:D_VDIM + 1, TILE:])
        o = jnp.where(lane < D_VDIM, (o1 - par_ref[0:1, 0:1] * o2).T, 0.0)
        msq = jnp.sum(o * o, axis=-1, keepdims=True) * (1.0 / D_VDIM)
        o_ref[...] = (o * lax.rsqrt(msq + EPS) * par_ref[1:2, :]).astype(BF16)

    _attend(pl.program_id(2), NL, NT, ntq, lambda: q2_scr[...], k_ref, vt_ref, (s0, s1), (m0, m1), (a0, a1), finish)


def _diff_call(qd, kd, vd, dpar, dims, ntq):
    B, NT, NL = dims
    T = qd.shape[1]
    S = NT * TILE
    qmap = lambda b, h, i: (h, b * NT + jnp.minimum(i, ntq - 1), 0)
    omap = lambda b, h, i: (h, b * NT + jnp.maximum(i - 2, 0), 0)
    kmap = lambda b, h, i: (h, b, 0)
    vmap = lambda b, h, i: (h, b, 0, 0)
    return pl.pallas_call(
        functools.partial(_diff_kernel, NL=NL, NT=NT, ntq=ntq),
        grid=(B, D_HEADS, ntq + 2),
        in_specs=[pl.BlockSpec((None, TILE, LANES), qmap),
                  pl.BlockSpec((None, S, LANES), kmap),
                  pl.BlockSpec((None, None, LANES, S), vmap),
                  pl.BlockSpec((8, LANES), lambda b, h, i: (0, 0))],
        out_specs=pl.BlockSpec((None, TILE, LANES), omap),
        out_shape=jax.ShapeDtypeStruct((D_HEADS, T, LANES), BF16),
        scratch_shapes=([pltpu.VMEM((2 * TILE, LANES), BF16)] + [pltpu.VMEM((2, S, TILE), F32)] * 2
                        + [pltpu.VMEM((2, SUBLANES, TILE), F32)] * 2 + [pltpu.VMEM((LANES, 2 * TILE), F32)] * 2),
        compiler_params=_cparams(("arbitrary", "arbitrary", "arbitrary")),
        name="diff_attn",
    )(qd, kd, vd, dpar)


def _conv_kernel(z_ref, w_ref, par_ref, o_ref, zp_ref, sh_ref, *, segs):
    zeros = jnp.zeros((CONV_HALO, GROUP_W), F32)
    pos = 0
    starts = []
    for (r0, n) in segs:
        zp_ref[pos:pos + CONV_HALO, :] = zeros
        zp_ref[pos + CONV_HALO:pos + CONV_HALO + n, :] = z_ref[r0:r0 + n, :]
        starts.append(pos + CONV_HALO)
        pos += CONV_HALO + n
    zp_ref[pos:pos + CONV_HALO, :] = zeros
    bias, ln_g, ln_b = par_ref[0:1, :], par_ref[1:2, :], par_ref[2:3, :]

    for (r0, n), p0 in zip(segs, starts):
        def chunk(j, carry, r0=r0, p0=p0):
            base = pl.multiple_of(j * CONV_ROWS, CONV_ROWS)
            win = zp_ref[pl.ds(base + (p0 - CONV_HALO), CONV_ROWS + 2 * CONV_HALO), :]
            acc = jnp.zeros((CONV_ROWS, GROUP_W), F32)
            first = CONV_HALO - CONV_W // 2
            for res in range(SUBLANES):
                taps = [k for k in range(CONV_W) if (first + k) % SUBLANES == res]
                if not taps:
                    continue
                span = (first + taps[-1]) - res + CONV_ROWS
                sh_ref[res, 0:span, :] = win[res:res + span, :]
                for k in taps:
                    a = first + k - res
                    acc = acc + sh_ref[res, a:a + CONV_ROWS, :] * w_ref[k:k + 1, :]
            u = acc + bias
            mu = jnp.mean(u, axis=-1, keepdims=True)
            uc = u - mu
            var = jnp.mean(uc * uc, axis=-1, keepdims=True)
            y = uc * lax.rsqrt(var + EPS) * ln_g + ln_b
            o_ref[pl.ds(pl.multiple_of(base + r0, CONV_ROWS), CONV_ROWS), :] = (
                y * _sigmoid(y)).astype(BF16)
            return carry

        lax.fori_loop(0, n // CONV_ROWS, chunk, 0)


def _conv_call(z, conv_w, conv_par, dims, with_ctx):
    B, NT, NL = dims
    T = z.shape[0]
    S = NT * TILE
    L = NL * TILE
    segs = ((0, L), (L, S - L)) if with_ctx else ((0, L),)
    pad_rows = sum(n for _, n in segs) + CONV_HALO * (len(segs) + 1)
    return pl.pallas_call(
        functools.partial(_conv_kernel, segs=segs),
        grid=(B,),
        in_specs=[pl.BlockSpec((S, GROUP_W), lambda b: (b, 0)),
                  pl.BlockSpec((32, GROUP_W), lambda b: (0, 0)),
                  pl.BlockSpec((8, GROUP_W), lambda b: (0, 0))],
        out_specs=pl.BlockSpec((S, GROUP_W), lambda b: (b, 0)),
        out_shape=jax.ShapeDtypeStruct((T, GROUP_W), BF16),
        scratch_shapes=[pltpu.VMEM((pad_rows, GROUP_W), F32),
                        pltpu.VMEM((SUBLANES, CONV_ROWS + 2 * CONV_HALO, GROUP_W), F32)],
        compiler_params=_cparams(("arbitrary",)),
        name="conformer_conv",
    )(z, conv_w, conv_par)


def _dft_kernel(c_ref, s_ref, x_ref, cb_ref, sb_ref, o_ref, *, r0, n):
    xs = x_ref[r0:r0 + n, :]
    u = jnp.dot(c_ref[...], xs, preferred_element_type=F32).astype(BF16)
    w = jnp.dot(s_ref[...], xs, preferred_element_type=F32).astype(BF16)
    y = (jnp.dot(u, cb_ref[...], preferred_element_type=F32)
         - jnp.dot(w, sb_ref[...], preferred_element_type=F32))
    o_ref[...] = y.astype(BF16)


def _dft_call(cu, cmat, smat, cbd, sbd, dims, r0, n, prev=None):
    B, NT, NL = dims
    T = cu.shape[0]
    S = NT * TILE
    nr = n // TILE
    t0 = r0 // TILE
    args = [cmat, smat, cu, cbd, sbd]
    in_specs = [pl.BlockSpec((TILE, n), lambda r, b: (r, 0)),
                pl.BlockSpec((TILE, n), lambda r, b: (r, 0)),
                pl.BlockSpec((S, GROUP_W), lambda r, b: (b, 0)),
                pl.BlockSpec((GROUP_W, GROUP_W), lambda r, b: (0, 0)),
                pl.BlockSpec((GROUP_W, GROUP_W), lambda r, b: (0, 0))]
    kern = functools.partial(_dft_kernel, r0=r0, n=n)
    aliases = {}
    if prev is not None:
        args.append(prev)
        in_specs.append(pl.BlockSpec(memory_space=pl.ANY))
        aliases = {5: 0}
        kern = lambda c, s, x, cb, sb, _prev, o: _dft_kernel(c, s, x, cb, sb, o, r0=r0, n=n)
    return pl.pallas_call(
        kern,
        grid=(nr, B),
        in_specs=in_specs,
        out_specs=pl.BlockSpec((TILE, GROUP_W), lambda r, b: (b * NT + t0 + r, 0)),
        out_shape=jax.ShapeDtypeStruct((T, GROUP_W), BF16),
        input_output_aliases=aliases,
        compiler_params=_cparams(("arbitrary", "arbitrary")),
        name="fourier_mix",
    )(*args)


def _out_kernel(x_ref, oa_ref, ob_ref, oc_ref, od_ref, w_ref, g1_ref, a2_ref, sh2_ref, rw_ref, rb_ref,
                xn_ref, h2_ref, ti_ref, gt_ref, cnt_ref, slab_ref, logit_ref, *, n_tiles):
    t = pl.program_id(0)

    @pl.when(t == 0)
    def _():
        cnt_ref[...] = jnp.zeros(cnt_ref.shape, F32)
        _out_project(x_ref, oa_ref, ob_ref, oc_ref, od_ref, w_ref, g1_ref, a2_ref, sh2_ref, rw_ref, rb_ref,
                     xn_ref, h2_ref, slab_ref, logit_ref)

    @pl.when((t >= 1) & (t < n_tiles))
    def _():
        logits = logit_ref[...]
        _out_project(x_ref, oa_ref, ob_ref, oc_ref, od_ref, w_ref, g1_ref, a2_ref, sh2_ref, rw_ref, rb_ref,
                     xn_ref, h2_ref, slab_ref, logit_ref)
        _out_route(logits, ti_ref, gt_ref, cnt_ref)

    @pl.when(t == n_tiles)
    def _():
        _out_route(logit_ref[...], ti_ref, gt_ref, cnt_ref)


def _out_project(x_ref, oa_ref, ob_ref, oc_ref, od_ref, w_ref, g1_ref, a2_ref, sh2_ref, rw_ref, rb_ref,
                 xn_ref, h2_ref, slab_ref, logit_ref):
    ocat = jnp.concatenate([oa_ref[hh] for hh in range(A_HEADS)] + [ob_ref[...], oc_ref[...]]
                           + [od_ref[hh] for hh in range(D_HEADS)], axis=1)
    r = jnp.dot(ocat, w_ref[...], preferred_element_type=F32)
    xn = x_ref[...] + g1_ref[...] * r
    xn_ref[...] = xn
    ms = jnp.mean(xn * xn, axis=-1, keepdims=True)
    h2 = xn * lax.rsqrt(ms + EPS) * a2_ref[...] + sh2_ref[...]
    _rows_to_tiles(h2, h2_ref, slab_ref)
    logit_ref[...] = jnp.dot(h2.astype(BF16), rw_ref[...], preferred_element_type=F32) + rb_ref[...]


def _out_route(logits, ti_ref, gt_ref, cnt_ref):
    lane = _lane_iota(logits.shape)
    lanef = lane.astype(F32)
    vals, idxs = [], []
    cur = logits
    for _ in range(TOP_K):
        mx = jnp.max(cur, axis=-1, keepdims=True)
        idx = jnp.min(jnp.where(cur == mx, lanef, float(LANES)), axis=-1, keepdims=True)
        vals.append(mx)
        idxs.append(idx)
        cur = jnp.where(lanef == idx, NEG_BIG * 2, cur)
    exps = [jnp.exp(v - vals[0]) for v in vals]
    inv = 1.0 / (exps[0] + exps[1] + exps[2] + exps[3])
    ti = jnp.zeros(logits.shape, F32)
    gt = jnp.zeros(logits.shape, F32)
    onehot = jnp.zeros(logits.shape, F32)
    for k in range(TOP_K):
        ti = jnp.where(lane == k, idxs[k], ti)
        gt = jnp.where(lane == k, exps[k] * inv, gt)
        onehot = onehot + jnp.where(lanef == idxs[k], 1.0, 0.0)
    ti_ref[...] = ti.astype(I32)
    gt_ref[...] = gt
    cnt_ref[0:1, :] += jnp.sum(onehot, axis=0, keepdims=True)


def _out_call(x_all, oa, ob, oc, od, w_out_p, g1, a2, sh2, rw, rb, dims, ntq):
    B, NT, NL = dims
    n_tiles = B * ntq
    n_tok = n_tiles * TILE
    cur = lambda t: jnp.minimum(t, n_tiles - 1)
    bat = lambda t: cur(t) // ntq
    pos = lambda t: cur(t) % ntq
    sel = lambda t: (jnp.where(pos(t) >= NL, B, bat(t)), 0, 0)
    row = lambda t: (bat(t) * NT + pos(t), 0)
    hrow = lambda t: (0, bat(t) * NT + pos(t), 0)
    crow = lambda t: (cur(t), 0)
    prow = lambda t: (jnp.maximum(t - 1, 0), 0)
    const = lambda t: (0, 0)
    return pl.pallas_call(
        functools.partial(_out_kernel, n_tiles=n_tiles),
        grid=(n_tiles + 1,),
        in_specs=[pl.BlockSpec((TILE, D_MODEL), row),
                  pl.BlockSpec((A_HEADS, TILE, LANES), hrow),
                  pl.BlockSpec((TILE, GROUP_W), row),
                  pl.BlockSpec((TILE, GROUP_W), row),
                  pl.BlockSpec((D_HEADS, TILE, LANES), hrow),
                  pl.BlockSpec((OUT_K, D_MODEL), const),
                  pl.BlockSpec((None, 1, D_MODEL), sel),
                  pl.BlockSpec((None, 1, D_MODEL), sel),
                  pl.BlockSpec((None, 1, D_MODEL), sel),
                  pl.BlockSpec((D_MODEL, LANES), const),
                  pl.BlockSpec((1, LANES), const)],
        out_specs=[pl.BlockSpec((TILE, D_MODEL), crow),
                   pl.BlockSpec((TILE * ROW_SUB, LANES), crow),
                   pl.BlockSpec((TILE, LANES), prow),
                   pl.BlockSpec((TILE, LANES), prow),
                   pl.BlockSpec((8, LANES), const)],
        out_shape=[jax.ShapeDtypeStruct((n_tok, D_MODEL), F32),
                   jax.ShapeDtypeStruct((n_tok * ROW_SUB, LANES), F32),
                   jax.ShapeDtypeStruct((n_tok, LANES), I32),
                   jax.ShapeDtypeStruct((n_tok, LANES), F32),
                   jax.ShapeDtypeStruct((8, LANES), F32)],
        scratch_shapes=[_slab_scratch(), pltpu.VMEM((TILE, LANES), F32)],
        compiler_params=_cparams(("arbitrary",)),
        name="out_proj_router",
    )(x_all, oa, ob, oc, od, w_out_p, g1, a2, sh2, rw, rb)


def _rank_kernel(ti_ref, ps_ref, tri_ref, dest_ref, carry_ref):
    @pl.when(pl.program_id(0) == 0)
    def _():
        carry_ref[...] = jnp.zeros(carry_ref.shape, F32)

    ti = ti_ref[...]
    lane = _lane_iota(ti.shape)
    hits = [lane == ti[:, k:k + 1] for k in range(TOP_K)]
    onehot = jnp.zeros(ti.shape, F32)
    for hk in hits:
        onehot = onehot + jnp.where(hk, 1.0, 0.0)
    before = jnp.dot(tri_ref[...], onehot.astype(BF16), preferred_element_type=F32)
    base = before + carry_ref[0:1, :] + ps_ref[...]
    dest = jnp.zeros(ti.shape, F32)
    for k, hk in enumerate(hits):
        dk = jnp.sum(jnp.where(hk, base, 0.0), axis=-1, keepdims=True)
        dest = jnp.where(lane == k, dk, dest)
    dest_ref[...] = dest.astype(I32)
    carry_ref[0:1, :] += jnp.sum(onehot, axis=0, keepdims=True)


def _rank_call(topi, pad_start, tri):
    n_tok = topi.shape[0]
    return pl.pallas_call(
        _rank_kernel,
        grid=(n_tok // TILE,),
        in_specs=[pl.BlockSpec((TILE, LANES), lambda t: (t, 0)),
                  pl.BlockSpec((1, LANES), lambda t: (0, 0)),
                  pl.BlockSpec((TILE, TILE), lambda t: (0, 0))],
        out_specs=pl.BlockSpec((TILE, LANES), lambda t: (t, 0)),
        out_shape=jax.ShapeDtypeStruct((n_tok, LANES), I32),
        scratch_shapes=[pltpu.VMEM((8, LANES), F32)],
        compiler_params=_cparams(("arbitrary",)),
        name="slot_rank",
    )(topi, pad_start, tri)


def _tile_copy(src_ref, src_row, dst_ref, dst_row, sem):
    src = src_ref.at[pl.ds(pl.multiple_of(src_row * ROW_SUB, ROW_SUB), ROW_SUB)]
    dst = dst_ref.at[pl.ds(pl.multiple_of(dst_row * ROW_SUB, ROW_SUB), ROW_SUB)]
    return pltpu.make_async_copy(src, dst, sem)


def _dispatch_kernel(fill_start_ref, fill_on_ref, dest_ref, h2_ref, zero_ref, xs_ref, sem):
    @pl.when(pl.program_id(0) == 0)
    def _():
        def fill(e, wait):
            @pl.when(fill_on_ref[e] > 0)
            def _():
                start = pl.multiple_of(fill_start_ref[e] * ROW_SUB, MOE_BLOCK * ROW_SUB)
                cp = pltpu.make_async_copy(zero_ref, xs_ref.at[pl.ds(start, MOE_BLOCK * ROW_SUB)], sem.at[1])
                if wait:
                    cp.wait()
                else:
                    cp.start()

        lax.fori_loop(0, N_EXPERTS, lambda e, c: (fill(e, False), c)[1], 0)
        lax.fori_loop(0, N_EXPERTS, lambda e, c: (fill(e, True), c)[1], 0)

    def issue(r, c):
        for k in range(TOP_K):
            _tile_copy(h2_ref, r, xs_ref, dest_ref[0, 0, TOP_K * r + k], sem.at[0]).start(priority=k % 2)
        return c

    lax.fori_loop(0, TILE, issue, 0, unroll=DMA_UNROLL)
    for k in range(TOP_K):
        pltpu.make_async_copy(h2_ref, xs_ref.at[pl.ds(0, TILE * ROW_SUB)], sem.at[0]).wait()


def _dispatch_call(fill_start, fill_on, dest3, h2t, n_slots):
    n_tok = h2t.shape[0] // ROW_SUB
    zero_blk = jnp.zeros((MOE_BLOCK * ROW_SUB, LANES), F32)
    return pl.pallas_call(
        _dispatch_kernel,
        grid_spec=pltpu.PrefetchScalarGridSpec(
            num_scalar_prefetch=2,
            grid=(n_tok // TILE,),
            in_specs=[pl.BlockSpec((1, 1, TOP_K * TILE), lambda t, fs, fo: (t, 0, 0), memory_space=pltpu.SMEM),
                      pl.BlockSpec((TILE * ROW_SUB, LANES), lambda t, fs, fo: (t, 0)),
                      pl.BlockSpec((MOE_BLOCK * ROW_SUB, LANES), lambda t, fs, fo: (0, 0))],
            out_specs=pl.BlockSpec(memory_space=pl.ANY),
            scratch_shapes=[pltpu.SemaphoreType.DMA((2,))]),
        out_shape=jax.ShapeDtypeStruct((n_slots * ROW_SUB, LANES), F32),
        compiler_params=_cparams(("arbitrary",)),
        name="moe_dispatch",
    )(fill_start, fill_on, dest3, h2t, zero_blk)


def _expert_kernel(blk_e_ref, nact_ref, xs_ref, wgu_ref, bgu_ref, wd_ref, bd_ref, yb_ref,
                   slab_ref, wgu_bf, wd_bf):
    s = pl.program_id(0)
    nact = nact_ref[0]

    @pl.when((s >= 1) & (s <= nact))
    def _():
        x = _tiles_to_rows(xs_ref, slab_ref).astype(BF16)
        gu = jnp.dot(x, wgu_bf[...], preferred_element_type=F32) + bgu_ref[...]
        g = jnp.minimum(gu[:, :D_FF], SWIGLU_LIMIT)
        u = jnp.clip(gu[:, D_FF:], -SWIGLU_LIMIT, SWIGLU_LIMIT)
        act = g * _sigmoid(SWIGLU_ALPHA * g) * (u + 1.0)
        y = jnp.dot(act.astype(BF16), wd_bf[...], preferred_element_type=F32) + bd_ref[...]
        _rows_to_tiles(y, yb_ref, slab_ref)

    here = jnp.minimum(s, pl.num_programs(0) - 2)

    @pl.when((s < nact) & ((s == 0) | (blk_e_ref[here] != blk_e_ref[jnp.maximum(here - 1, 0)])))
    def _():
        wgu_bf[...] = wgu_ref[...].astype(BF16)
        wd_bf[...] = wd_ref[...].astype(BF16)


def _expert_call(blk_e, n_active, xs, w_gu, b_gu, w_down, b_down, li):
    n_slots = xs.shape[0] // ROW_SUB
    n_blocks = n_slots // MOE_BLOCK
    clamp = lambda j, na: jnp.clip(j, 0, na[0] - 1)
    blk = lambda s, be, na: (clamp(s - 1, na), 0)
    wsel = lambda s, be, na: (be[clamp(s - 1, na)], 0, 0)
    wsel4 = lambda s, be, na: (li, be[clamp(s, na)], 0, 0)
    return pl.pallas_call(
        _expert_kernel,
        grid_spec=pltpu.PrefetchScalarGridSpec(
            num_scalar_prefetch=2,
            grid=(n_blocks + 1,),
            in_specs=[pl.BlockSpec((MOE_BLOCK * ROW_SUB, LANES), blk),
                      pl.BlockSpec((None, None, D_MODEL, 2 * D_FF), wsel4),
                      pl.BlockSpec((None, 1, 2 * D_FF), wsel),
                      pl.BlockSpec((None, None, D_FF, D_MODEL), wsel4),
                      pl.BlockSpec((None, 1, D_MODEL), wsel)],
            out_specs=pl.BlockSpec((MOE_BLOCK * ROW_SUB, LANES), blk),
            scratch_shapes=[_slab_scratch(), pltpu.VMEM((D_MODEL, 2 * D_FF), BF16),
                            pltpu.VMEM((D_FF, D_MODEL), BF16)]),
        out_shape=jax.ShapeDtypeStruct((n_slots * ROW_SUB, LANES), F32),
        compiler_params=_cparams(("arbitrary",)),
        name="moe_experts",
    )(blk_e, n_active, xs, w_gu, b_gu, w_down, b_down)


def _combine_kernel(dest_ref, dest_next_ref, x_ref, gt_ref, g2_ref, yb_ref, o_ref, buf0, buf1, slab_ref, sem):
    step = pl.program_id(0) * pl.num_programs(1) + pl.program_id(1)
    last = pl.num_programs(0) * pl.num_programs(1) - 1

    def gather(d_ref, buf, s):
        def issue(r, c):
            for k in range(TOP_K):
                _tile_copy(yb_ref, d_ref[0, 0, TOP_K * r + k], buf.at[k], r, s).start(priority=k % 2)
            return c

        lax.fori_loop(0, TILE, issue, 0, unroll=DMA_UNROLL)

    pl.when(step == 0)(lambda: gather(dest_ref, buf0, sem.at[0]))

    def tile(parity):
        cur, nxt = (buf0, buf1) if parity == 0 else (buf1, buf0)
        pl.when(step < last)(lambda: gather(dest_next_ref, nxt, sem.at[1 - parity]))
        for k in range(TOP_K):
            pltpu.make_async_copy(yb_ref.at[pl.ds(0, TILE * ROW_SUB)], cur.at[k], sem.at[parity]).wait()
        gt = gt_ref[...]
        y = gt[:, 0:1] * _tiles_to_rows(cur.at[0], slab_ref)
        for k in range(1, TOP_K):
            y = y + gt[:, k:k + 1] * _tiles_to_rows(cur.at[k], slab_ref)
        o_ref[...] = x_ref[...] + g2_ref[...] * y

    for parity in range(2):
        pl.when(lax.rem(step, 2) == parity)(functools.partial(tile, parity))


def _combine_call(dest3, xn, gates, g2, yb, dims, ntq):
    B, NT, NL = dims
    n_tok = xn.shape[0]
    sel = lambda b, i: (jnp.where(i >= NL, B, b), 0, 0)
    crow = lambda b, i: (b * ntq + i, 0)
    nxt = lambda b, i: (jnp.minimum(b * ntq + i + 1, B * ntq - 1), 0, 0)
    return pl.pallas_call(
        _combine_kernel,
        grid=(B, ntq),
        in_specs=[pl.BlockSpec((1, 1, TOP_K * TILE), lambda b, i: (b * ntq + i, 0, 0), memory_space=pltpu.SMEM),
                  pl.BlockSpec((1, 1, TOP_K * TILE), nxt, memory_space=pltpu.SMEM),
                  pl.BlockSpec((TILE, D_MODEL), crow),
                  pl.BlockSpec((TILE, LANES), crow),
                  pl.BlockSpec((None, 1, D_MODEL), sel),
                  pl.BlockSpec(memory_space=pl.ANY)],
        out_specs=pl.BlockSpec((TILE, D_MODEL), crow),
        out_shape=jax.ShapeDtypeStruct((n_tok, D_MODEL), F32),
        scratch_shapes=[pltpu.VMEM((TOP_K, TILE * ROW_SUB, LANES), F32),
                        pltpu.VMEM((TOP_K, TILE * ROW_SUB, LANES), F32), _slab_scratch(),
                        pltpu.SemaphoreType.DMA((2,))],
        compiler_params=_cparams(("arbitrary", "arbitrary")),
        name="moe_combine",
    )(dest3, dest3, xn, gates, g2, yb)


def _pad_heads_cols(w, n_heads, width):
    w = w.reshape(w.shape[0], n_heads, width)
    return jnp.pad(w, ((0, 0), (0, 0), (0, LANES - width))).reshape(w.shape[0], n_heads * LANES)


def _pad_heads_rows(w, n_heads, width):
    w = w.reshape(n_heads, width, w.shape[1])
    return jnp.pad(w, ((0, 0), (0, LANES - width), (0, 0))).reshape(n_heads * LANES, w.shape[2])


def _prep_w_in(w):
    splits = (256, 128, 128, 256, 256, 256, 256, 256, 256)
    parts, o = [], 0
    for s in splits:
        parts.append(w[:, o:o + s])
        o += s
    aq, ak, av, bu, bg, cu, dq, dk, dv = parts
    return jnp.concatenate([
        _pad_heads_cols(aq, A_HEADS, HEAD_DIM), _pad_heads_cols(ak, A_KV_HEADS, HEAD_DIM),
        _pad_heads_cols(av, A_KV_HEADS, HEAD_DIM), bu, bg, cu,
        _pad_heads_cols(dq, D_HEADS, 2 * D_QKDIM), _pad_heads_cols(dk, D_HEADS, 2 * D_QKDIM),
        _pad_heads_cols(dv, D_HEADS, D_VDIM)], axis=1).astype(BF16)


def _prep_w_out(w):
    return jnp.concatenate([
        _pad_heads_rows(w[0:GROUP_W], A_HEADS, HEAD_DIM), w[GROUP_W:3 * GROUP_W],
        _pad_heads_rows(w[3 * GROUP_W:], D_HEADS, D_VDIM)], axis=0).astype(BF16)


def _pad_lanes(v):
    return jnp.pad(v.astype(F32), (0, LANES - v.shape[0]))


def _rope_tables(L, C):
    t = jnp.arange(L, dtype=jnp.int32)
    rows = (t // GRID_W).astype(F32)[:, None]
    cols = (t % GRID_W).astype(F32)[:, None]
    lane = jnp.arange(LANES)

    def table(group):
        nfreq = group // 2
        j = lane % nfreq
        inv = jnp.power(ROPE_THETA, -(2.0 * j.astype(F32)) / group)[None, :]
        use_cols = ((lane // group) % 2) == 1
        ang = jnp.where(use_cols[None, :], cols, rows) * inv
        sign = jnp.where((lane % group) < nfreq, -1.0, 1.0)[None, :]
        live = (lane < 2 * D_QKDIM)[None, :]
        cos = jnp.where(live, jnp.cos(ang), 1.0)
        sin = jnp.where(live, jnp.sin(ang) * sign, 0.0)
        cos = jnp.concatenate([cos, jnp.ones((C, LANES), F32)], axis=0)
        sin = jnp.concatenate([sin, jnp.zeros((C, LANES), F32)], axis=0)
        return cos, sin

    ca, sa = table(HEAD_DIM // 2)
    cd, sd = table(D_QKDIM // 2)
    return ca, sa, cd, sd


def _dft_mats(n):
    r = 1 << (int(math.log2(n)) // 2)
    assert n % r == 0
    k = jnp.arange(n, dtype=jnp.int32)[None, :]

    def table(rows, period):
        idx = jnp.arange(rows, dtype=jnp.int32)[:, None]
        ang = ((idx * k) % period).astype(F32) * (2.0 * math.pi / period)
        return jnp.cos(ang), jnp.sin(ang)

    c1, s1 = table(n // r, n // r)
    c2, s2 = table(r, n)
    c1, s1, c2, s2 = c1[:, None, :], s1[:, None, :], c2[None, :, :], s2[None, :, :]
    scale = 1.0 / math.sqrt(n)
    cos = ((c1 * c2 - s1 * s2) * scale).reshape(n, n).astype(BF16)
    sin = ((s1 * c2 + c1 * s2) * scale).reshape(n, n).astype(BF16)
    return cos, sin


def _channel_dft_mats():
    w = GROUP_W // FNET_GROUPS
    k = jnp.arange(GROUP_W, dtype=jnp.int32)
    same = (k[:, None] // w) == (k[None, :] // w)
    prod = ((k[:, None] % w) * (k[None, :] % w)) % w
    ang = prod.astype(F32) * (2.0 * math.pi / w)
    scale = 1.0 / math.sqrt(w)
    return (jnp.where(same, jnp.cos(ang) * scale, 0.0).astype(BF16),
            jnp.where(same, jnp.sin(ang) * scale, 0.0).astype(BF16))


def _slot_plan(counts, n_blocks):
    cnt = counts[0, :N_EXPERTS].astype(I32)
    padded = (cnt + MOE_BLOCK - 1) // MOE_BLOCK * MOE_BLOCK
    pad_ends = jnp.cumsum(padded)
    pad_starts = pad_ends - padded
    n_active = (pad_ends[-1] // MOE_BLOCK).reshape(1).astype(I32)
    blk_start = jnp.arange(n_blocks, dtype=I32) * MOE_BLOCK
    blk_e = jnp.minimum(jnp.sum(pad_ends[None, :] <= blk_start[:, None], axis=1), N_EXPERTS - 1).astype(I32)
    fill_start = jnp.maximum(pad_ends - MOE_BLOCK, 0).astype(I32)
    fill_on = (cnt > 0).astype(I32)
    ps_vec = jnp.pad(pad_starts.astype(F32), (0, LANES - N_EXPERTS)).reshape(1, LANES)
    return ps_vec, blk_e, n_active, fill_start, fill_on


def kernel(x, c, ctx, c_ctx, norm1_g, norm2_g, ada_w, ada_b, w_in, a_qnorm_g, a_knorm_g, conv_dw_w, conv_dw_b,
           conv_ln_g, conv_ln_b, d_qnorm_g, d_knorm_g, d_lambda, d_subln_g, w_out, router_w, router_b,
           exp_w_gu, exp_b_gu, exp_w_down, exp_b_down):
    B, L, D = x.shape
    C = ctx.shape[1]
    depth = ada_w.shape[0]
    assert D == D_MODEL and L % TILE == 0 and C == TILE and L % GRID_W == 0
    NL, NT = L // TILE, (L + C) // TILE
    dims = (B, NT, NL)
    S = NT * TILE

    x_all = jnp.concatenate([x, ctx], axis=1).reshape(B * S, D)

    a = jnp.concatenate([c, c_ctx[None, :]], axis=0)
    a = jnp.pad(a * jax.nn.sigmoid(a), ((0, 16 - (B + 1) % 16 if (B + 1) % 16 else 0), (0, 0)))
    mod_all = _ada_call(a, ada_w, ada_b)

    rope = _rope_tables(L, C)
    cmat, smat = _dft_mats(L)
    cmat_c, smat_c = _dft_mats(C)
    cbd, sbd = _channel_dft_mats()
    tri = jnp.tril(jnp.ones((TILE, TILE), F32), -1).astype(BF16)

    x_lat = None
    for li in range(depth):
        last = li == depth - 1
        ntq = NL if last else NT
        mod = mod_all[li, :B + 1].reshape(B + 1, 6, 1, D)
        sh1, sc1, g1, sh2, sc2, g2 = (mod[:, j] for j in range(6))
        a1 = norm1_g[li][None, None, :] * (1.0 + sc1)
        a2 = norm2_g[li][None, None, :] * (1.0 + sc2)

        gains = jnp.stack([
            _pad_lanes(a_qnorm_g[li]) * (HEAD_DIM ** -0.5 * LOG2E), _pad_lanes(a_knorm_g[li]),
            _pad_lanes(jnp.tile(d_qnorm_g[li], 2)) * (D_QKDIM ** -0.5 * LOG2E), _pad_lanes(jnp.tile(d_knorm_g[li], 2)),
        ] + [jnp.zeros((LANES,), F32)] * 4)
        qa, ka, va, z, cu, qd, kd, vd = _proj_call(x_all, a1, sh1, _prep_w_in(w_in[li]), gains, rope, dims)

        lam_init = 0.8 - 0.6 * math.exp(-0.3 * li)
        lq1, lk1, lq2, lk2 = d_lambda[li].astype(F32)
        lam = jnp.exp(jnp.sum(lq1 * lk1)) - jnp.exp(jnp.sum(lq2 * lk2)) + lam_init
        dpar = jnp.stack([jnp.full((LANES,), lam, F32), _pad_lanes(d_subln_g[li]) * (1.0 - lam_init)]
                         + [jnp.zeros((LANES,), F32)] * 6)

        oa = _gqa_call(qa, ka, va, dims, ntq)
        od = _diff_call(qd, kd, vd, dpar, dims, ntq)
        conv_w = jnp.pad(conv_dw_w[li], ((0, 32 - CONV_W), (0, 0)))
        conv_par = jnp.stack([conv_dw_b[li], conv_ln_g[li], conv_ln_b[li]] + [jnp.zeros((GROUP_W,), F32)] * 5)
        ob = _conv_call(z, conv_w, conv_par, dims, with_ctx=not last)
        oc = _dft_call(cu, cmat, smat, cbd, sbd, dims, 0, L)
        if not last:
            oc = _dft_call(cu, cmat_c, smat_c, cbd, sbd, dims, L, C, prev=oc)

        rw = jnp.pad(router_w[li], ((0, 0), (0, LANES - N_EXPERTS))).astype(BF16)
        rb = jnp.concatenate([router_b[li].astype(F32), jnp.full((LANES - N_EXPERTS,), NEG_BIG, F32)]).reshape(1, LANES)
        xn, h2t, topi, gates, counts = _out_call(x_all, oa, ob, oc, od, _prep_w_out(w_out[li]), g1, a2, sh2,
                                                 rw, rb, dims, ntq)

        n_tok = B * ntq * TILE
        n_blocks = -(-(n_tok * TOP_K + N_EXPERTS * (MOE_BLOCK - 1)) // MOE_BLOCK)
        ps_vec, blk_e, n_active, fill_start, fill_on = _slot_plan(counts, n_blocks)
        dest = _rank_call(topi, ps_vec, tri)
        dest3 = dest[:, :TOP_K].reshape(n_tok // TILE, 1, TOP_K * TILE)
        xs = _dispatch_call(fill_start, fill_on, dest3, h2t, n_blocks * MOE_BLOCK)
        yb = _expert_call(blk_e, n_active, xs, exp_w_gu, exp_b_gu[li][:, None, :],
                          exp_w_down, exp_b_down[li][:, None, :], li)
        x_next = _combine_call(dest3, xn, gates, g2, yb, dims, ntq)
        if last:
            x_lat = x_next
        else:
            x_all = x_next
    return x_lat.reshape(B, L, D)
```
